```python
import math
import jax
import jax.numpy as jnp
from jax import lax
import numpy as np


D_MODEL = 1024
BATCH = 32
SEQ = 2048
DEPTH = 4

GRID_W = 64
CTX_LEN = 256
EPS = 1e-6
F32 = jnp.float32
HEAD_DIM = 64
ATT_WIDTH = D_MODEL // 2
ATT_HEADS = ATT_WIDTH // HEAD_DIM
ATT_KV_HEADS = ATT_HEADS // 4
ATT_GROUP = ATT_HEADS // ATT_KV_HEADS
WINDOW = 128
ATT_BLOCK = 128
ROPE_THETA = 10000.0
ROPE_AXIS_DIM = HEAD_DIM // 2
SSD_WIDTH = D_MODEL // 4
SSD_HEAD_DIM = 64
SSD_HEADS = SSD_WIDTH // SSD_HEAD_DIM
SSD_STATE = 64
SSD_GROUPS = 2
SSD_CONV = 3
SSD_CHUNK = 128
HYENA_WIDTH = D_MODEL // 4
HYENA_GROUPS = 4
HYENA_ORDER = 2
HYENA_CONV = 3
HYENA_POS_DIM = 33
HYENA_FILTER_HIDDEN = 64
HYENA_FAST_DECAY = 0.3
HYENA_SLOW_DECAY = 1.5
HYENA_DECAY_TARGET = 1e-2
MIX_WIDTH = ATT_WIDTH + SSD_WIDTH + HYENA_WIDTH
Q_COLS = ATT_HEADS * HEAD_DIM
KV_COLS = ATT_KV_HEADS * HEAD_DIM
SSD_XBC_COLS = SSD_WIDTH + 2 * SSD_GROUPS * SSD_STATE
SSD_DT_COLS = 2 * SSD_HEADS
HY_COLS = (HYENA_ORDER + 1) * HYENA_WIDTH
PROJ_WIDTH = Q_COLS + 2 * KV_COLS + SSD_WIDTH + SSD_XBC_COLS + SSD_DT_COLS + HY_COLS
FFN_DIM = 256 * ((8 * D_MODEL // 3 + 255) // 256)
N_EXPERTS = 8
MOE_TOP_K = 2
EXPERT_DIM = FFN_DIM

kernel_name = 'hybrid_swa_ssd_hyena_moe_dit'


def _rms(x):
    xf = x.astype(F32)
    return (xf * lax.rsqrt(jnp.mean(xf * xf, axis=-1, keepdims=True) + EPS)).astype(x.dtype)


def rms_norm(x, g):
    return _rms(x) * g


def group_rms_norm(x, g, groups):
    shp = x.shape
    return _rms(x.reshape(shp[:-1] + (groups, shp[-1] // groups))).reshape(shp) * g


def short_conv(u, w, b):
    k = w.shape[0]
    y = lax.conv_general_dilated(u, w[:, None, :], window_strides=(1,), padding=[(k // 2, k // 2)],
                                 dimension_numbers=('NWC', 'WIO', 'NWC'), feature_group_count=u.shape[-1])
    return y + b


def swiglu(t, wg, wu, wd):
    return (jax.nn.silu(t @ wg) * (t @ wu)) @ wd


def moe_swiglu(h, router, wg, wu, wd):
    shp = h.shape
    t = h.reshape(-1, shp[-1])
    logits = (t @ router).astype(F32)
    top_v, top_i = lax.top_k(logits, MOE_TOP_K)
    top_w = jax.nn.softmax(top_v, axis=-1)
    combine = jnp.sum(jax.nn.one_hot(top_i, N_EXPERTS, dtype=F32) * top_w[..., None], axis=1).astype(h.dtype)
    out = jnp.zeros_like(t)
    for e in range(N_EXPERTS):
        out = out + combine[:, e:e + 1] * swiglu(t, wg[e], wu[e], wd[e])
    return out.reshape(shp)


def channel_mixer(h, layer, ffn_w_gate, ffn_w_up, ffn_w_down, moe_router, moe_w_gate, moe_w_up, moe_w_down):
    j = layer // 2
    if layer % 2 == 0:
        return swiglu(h, ffn_w_gate[j], ffn_w_up[j], ffn_w_down[j])
    return moe_swiglu(h, moe_router[j], moe_w_gate[j], moe_w_up[j], moe_w_down[j])


def axial_rope_tables(seq_len):
    rows = seq_len // GRID_W
    row = jnp.repeat(jnp.arange(rows, dtype=F32), GRID_W)
    col = jnp.tile(jnp.arange(GRID_W, dtype=F32), rows)
    inv = ROPE_THETA ** (-jnp.arange(0, ROPE_AXIS_DIM, 2, dtype=F32) / ROPE_AXIS_DIM)
    ang = jnp.stack([row[:, None] * inv, col[:, None] * inv], axis=1)
    return jnp.cos(ang), jnp.sin(ang)


def apply_axial_rope(x, cos, sin):
    xr = x.reshape(x.shape[:-1] + (2, 2, ROPE_AXIS_DIM // 2))
    x1, x2 = xr[..., 0, :], xr[..., 1, :]
    cs = cos[None, :, None].astype(x.dtype)
    sn = sin[None, :, None].astype(x.dtype)
    return jnp.stack([x1 * cs - x2 * sn, x1 * sn + x2 * cs], axis=-2).reshape(x.shape)


def _heads(t, n):
    return t.reshape(t.shape[:-1] + (n, HEAD_DIM))


def latent_window_attention(q, k, v, k_ctx, v_ctx, sink):
    b, seq_len = q.shape[:2]
    band = ATT_BLOCK + 2 * WINDOW
    pad = ((0, 0), (WINDOW, WINDOW), (0, 0), (0, 0))
    kp = jnp.pad(k, pad)
    vp = jnp.pad(v, pad)
    rel = jnp.arange(band)[None, :] - WINDOW - jnp.arange(ATT_BLOCK)[:, None]

    def one_block(i):
        start = i * ATT_BLOCK
        qb = lax.dynamic_slice_in_dim(q, start, ATT_BLOCK, axis=1)
        kb = lax.dynamic_slice_in_dim(kp, start, band, axis=1)
        vb = lax.dynamic_slice_in_dim(vp, start, band, axis=1)
        kpos = start - WINDOW + jnp.arange(band)
        valid = (jnp.abs(rel) <= WINDOW) & ((kpos >= 0) & (kpos < seq_len))[None, :]
        s_loc = jnp.where(valid, jnp.einsum('bqhgd,bkhd->bhgqk', qb, kb).astype(F32), -jnp.inf)
        s_ctx = jnp.einsum('bqhgd,bkhd->bhgqk', qb, k_ctx).astype(F32)
        s_snk = jnp.broadcast_to(sink[None, :, :, None, None], s_ctx.shape[:-1] + (1,))
        p = jax.nn.softmax(jnp.concatenate([s_loc, s_ctx, s_snk], axis=-1), axis=-1).astype(v.dtype)
        return (jnp.einsum('bhgqk,bkhd->bqhgd', p[..., :band], vb)
                + jnp.einsum('bhgqk,bkhd->bqhgd', p[..., band:-1], v_ctx))

    out = lax.map(one_block, jnp.arange(seq_len // ATT_BLOCK))
    return jnp.moveaxis(out, 0, 1).reshape(b, seq_len, ATT_WIDTH)


def context_attention(q, k, v, sink):
    b, ctx_len = q.shape[:2]
    s = jnp.einsum('bqhgd,bkhd->bhgqk', q, k).astype(F32)
    s_snk = jnp.broadcast_to(sink[None, :, :, None, None], s.shape[:-1] + (1,))
    p = jax.nn.softmax(jnp.concatenate([s, s_snk], axis=-1), axis=-1).astype(v.dtype)
    return jnp.einsum('bhgqk,bkhd->bqhgd', p[..., :-1], v).reshape(b, ctx_len, ATT_WIDTH)


def ssd_prep(xbc, dt_raw, lp):
    xbc = jax.nn.silu(short_conv(xbc, lp['ssd_conv_w'], lp['ssd_conv_b']))
    xs, bm, cm = jnp.split(xbc, [SSD_WIDTH, SSD_WIDTH + SSD_GROUPS * SSD_STATE], axis=-1)
    b, seq_len = xs.shape[:2]
    rep = SSD_HEADS // SSD_GROUPS
    bh = jnp.repeat(bm.reshape(b, seq_len, SSD_GROUPS, SSD_STATE), rep, axis=2)
    ch = jnp.repeat(cm.reshape(b, seq_len, SSD_GROUPS, SSD_STATE), rep, axis=2)
    dt = jax.nn.softplus(dt_raw.astype(F32).reshape(b, seq_len, 2, SSD_HEADS) + lp['ssd_dt_bias'].astype(F32))
    return xs.reshape(b, seq_len, SSD_HEADS, SSD_HEAD_DIM), bh, ch, dt


def ssd_chunked(xh, dt, a, bh, ch, h0, want_y):
    b, seq_len, nh, hp = xh.shape
    nc = seq_len // SSD_CHUNK
    shp = (b, nc, SSD_CHUNK)
    xdt = (xh.astype(F32) * dt[..., None]).reshape(shp + (nh, hp))
    bc = bh.astype(F32).reshape(shp + (nh, SSD_STATE))
    la = jnp.cumsum((dt * a).reshape(shp + (nh,)), axis=2)
    w_end = jnp.exp(la[:, :, -1:] - la)
    states = jnp.einsum('bcshn,bcshp->bchpn', bc, xdt * w_end[..., None])

    def step(h, inp):
        s, dec = inp
        return h * dec[:, :, None, None] + s, h

    h_last, h_start = lax.scan(step, h0, (jnp.moveaxis(states, 1, 0), jnp.moveaxis(jnp.exp(la[:, :, -1]), 1, 0)))
    if not want_y:
        return None, h_last
    cc = ch.astype(F32).reshape(shp + (nh, SSD_STATE))
    seg = la[:, :, :, None] - la[:, :, None, :]
    lower = jnp.tril(jnp.ones((SSD_CHUNK, SSD_CHUNK), bool))[None, None, :, :, None]
    decay = jnp.exp(jnp.where(lower, seg, -jnp.inf))
    scores = jnp.einsum('bclhn,bcshn->bclsh', cc, bc)
    y_diag = jnp.einsum('bclsh,bcshp->bclhp', scores * decay, xdt)
    y_off = jnp.einsum('bclhn,bchpn->bclhp', cc, jnp.moveaxis(h_start, 0, 1)) * jnp.exp(la)[..., None]
    return (y_diag + y_off).reshape(b, seq_len, nh, hp), h_last


def _flip(t, rev):
    return jnp.flip(t, axis=1) if rev else t


def ssd_bidirectional(lat, ctx, a_log, d_skip, want_ctx):
    a = -jnp.exp(a_log.astype(F32))
    xl, bl, cl, dl = lat
    xc, bc, cc, dc = ctx
    h0 = jnp.zeros((xl.shape[0], SSD_HEADS, SSD_HEAD_DIM, SSD_STATE), F32)
    dsk = d_skip.astype(F32)[:, None]
    y_lat = xl.astype(F32) * dsk
    y_ctx = xc.astype(F32) * dsk if want_ctx else None
    for direction in range(2):
        rev = direction == 1
        yc, hc = ssd_chunked(_flip(xc, rev), _flip(dc[:, :, direction], rev), a[direction],
                             _flip(bc, rev), _flip(cc, rev), h0, want_ctx)
        yl, _ = ssd_chunked(_flip(xl, rev), _flip(dl[:, :, direction], rev), a[direction],
                            _flip(bl, rev), _flip(cl, rev), hc, True)
        y_lat = y_lat + _flip(yl, rev)
        if want_ctx:
            y_ctx = y_ctx + _flip(yc, rev)
    return y_lat, y_ctx


def hyena_filters(seq_len, lp):
    t = jnp.linspace(0.0, 1.0, seq_len, dtype=F32)[:, None]
    w = (2.0 * math.pi / seq_len) * jnp.arange(seq_len, dtype=F32)[:, None]
    bands = (HYENA_POS_DIM - 1) // 2
    freqs = jnp.linspace(1e-4, bands - 1, bands, dtype=F32)[None, :]
    z = jnp.concatenate([t, jnp.cos(freqs * w), -jnp.sin(freqs * w)], axis=-1)
    h = jnp.sin(lp['hy_f1'].astype(F32) * (z @ lp['hy_w1'].astype(F32) + lp['hy_b1'].astype(F32)))
    h = jnp.sin(lp['hy_f2'].astype(F32) * (h @ lp['hy_w2'].astype(F32) + lp['hy_b2'].astype(F32)))
    h = (h @ lp['hy_w3'].astype(F32)).reshape(seq_len, 2, HYENA_ORDER, HYENA_WIDTH)
    deltas = jnp.abs(jnp.linspace(math.log(HYENA_DECAY_TARGET) / HYENA_SLOW_DECAY,
                                  math.log(HYENA_DECAY_TARGET) / HYENA_FAST_DECAY, HYENA_WIDTH, dtype=F32))
    h = h * jnp.exp(-t * deltas)[:, None, None, :]
    k2 = jnp.concatenate([h[:, 0], jnp.zeros((1, HYENA_ORDER, HYENA_WIDTH), F32),
                          jnp.flip(h[1:, 1], axis=0)], axis=0)
    return jnp.fft.rfft(k2, axis=0)


def long_conv(u, kf, bias):
    seq_len = u.shape[1]
    uf = u.astype(F32)
    y = jnp.fft.irfft(jnp.fft.rfft(uf, n=2 * seq_len, axis=1) * kf[None], n=2 * seq_len, axis=1)[:, :seq_len]
    return (y + uf * bias.astype(F32)).astype(u.dtype)


def hyena_operator(u, lp):
    kf = hyena_filters(u.shape[1], lp)
    u = short_conv(u, lp['hy_conv_w'], lp['hy_conv_b'])
    v, x1, x2 = jnp.split(u, 3, axis=-1)
    z = x1 * long_conv(v, kf[:, 0], lp['hy_bias'][0])
    return x2 * long_conv(z, kf[:, 1], lp['hy_bias'][1])


def split_proj(p):
    widths = (Q_COLS, KV_COLS, KV_COLS, SSD_WIDTH, SSD_XBC_COLS, SSD_DT_COLS)
    cuts = []
    total = 0
    for wdt in widths:
        total += wdt
        cuts.append(total)
    return jnp.split(p, cuts, axis=-1)


def merge_groups(att, ssd_y, z, hy, lp):
    ssd_out = rms_norm(ssd_y.reshape(z.shape).astype(z.dtype) * jax.nn.silu(z), lp['ssd_norm'])
    y = jnp.concatenate([rms_norm(att, lp['att_out_norm']), ssd_out,
                         group_rms_norm(hy, lp['hy_out_norm'], HYENA_GROUPS)], axis=-1)
    return y @ lp['w_out']


def hybrid_mixer(hx, hc, rope_cos, rope_sin, lp, ctx_out):
    b, seq_len = hx.shape[:2]
    ctx_len = hc.shape[1]
    q_x, k_x, v_x, z_x, xbc_x, dt_x, hy_x = split_proj(hx @ lp['w_in'])
    q_c, k_c, v_c, z_c, xbc_c, dt_c, hy_c = split_proj(hc @ lp['w_in'])
    scale = HEAD_DIM ** -0.5
    sink = lp['att_sinks'].astype(F32).reshape(ATT_KV_HEADS, ATT_GROUP)
    q_l = apply_axial_rope(rms_norm(_heads(q_x, ATT_HEADS), lp['q_norm']), rope_cos, rope_sin) * scale
    k_l = apply_axial_rope(rms_norm(_heads(k_x, ATT_KV_HEADS), lp['k_norm']), rope_cos, rope_sin)
    k_cx = rms_norm(_heads(k_c, ATT_KV_HEADS), lp['k_norm'])
    v_cx = _heads(v_c, ATT_KV_HEADS)
    att_l = latent_window_attention(q_l.reshape(b, seq_len, ATT_KV_HEADS, ATT_GROUP, HEAD_DIM), k_l,
                                    _heads(v_x, ATT_KV_HEADS), k_cx, v_cx, sink)
    ssd_l, ssd_c = ssd_bidirectional(ssd_prep(xbc_x, dt_x, lp), ssd_prep(xbc_c, dt_c, lp),
                                     lp['ssd_a_log'], lp['ssd_d'], ctx_out)
    y_l = merge_groups(att_l, ssd_l, z_x, hyena_operator(hy_x, lp), lp)
    if not ctx_out:
        return y_l, None
    q_cx = rms_norm(_heads(q_c, ATT_HEADS), lp['q_norm']) * scale
    att_c = context_attention(q_cx.reshape(b, ctx_len, ATT_KV_HEADS, ATT_GROUP, HEAD_DIM), k_cx, v_cx, sink)
    y_c = merge_groups(att_c, ssd_c, z_c, hyena_operator(hy_c, lp), lp)
    return y_l, y_c


def setup_inputs(seed: int = 0) -> dict:
    key = jax.random.key(seed)
    ks = iter(jax.random.split(key, 48))

    def nrm(shape, scale):
        return jax.random.normal(next(ks), shape, F32) * scale

    def gain(shape):
        return 1.0 + 0.05 * jax.random.normal(next(ks), shape, F32)

    n_dense = (DEPTH + 1) // 2
    n_moe = DEPTH // 2
    dt0 = jnp.exp(jax.random.uniform(next(ks), (DEPTH, 2, SSD_HEADS), F32) * (math.log(0.1) - math.log(1e-3)) + math.log(1e-3))
    return {
        'x': nrm((BATCH, SEQ, D_MODEL), 1.0),
        'c': nrm((BATCH, D_MODEL), 1.0),
        'ctx': nrm((BATCH, CTX_LEN, D_MODEL), 1.0),
        'c_ctx': nrm((D_MODEL,), 1.0),
        'w_ada': nrm((DEPTH, D_MODEL, 6 * D_MODEL), 0.5 * D_MODEL ** -0.5),
        'b_ada': nrm((DEPTH, 6 * D_MODEL), 0.02),
        'norm1': gain((DEPTH, D_MODEL)),
        'norm2': gain((DEPTH, D_MODEL)),
        'w_in': nrm((DEPTH, D_MODEL, PROJ_WIDTH), D_MODEL ** -0.5),
        'w_out': nrm((DEPTH, MIX_WIDTH, D_MODEL), MIX_WIDTH ** -0.5),
        'q_norm': gain((DEPTH, HEAD_DIM)),
        'k_norm': gain((DEPTH, HEAD_DIM)),
        'att_sinks': nrm((DEPTH, ATT_HEADS), 0.5),
        'att_out_norm': gain((DEPTH, ATT_WIDTH)),
        'ssd_conv_w': nrm((DEPTH, SSD_CONV, SSD_XBC_COLS), SSD_CONV ** -0.5),
        'ssd_conv_b': nrm((DEPTH, SSD_XBC_COLS), 0.02),
        'ssd_dt_bias': dt0 + jnp.log(-jnp.expm1(-dt0)),
        'ssd_a_log': jnp.log(jax.random.uniform(next(ks), (DEPTH, 2, SSD_HEADS), F32, 1.0, 16.0)),
        'ssd_d': gain((DEPTH, SSD_HEADS)),
        'ssd_norm': gain((DEPTH, SSD_WIDTH)),
        'hy_conv_w': nrm((DEPTH, HYENA_CONV, HY_COLS), HYENA_CONV ** -0.5),
        'hy_conv_b': nrm((DEPTH, HY_COLS), 0.02),
        'hy_w1': nrm((DEPTH, HYENA_POS_DIM, HYENA_FILTER_HIDDEN), HYENA_POS_DIM ** -0.5),
        'hy_b1': nrm((DEPTH, HYENA_FILTER_HIDDEN), 0.1),
        'hy_f1': gain((DEPTH, HYENA_FILTER_HIDDEN)),
        'hy_w2': nrm((DEPTH, HYENA_FILTER_HIDDEN, HYENA_FILTER_HIDDEN), HYENA_FILTER_HIDDEN ** -0.5),
        'hy_b2': nrm((DEPTH, HYENA_FILTER_HIDDEN), 0.1),
        'hy_f2': gain((DEPTH, HYENA_FILTER_HIDDEN)),
        'hy_w3': nrm((DEPTH, HYENA_FILTER_HIDDEN, 2 * HYENA_ORDER * HYENA_WIDTH), 0.02),
        'hy_bias': nrm((DEPTH, HYENA_ORDER, HYENA_WIDTH), 0.1),
        'hy_out_norm': gain((DEPTH, HYENA_WIDTH)),
        'ffn_w_gate': nrm((n_dense, D_MODEL, FFN_DIM), D_MODEL ** -0.5),
        'ffn_w_up': nrm((n_dense, D_MODEL, FFN_DIM), D_MODEL ** -0.5),
        'ffn_w_down': nrm((n_dense, FFN_DIM, D_MODEL), FFN_DIM ** -0.5),
        'moe_router': nrm((n_moe, D_MODEL, N_EXPERTS), D_MODEL ** -0.5),
        'moe_w_gate': nrm((n_moe, N_EXPERTS, D_MODEL, EXPERT_DIM), D_MODEL ** -0.5),
        'moe_w_up': nrm((n_moe, N_EXPERTS, D_MODEL, EXPERT_DIM), D_MODEL ** -0.5),
        'moe_w_down': nrm((n_moe, N_EXPERTS, EXPERT_DIM, D_MODEL), EXPERT_DIM ** -0.5),
    }


def reference(x, c, ctx, c_ctx, w_ada, b_ada, norm1, norm2, w_in, w_out, q_norm, k_norm, att_sinks,
              att_out_norm, ssd_conv_w, ssd_conv_b, ssd_dt_bias, ssd_a_log, ssd_d, ssd_norm, hy_conv_w,
              hy_conv_b, hy_w1, hy_b1, hy_f1, hy_w2, hy_b2, hy_f2, hy_w3, hy_bias, hy_out_norm, ffn_w_gate,
              ffn_w_up, ffn_w_down, moe_router, moe_w_gate, moe_w_up, moe_w_down):
    seq_len = x.shape[1]
    rope_cos, rope_sin = axial_rope_tables(seq_len)
    silu_c = jax.nn.silu(c)
    silu_cc = jax.nn.silu(c_ctx)
    xc = ctx
    for i in range(DEPTH):
        last = i == DEPTH - 1
        lp = {
            'w_in': w_in[i], 'w_out': w_out[i], 'q_norm': q_norm[i], 'k_norm': k_norm[i],
            'att_sinks': att_sinks[i], 'att_out_norm': att_out_norm[i],
            'ssd_conv_w': ssd_conv_w[i], 'ssd_conv_b': ssd_conv_b[i], 'ssd_dt_bias': ssd_dt_bias[i],
            'ssd_a_log': ssd_a_log[i], 'ssd_d': ssd_d[i], 'ssd_norm': ssd_norm[i],
            'hy_conv_w': hy_conv_w[i], 'hy_conv_b': hy_conv_b[i], 'hy_w1': hy_w1[i], 'hy_b1': hy_b1[i],
            'hy_f1': hy_f1[i], 'hy_w2': hy_w2[i], 'hy_b2': hy_b2[i], 'hy_f2': hy_f2[i], 'hy_w3': hy_w3[i],
            'hy_bias': hy_bias[i], 'hy_out_norm': hy_out_norm[i],
        }
        mod = (silu_c @ w_ada[i] + b_ada[i]).reshape(c.shape[0], 6, 1, D_MODEL)
        modc = (silu_cc @ w_ada[i] + b_ada[i]).reshape(6, D_MODEL)
        hx = rms_norm(x, norm1[i]) * (1 + mod[:, 1]) + mod[:, 0]
        hc = rms_norm(xc, norm1[i]) * (1 + modc[1]) + modc[0]
        mix_x, mix_c = hybrid_mixer(hx, hc, rope_cos, rope_sin, lp, not last)
        x = x + mod[:, 2] * mix_x
        x = x + mod[:, 5] * channel_mixer(rms_norm(x, norm2[i]) * (1 + mod[:, 4]) + mod[:, 3], i,
                                          ffn_w_gate, ffn_w_up, ffn_w_down, moe_router, moe_w_gate, moe_w_up, moe_w_down)
        if not last:
            xc = xc + modc[2] * mix_c
            xc = xc + modc[5] * channel_mixer(rms_norm(xc, norm2[i]) * (1 + modc[4]) + modc[3], i,
                                              ffn_w_gate, ffn_w_up, ffn_w_down, moe_router, moe_w_gate, moe_w_up, moe_w_down)
    return x
```

```python
import functools
import math

import jax
import jax.numpy as jnp
from jax import lax
from jax.experimental import pallas as pl
from jax.experimental.pallas import tpu as pltpu

F32 = jnp.float32
BF16 = jnp.bfloat16

D_MODEL = 1024
DEPTH = 4
GRID_W = 64
EPS = 1e-6
HEAD_DIM = 64
ATT_WIDTH = 512
ATT_HEADS = 8
ATT_KV_HEADS = 2
ATT_GROUP = 4
WINDOW = 128
ATT_BLOCK = 128
ROPE_THETA = 10000.0
ROPE_AXIS_DIM = 32
SSD_WIDTH = 256
SSD_HEAD_DIM = 64
SSD_HEADS = 4
SSD_STATE = 64
SSD_GROUPS = 2
SSD_CHUNK = 128
HYENA_WIDTH = 256
HYENA_GROUPS = 4
HYENA_ORDER = 2
HYENA_POS_DIM = 33
HYENA_FAST_DECAY = 0.3
HYENA_SLOW_DECAY = 1.5
HYENA_DECAY_TARGET = 1e-2
Q_COLS = 512
KV_COLS = 128
SSD_XBC_COLS = 512
SSD_DT_COLS = 8
HY_COLS = 768
FFN_DIM = 2816
N_EXPERTS = 8
FFN_CHUNK = 256
N_FFN_CHUNKS = FFN_DIM // FFN_CHUNK
LANES = 128
QKV_W = Q_COLS + 2 * KV_COLS
ZX_W = SSD_WIDTH + SSD_XBC_COLS
PROJ_PAD = QKV_W + ZX_W + HY_COLS + LANES
VMEM_LIMIT = 56 * 1024 * 1024
TOKEN_TILE = 512
EXPERT_TILE = 512


def _cparams(n_axes):
    return pltpu.CompilerParams(dimension_semantics=("arbitrary",) * n_axes,
                                vmem_limit_bytes=VMEM_LIMIT)


def _silu(v):
    return v / (1.0 + jnp.exp(-v))


def _modnorm(x, g, scale, shift):
    ms = jnp.mean(x * x, axis=-1, keepdims=True)
    return x * lax.rsqrt(ms + EPS) * g * (1.0 + scale) + shift


def _segsum(t, bd):
    hi = t.astype(BF16)
    lo = (t - hi.astype(F32)).astype(BF16)
    return (jnp.dot(hi, bd, preferred_element_type=F32)
            + jnp.dot(lo, bd, preferred_element_type=F32))


def _mod_index(tiles_per_batch, n_batch):
    return lambda i: (jnp.minimum(i // tiles_per_batch, n_batch), 0, 0)


def _adaln_kernel(c_ref, w_ref, b_ref, o_ref):
    s = _silu(c_ref[...]).astype(BF16)
    o_ref[...] = jnp.dot(s, w_ref[...].astype(BF16), preferred_element_type=F32) + b_ref[...]


def _adaln(cc, w_ada, b_ada):
    depth, d, n = w_ada.shape
    r = cc.shape[0]
    tn = 512
    return pl.pallas_call(
        _adaln_kernel,
        grid=(depth, n // tn),
        in_specs=[pl.BlockSpec((r, d), lambda l, j: (0, 0)),
                  pl.BlockSpec((None, d, tn), lambda l, j: (l, 0, j)),
                  pl.BlockSpec((None, 1, tn), lambda l, j: (l, 0, j))],
        out_specs=pl.BlockSpec((None, r, tn), lambda l, j: (l, 0, j)),
        out_shape=jax.ShapeDtypeStruct((depth, r, n), F32),
        compiler_params=_cparams(2),
        name="adaln",
    )(cc, w_ada, b_ada.reshape(depth, 1, n))


def _inproj_kernel(x_ref, mod_ref, g1_ref, w_ref, cos_ref, sin_ref, qg_ref, kg_ref, bd_ref,
                   q_ref, k_ref, v_ref, zx_ref, hy_ref, dt_ref):
    x = x_ref[...]
    h = _modnorm(x, g1_ref[...], mod_ref[0, 1:2, :], mod_ref[0, 0:1, :]).astype(BF16)
    pq = jnp.dot(h, w_ref[:, 0:QKV_W], preferred_element_type=F32)
    cos = cos_ref[...]
    sin = sin_ref[...]
    lane = lax.broadcasted_iota(jnp.int32, (1, LANES), 1)
    first_half = (lane % 32) < 16

    def rope(t):
        partner = jnp.where(first_half, pltpu.roll(t, LANES - 16, 1), pltpu.roll(t, 16, 1))
        return t * cos + partner * sin

    q = pq[:, 0:Q_COLS]
    qn = q * lax.rsqrt(_segsum(q * q, bd_ref[...]) * (1.0 / HEAD_DIM) + EPS) * qg_ref[...]
    scale = HEAD_DIM ** -0.5
    for j in range(Q_COLS // LANES):
        q_ref[:, LANES * j:LANES * (j + 1)] = (rope(qn[:, LANES * j:LANES * (j + 1)]) * scale).astype(BF16)
    k = pq[:, Q_COLS:Q_COLS + KV_COLS]
    kn = k * lax.rsqrt(_segsum(k * k, bd_ref[0:KV_COLS, 0:KV_COLS]) * (1.0 / HEAD_DIM) + EPS) * kg_ref[...]
    k_ref[...] = rope(kn).astype(BF16)
    v_ref[...] = pq[:, Q_COLS + KV_COLS:QKV_W].astype(BF16)
    zx_ref[...] = jnp.dot(h, w_ref[:, QKV_W:QKV_W + ZX_W], preferred_element_type=F32)
    hy_ref[...] = jnp.dot(h, w_ref[:, QKV_W + ZX_W:QKV_W + ZX_W + HY_COLS], preferred_element_type=F32)
    dt_ref[...] = jnp.dot(h, w_ref[:, QKV_W + ZX_W + HY_COLS:PROJ_PAD], preferred_element_type=F32)


def _inproj(xa, mods, g1, w_cat, cos_t, sin_t, qg, kg, bd, n_batch, seq_len):
    t, d = xa.shape
    tm = TOKEN_TILE
    tpb = seq_len // tm
    n_lat = n_batch * tpb
    rope_idx = lambda i: (jnp.where(i < n_lat, i % tpb, tpb), 0)
    row = lambda w: pl.BlockSpec((tm, w), lambda i: (i, 0))
    const = lambda a: pl.BlockSpec(a.shape, lambda i: (0,) * a.ndim)
    return pl.pallas_call(
        _inproj_kernel,
        grid=(t // tm,),
        in_specs=[row(d),
                  pl.BlockSpec((1, 6, d), _mod_index(tpb, n_batch)),
                  const(g1), const(w_cat),
                  pl.BlockSpec((tm, LANES), rope_idx), pl.BlockSpec((tm, LANES), rope_idx),
                  const(qg), const(kg), const(bd)],
        out_specs=[row(Q_COLS), row(KV_COLS), row(KV_COLS), row(ZX_W), row(HY_COLS), row(LANES)],
        out_shape=[jax.ShapeDtypeStruct((t, Q_COLS), BF16),
                   jax.ShapeDtypeStruct((t, KV_COLS), BF16),
                   jax.ShapeDtypeStruct((t, KV_COLS), BF16),
                   jax.ShapeDtypeStruct((t, ZX_W), F32),
                   jax.ShapeDtypeStruct((t, HY_COLS), F32),
                   jax.ShapeDtypeStruct((t, LANES), F32)],
        compiler_params=_cparams(1),
        name="inproj",
    )(xa, mods, g1, w_cat, cos_t, sin_t, qg, kg, bd)


def _attn_kernel(sink_ref, q_ref, *refs, n_q, band):
    if band:
        k_ref, v_ref, kc_ref, vc_ref, o_ref = refs
        seq_len = k_ref.shape[0]
    else:
        kc_ref, vc_ref, o_ref = refs
    qb = ATT_BLOCK
    rows = ATT_GROUP * qb
    row_id = lax.broadcasted_iota(jnp.int32, (rows, 1), 0)
    if band:
        rel0 = (lax.broadcasted_iota(jnp.int32, (rows, band), 1)
                - lax.broadcasted_iota(jnp.int32, (rows, band), 0) % qb)
    kc = kc_ref[...]
    vc = vc_ref[...]
    nt = (((1,), (1,)), ((), ()))

    def block(i, carry):
        q0 = pl.multiple_of(i * qb, qb)
        qs = q_ref[pl.ds(q0, qb), :]
        if band:
            k0 = pl.multiple_of(jnp.clip(q0 - WINDOW, 0, seq_len - band), qb)
            kb = k_ref[pl.ds(k0, band), :]
            vb = v_ref[pl.ds(k0, band), :]
            valid = jnp.abs(rel0 + (k0 - q0)) <= WINDOW
        for j in range(ATT_KV_HEADS):
            hs = slice(j * HEAD_DIM, (j + 1) * HEAD_DIM)
            qh = jnp.concatenate(
                [qs[:, (ATT_GROUP * j + g) * HEAD_DIM:(ATT_GROUP * j + g + 1) * HEAD_DIM]
                 for g in range(ATT_GROUP)], axis=0)
            snk = jnp.zeros((rows, 1), F32)
            for g in range(ATT_GROUP):
                snk = jnp.where(row_id // qb == g, sink_ref[ATT_GROUP * j + g], snk)
            s_ctx = lax.dot_general(qh, kc[:, hs], nt, preferred_element_type=F32)
            m = jnp.maximum(jnp.max(s_ctx, axis=-1, keepdims=True), snk)
            if band:
                s_loc = lax.dot_general(qh, kb[:, hs], nt, preferred_element_type=F32)
                s_loc = jnp.where(valid, s_loc, -jnp.inf)
                m = jnp.maximum(m, jnp.max(s_loc, axis=-1, keepdims=True))
            p_ctx = jnp.exp(s_ctx - m)
            den = jnp.sum(p_ctx, axis=-1, keepdims=True) + jnp.exp(snk - m)
            o = jnp.dot(p_ctx.astype(BF16), vc[:, hs], preferred_element_type=F32)
            if band:
                p_loc = jnp.exp(s_loc - m)
                den = den + jnp.sum(p_loc, axis=-1, keepdims=True)
                o = o + jnp.dot(p_loc.astype(BF16), vb[:, hs], preferred_element_type=F32)
            o = o / den
            for g in range(ATT_GROUP):
                c0 = (ATT_GROUP * j + g) * HEAD_DIM
                o_ref[pl.ds(q0, qb), c0:c0 + HEAD_DIM] = o[g * qb:(g + 1) * qb, :]
        return carry

    lax.fori_loop(0, n_q, block, 0)


def _attention(sinks, q, k, v, n_batch, seq_len, ctx_len, latent):
    ctx_blk0 = n_batch * seq_len // ctx_len
    ctx_spec = pl.BlockSpec((ctx_len, KV_COLS), lambda b: (ctx_blk0 + b, 0))
    smem = pl.BlockSpec(memory_space=pltpu.SMEM)
    if latent:
        rows = seq_len
        band = ATT_BLOCK + 2 * WINDOW
        in_specs = [smem, pl.BlockSpec((rows, Q_COLS), lambda b: (b, 0)),
                    pl.BlockSpec((rows, KV_COLS), lambda b: (b, 0)),
                    pl.BlockSpec((rows, KV_COLS), lambda b: (b, 0)), ctx_spec, ctx_spec]
        args = (sinks, q, k, v, k, v)
    else:
        rows = ctx_len
        band = 0
        in_specs = [smem, pl.BlockSpec((rows, Q_COLS), lambda b: (ctx_blk0 + b, 0)), ctx_spec, ctx_spec]
        args = (sinks, q, k, v)
    return pl.pallas_call(
        functools.partial(_attn_kernel, n_q=rows // ATT_BLOCK, band=band),
        grid=(n_batch,),
        in_specs=in_specs,
        out_specs=pl.BlockSpec((rows, ATT_WIDTH), lambda b: (b, 0)),
        out_shape=jax.ShapeDtypeStruct((n_batch * rows, ATT_WIDTH), F32),
        compiler_params=_cparams(1),
        name="attn_latent" if latent else "attn_ctx",
    )(*args)


def _merge_kernel(att_ref, sy_ref, z_ref, hy_ref, x_ref, mod_ref, ga_ref, gs_ref, gh_ref, bd_ref, w_ref, o_ref):
    att = att_ref[...]
    a = att * lax.rsqrt(jnp.mean(att * att, axis=-1, keepdims=True) + EPS) * ga_ref[...]
    s = sy_ref[...] * _silu(z_ref[...])
    s = s * lax.rsqrt(jnp.mean(s * s, axis=-1, keepdims=True) + EPS) * gs_ref[...]
    hy = hy_ref[...]
    hn = hy * lax.rsqrt(_segsum(hy * hy, bd_ref[...]) * (1.0 / (HYENA_WIDTH // HYENA_GROUPS)) + EPS) * gh_ref[...]
    y = jnp.dot(a.astype(BF16), w_ref[0:ATT_WIDTH, :], preferred_element_type=F32)
    y = y + jnp.dot(s.astype(BF16), w_ref[ATT_WIDTH:ATT_WIDTH + SSD_WIDTH, :], preferred_element_type=F32)
    y = y + jnp.dot(hn.astype(BF16), w_ref[ATT_WIDTH + SSD_WIDTH:, :], preferred_element_type=F32)
    o_ref[...] = x_ref[...] + mod_ref[0, 2:3, :] * y


def _merge(att, sy, zx, hy, xa, mods, ga, gs, gh, bd, w_out, n_batch, seq_len):
    t = att.shape[0]
    d = xa.shape[1]
    tm = TOKEN_TILE
    row = lambda w: pl.BlockSpec((tm, w), lambda i: (i, 0))
    const = lambda a: pl.BlockSpec(a.shape, lambda i: (0,) * a.ndim)
    return pl.pallas_call(
        _merge_kernel,
        grid=(t // tm,),
        in_specs=[row(ATT_WIDTH), row(SSD_WIDTH), row(SSD_WIDTH), row(HYENA_WIDTH), row(d),
                  pl.BlockSpec((1, 6, d), _mod_index(seq_len // tm, n_batch)),
                  const(ga), const(gs), const(gh), const(bd), const(w_out)],
        out_specs=row(d),
        out_shape=jax.ShapeDtypeStruct((t, d), F32),
        compiler_params=_cparams(1),
        name="merge_outproj",
    )(att, sy, zx, hy, xa, mods, ga, gs, gh, bd, w_out)


def _swiglu_accumulate(h, wg_ref, wu_ref, wd_ref, acc_ref):
    acc_ref[...] = jnp.zeros_like(acc_ref)

    def chunk(c, carry):
        g = jnp.dot(h, wg_ref[c], preferred_element_type=F32)
        u = jnp.dot(h, wu_ref[c], preferred_element_type=F32)
        a = (_silu(g) * u).astype(BF16)
        acc_ref[...] += jnp.dot(a, wd_ref[c], preferred_element_type=F32)
        return carry

    lax.fori_loop(0, N_FFN_CHUNKS, chunk, 0)


def _ffn_kernel(x_ref, mod_ref, g2_ref, wg_ref, wu_ref, wd_ref, o_ref, acc_ref):
    x = x_ref[...]
    h = _modnorm(x, g2_ref[...], mod_ref[0, 4:5, :], mod_ref[0, 3:4, :]).astype(BF16)
    _swiglu_accumulate(h, wg_ref, wu_ref, wd_ref, acc_ref)
    o_ref[...] = x + mod_ref[0, 5:6, :] * acc_ref[...]


def _ffn(xa, mods, g2, wg, wu, wd, n_batch, seq_len):
    t, d = xa.shape
    tm = TOKEN_TILE
    row = pl.BlockSpec((tm, d), lambda i: (i, 0))
    resident = lambda a: pl.BlockSpec(a.shape, lambda i: (0,) * a.ndim, pipeline_mode=pl.Buffered(1))
    return pl.pallas_call(
        _ffn_kernel,
        grid=(t // tm,),
        in_specs=[row, pl.BlockSpec((1, 6, d), _mod_index(seq_len // tm, n_batch)),
                  pl.BlockSpec(g2.shape, lambda i: (0, 0)), resident(wg), resident(wu), resident(wd)],
        out_specs=row,
        out_shape=jax.ShapeDtypeStruct((t, d), F32),
        scratch_shapes=[pltpu.VMEM((tm, d), F32)],
        compiler_params=_cparams(1),
        name="ffn",
    )(xa, mods, g2, wg, wu, wd)


def _router_kernel(x_ref, mod_ref, g2_ref, r_ref, h_ref, idx_ref, wt_ref):
    h = _modnorm(x_ref[...], g2_ref[...], mod_ref[0, 4:5, :], mod_ref[0, 3:4, :])
    h_ref[...] = h
    logits = jnp.dot(h, r_ref[...], precision=lax.Precision.HIGHEST, preferred_element_type=F32)
    lane = lax.broadcasted_iota(jnp.int32, logits.shape, 1)
    neg = -jnp.inf
    l1 = jnp.where(lane < N_EXPERTS, logits, neg)
    m1 = jnp.max(l1, axis=-1, keepdims=True)
    i1 = jnp.min(jnp.where(l1 == m1, lane, LANES), axis=-1, keepdims=True)
    l2 = jnp.where(lane == i1, neg, l1)
    m2 = jnp.max(l2, axis=-1, keepdims=True)
    i2 = jnp.min(jnp.where(l2 == m2, lane, LANES), axis=-1, keepdims=True)
    e = jnp.exp(m2 - m1)
    w1 = 1.0 / (1.0 + e)
    w2 = e / (1.0 + e)
    idx_ref[...] = jnp.where(lane == 0, i1, jnp.where(lane == 1, i2, 0))
    wt_ref[...] = jnp.where(lane == 0, w1, jnp.where(lane == 1, w2, 0.0))


def _router(xa, mods, g2, r_pad, n_rows, n_batch, seq_len):
    d = xa.shape[1]
    tm = TOKEN_TILE
    row = lambda w: pl.BlockSpec((tm, w), lambda i: (i, 0))
    return pl.pallas_call(
        _router_kernel,
        grid=(n_rows // tm,),
        in_specs=[row(d), pl.BlockSpec((1, 6, d), _mod_index(seq_len // tm, n_batch)),
                  pl.BlockSpec(g2.shape, lambda i: (0, 0)), pl.BlockSpec(r_pad.shape, lambda i: (0, 0))],
        out_specs=[row(d), row(LANES), row(LANES)],
        out_shape=[jax.ShapeDtypeStruct((n_rows, d), F32),
                   jax.ShapeDtypeStruct((n_rows, LANES), jnp.int32),
                   jax.ShapeDtypeStruct((n_rows, LANES), F32)],
        compiler_params=_cparams(1),
        name="moe_router",
    )(xa, mods, g2, r_pad)


def _row_copy(src, src_row, dst, dst_row, sem):
    return pltpu.make_async_copy(src.at[pl.ds(src_row, 1), :], dst.at[pl.ds(dst_row, 1), :], sem)


def _dispatch_kernel(dest_hbm, h_ref, xs_in, xs_out, dest_smem, sem_idx, sem_rows):
    del xs_in
    i = pl.program_id(0)
    tm = h_ref.shape[0]
    idx_copy = pltpu.make_async_copy(dest_hbm.at[i], dest_smem, sem_idx)
    idx_copy.start()
    idx_copy.wait()

    def issue(r, carry):
        _row_copy(h_ref, r, xs_out, dest_smem[2 * r], sem_rows).start()
        _row_copy(h_ref, r, xs_out, dest_smem[2 * r + 1], sem_rows).start()
        return carry

    lax.fori_loop(0, tm, issue, 0)

    def drain(r, carry):
        _row_copy(h_ref, 0, xs_out, 0, sem_rows).wait()
        return carry

    lax.fori_loop(0, 2 * tm, drain, 0)


def _dispatch(dest, h, xs0):
    n_rows, d = h.shape
    tm = TOKEN_TILE
    return pl.pallas_call(
        _dispatch_kernel,
        grid=(n_rows // tm,),
        in_specs=[pl.BlockSpec(memory_space=pl.ANY), pl.BlockSpec((tm, d), lambda i: (i, 0)),
                  pl.BlockSpec(memory_space=pl.ANY)],
        out_specs=pl.BlockSpec(memory_space=pl.ANY),
        out_shape=jax.ShapeDtypeStruct(xs0.shape, xs0.dtype),
        scratch_shapes=[pltpu.SMEM((2 * tm,), jnp.int32), pltpu.SemaphoreType.DMA(()),
                        pltpu.SemaphoreType.DMA(())],
        input_output_aliases={2: 0},
        compiler_params=pltpu.CompilerParams(dimension_semantics=("arbitrary",), vmem_limit_bytes=VMEM_LIMIT,
                                             has_side_effects=True),
        name="moe_dispatch",
    )(dest.reshape(n_rows // tm, 2 * tm), h, xs0)


def _expert_kernel(te_ref, nused_ref, xs_ref, wg_ref, wu_ref, wd_ref, o_ref, acc_ref):
    del te_ref
    live = pl.program_id(0) < nused_ref[0]

    @pl.when(live)
    def _():
        _swiglu_accumulate(xs_ref[...].astype(BF16), wg_ref, wu_ref, wd_ref, acc_ref)
        o_ref[...] = acc_ref[...]

    @pl.when(jnp.logical_not(live))
    def _():
        o_ref[...] = jnp.zeros_like(o_ref)


def _experts(tile_expert, n_used, xs, wg, wu, wd):
    s, d = xs.shape
    tm = EXPERT_TILE
    row = pl.BlockSpec((tm, d), lambda i, te, nu: (i, 0))
    wspec = lambda a: pl.BlockSpec((None,) + a.shape[1:], lambda i, te, nu: (te[i], 0, 0, 0))
    return pl.pallas_call(
        _expert_kernel,
        grid_spec=pltpu.PrefetchScalarGridSpec(
            num_scalar_prefetch=2,
            grid=(s // tm,),
            in_specs=[row, wspec(wg), wspec(wu), wspec(wd)],
            out_specs=row,
            scratch_shapes=[pltpu.VMEM((tm, d), F32)]),
        out_shape=jax.ShapeDtypeStruct((s, d), F32),
        compiler_params=_cparams(1),
        name="moe_experts",
    )(tile_expert, n_used, xs, wg, wu, wd)


def _combine_kernel(dest_hbm, eo_hbm, x_ref, wt_ref, mod_ref, o_ref, dest_smem, buf, sem_idx, sem_rows):
    i = pl.program_id(0)
    tm = x_ref.shape[0]
    idx_copy = pltpu.make_async_copy(dest_hbm.at[i], dest_smem, sem_idx)
    idx_copy.start()
    idx_copy.wait()

    def issue(r, carry):
        _row_copy(eo_hbm, dest_smem[2 * r], buf.at[0], r, sem_rows).start()
        _row_copy(eo_hbm, dest_smem[2 * r + 1], buf.at[1], r, sem_rows).start()
        return carry

    lax.fori_loop(0, tm, issue, 0)

    def drain(r, carry):
        _row_copy(eo_hbm, 0, buf.at[0], 0, sem_rows).wait()
        return carry

    lax.fori_loop(0, 2 * tm, drain, 0)
    wt = wt_ref[...]
    y = wt[:, 0:1] * buf[0] + wt[:, 1:2] * buf[1]
    o_ref[...] = x_ref[...] + mod_ref[0, 5:6, :] * y


def _combine(dest, eo, xa, wts, mods, n_rows, n_batch, seq_len):
    d = xa.shape[1]
    tm = TOKEN_TILE
    row = lambda w: pl.BlockSpec((tm, w), lambda i: (i, 0))
    return pl.pallas_call(
        _combine_kernel,
        grid=(n_rows // tm,),
        in_specs=[pl.BlockSpec(memory_space=pl.ANY), pl.BlockSpec(memory_space=pl.ANY), row(d), row(LANES),
                  pl.BlockSpec((1, 6, d), _mod_index(seq_len // tm, n_batch))],
        out_specs=row(d),
        out_shape=jax.ShapeDtypeStruct((n_rows, d), F32),
        scratch_shapes=[pltpu.SMEM((2 * tm,), jnp.int32), pltpu.VMEM((2, tm, d), F32),
                        pltpu.SemaphoreType.DMA(()), pltpu.SemaphoreType.DMA(())],
        compiler_params=_cparams(1),
        name="moe_combine",
    )(dest.reshape(n_rows // tm, 2 * tm), eo, xa, wts, mods)


def _moe(xa, mods, g2, r_pad, wg, wu, wd, n_rows, n_batch, seq_len):
    h, idx, wts = _router(xa, mods, g2, r_pad, n_rows, n_batch, seq_len)
    tm = EXPERT_TILE
    e_flat = idx[:, :2].reshape(-1)
    onehot = (e_flat[:, None] == jnp.arange(N_EXPERTS, dtype=jnp.int32)[None, :]).astype(jnp.int32)
    csum = jnp.cumsum(onehot, axis=0)
    counts = csum[-1]
    rank = jnp.sum(onehot * csum, axis=1) - 1
    padded = ((counts + tm - 1) // tm) * tm
    ends = jnp.cumsum(padded)
    starts = ends - padded
    dest = (jnp.sum(onehot * starts[None, :], axis=1) + rank).astype(jnp.int32)
    n_slots = 2 * n_rows + N_EXPERTS * tm
    tile_start = jnp.arange(n_slots // tm, dtype=jnp.int32) * tm
    tile_expert = jnp.minimum(jnp.sum((tile_start[:, None] >= ends[None, :]).astype(jnp.int32), axis=1),
                              N_EXPERTS - 1).astype(jnp.int32)
    n_used = (ends[-1:] // tm).astype(jnp.int32)
    xs = _dispatch(dest, h, jnp.zeros((n_slots, xa.shape[1]), F32))
    eo = _experts(tile_expert, n_used, xs, wg, wu, wd)
    return _combine(dest, eo, xa, wts, mods, n_rows, n_batch, seq_len)


def _rms(x):
    return x * lax.rsqrt(jnp.mean(x * x, axis=-1, keepdims=True) + EPS)


def _short_conv(u, w, b):
    k = w.shape[0]
    y = lax.conv_general_dilated(u, w[:, None, :], window_strides=(1,), padding=[(k // 2, k // 2)],
                                 dimension_numbers=('NWC', 'WIO', 'NWC'), feature_group_count=u.shape[-1])
    return y + b


def _ssd_prep(xbc, dt_raw, lp):
    xbc = jax.nn.silu(_short_conv(xbc, lp['ssd_conv_w'], lp['ssd_conv_b']))
    xs, bm, cm = jnp.split(xbc, [SSD_WIDTH, SSD_WIDTH + SSD_GROUPS * SSD_STATE], axis=-1)
    b, seq_len = xs.shape[:2]
    rep = SSD_HEADS // SSD_GROUPS
    bh = jnp.repeat(bm.reshape(b, seq_len, SSD_GROUPS, SSD_STATE), rep, axis=2)
    ch = jnp.repeat(cm.reshape(b, seq_len, SSD_GROUPS, SSD_STATE), rep, axis=2)
    dt = jax.nn.softplus(dt_raw.reshape(b, seq_len, 2, SSD_HEADS) + lp['ssd_dt_bias'])
    return xs.reshape(b, seq_len, SSD_HEADS, SSD_HEAD_DIM), bh, ch, dt


def _ssd_chunked(xh, dt, a, bh, ch, h0, want_y):
    b, seq_len, nh, hp = xh.shape
    nc = seq_len // SSD_CHUNK
    shp = (b, nc, SSD_CHUNK)
    xdt = (xh * dt[..., None]).reshape(shp + (nh, hp))
    bc = bh.reshape(shp + (nh, SSD_STATE))
    la = jnp.cumsum((dt * a).reshape(shp + (nh,)), axis=2)
    w_end = jnp.exp(la[:, :, -1:] - la)
    states = jnp.einsum('bcshn,bcshp->bchpn', bc, xdt * w_end[..., None])

    def step(h, inp):
        s, dec = inp
        return h * dec[:, :, None, None] + s, h

    h_last, h_start = lax.scan(step, h0, (jnp.moveaxis(states, 1, 0), jnp.moveaxis(jnp.exp(la[:, :, -1]), 1, 0)))
    if not want_y:
        return None, h_last
    cc = ch.reshape(shp + (nh, SSD_STATE))
    seg = la[:, :, :, None] - la[:, :, None, :]
    lower = jnp.tril(jnp.ones((SSD_CHUNK, SSD_CHUNK), bool))[None, None, :, :, None]
    decay = jnp.exp(jnp.where(lower, seg, -jnp.inf))
    scores = jnp.einsum('bclhn,bcshn->bclsh', cc, bc)
    y_diag = jnp.einsum('bclsh,bcshp->bclhp', scores * decay, xdt)
    y_off = jnp.einsum('bclhn,bchpn->bclhp', cc, jnp.moveaxis(h_start, 0, 1)) * jnp.exp(la)[..., None]
    return (y_diag + y_off).reshape(b, seq_len, nh, hp), h_last


def _flip(t, rev):
    return jnp.flip(t, axis=1) if rev else t


def _ssd_bidirectional(lat, ctx, a_log, d_skip, want_ctx):
    a = -jnp.exp(a_log)
    xl, bl, cl, dl = lat
    xc, bc, cc, dc = ctx
    h0 = jnp.zeros((xl.shape[0], SSD_HEADS, SSD_HEAD_DIM, SSD_STATE), F32)
    dsk = d_skip[:, None]
    y_lat = xl * dsk
    y_ctx = xc * dsk if want_ctx else None
    for direction in range(2):
        rev = direction == 1
        yc, hc = _ssd_chunked(_flip(xc, rev), _flip(dc[:, :, direction], rev), a[direction],
                              _flip(bc, rev), _flip(cc, rev), h0, want_ctx)
        yl, _ = _ssd_chunked(_flip(xl, rev), _flip(dl[:, :, direction], rev), a[direction],
                             _flip(bl, rev), _flip(cl, rev), hc, True)
        y_lat = y_lat + _flip(yl, rev)
        if want_ctx:
            y_ctx = y_ctx + _flip(yc, rev)
    return y_lat, y_ctx


def _hyena_filters(seq_len, lp):
    t = jnp.linspace(0.0, 1.0, seq_len, dtype=F32)[:, None]
    w = (2.0 * math.pi / seq_len) * jnp.arange(seq_len, dtype=F32)[:, None]
    bands = (HYENA_POS_DIM - 1) // 2
    freqs = jnp.linspace(1e-4, bands - 1, bands, dtype=F32)[None, :]
    z = jnp.concatenate([t, jnp.cos(freqs * w), -jnp.sin(freqs * w)], axis=-1)
    h = jnp.sin(lp['hy_f1'] * (z @ lp['hy_w1'] + lp['hy_b1']))
    h = jnp.sin(lp['hy_f2'] * (h @ lp['hy_w2'] + lp['hy_b2']))
    h = (h @ lp['hy_w3']).reshape(seq_len, 2, HYENA_ORDER, HYENA_WIDTH)
    deltas = jnp.abs(jnp.linspace(math.log(HYENA_DECAY_TARGET) / HYENA_SLOW_DECAY,
                                  math.log(HYENA_DECAY_TARGET) / HYENA_FAST_DECAY, HYENA_WIDTH, dtype=F32))
    h = h * jnp.exp(-t * deltas)[:, None, None, :]
    k2 = jnp.concatenate([h[:, 0], jnp.zeros((1, HYENA_ORDER, HYENA_WIDTH), F32),
                          jnp.flip(h[1:, 1], axis=0)], axis=0)
    return jnp.fft.rfft(k2, axis=0)


def _long_conv(u, kf, bias):
    seq_len = u.shape[1]
    y = jnp.fft.irfft(jnp.fft.rfft(u, n=2 * seq_len, axis=1) * kf[None], n=2 * seq_len, axis=1)[:, :seq_len]
    return y + u * bias


def _hyena_operator(u, lp):
    kf = _hyena_filters(u.shape[1], lp)
    u = _short_conv(u, lp['hy_conv_w'], lp['hy_conv_b'])
    v, x1, x2 = jnp.split(u, 3, axis=-1)
    z = x1 * _long_conv(v, kf[:, 0], lp['hy_bias'][0])
    return x2 * _long_conv(z, kf[:, 1], lp['hy_bias'][1])


def _rope_tables(seq_len, extra):
    rows = seq_len // GRID_W
    row = jnp.repeat(jnp.arange(rows, dtype=F32), GRID_W)
    col = jnp.tile(jnp.arange(GRID_W, dtype=F32), rows)
    inv = ROPE_THETA ** (-jnp.arange(0, ROPE_AXIS_DIM, 2, dtype=F32) / ROPE_AXIS_DIM)
    ang = jnp.stack([row[:, None] * inv, col[:, None] * inv], axis=1)
    cos = jnp.cos(ang)
    sin = jnp.sin(ang)
    cos_h = jnp.concatenate([cos, cos], axis=-1).reshape(seq_len, HEAD_DIM)
    sin_h = jnp.concatenate([-sin, sin], axis=-1).reshape(seq_len, HEAD_DIM)
    cos_t = jnp.concatenate([jnp.tile(cos_h, (1, LANES // HEAD_DIM)), jnp.ones((extra, LANES), F32)], axis=0)
    sin_t = jnp.concatenate([jnp.tile(sin_h, (1, LANES // HEAD_DIM)), jnp.zeros((extra, LANES), F32)], axis=0)
    return cos_t, sin_t


def _block_diag_ones(n, seg):
    i = jnp.arange(n) // seg
    return (i[:, None] == i[None, :]).astype(BF16)


def _chunked_in(w):
    *lead, d, f = w.shape
    w = w.reshape(*lead, d, f // FFN_CHUNK, FFN_CHUNK)
    return jnp.moveaxis(w, -2, -3).astype(BF16)


def _chunked_out(w):
    *lead, f, d = w.shape
    return w.reshape(*lead, f // FFN_CHUNK, FFN_CHUNK, d).astype(BF16)


def kernel(x, c, ctx, c_ctx, w_ada, b_ada, norm1, norm2, w_in, w_out, q_norm, k_norm, att_sinks, att_out_norm, ssd_conv_w, ssd_conv_b, ssd_dt_bias, ssd_a_log, ssd_d, ssd_norm, hy_conv_w, hy_conv_b, hy_w1, hy_b1, hy_f1, hy_w2, hy_b2, hy_f2, hy_w3, hy_bias, hy_out_norm, ffn_w_gate, ffn_w_up, ffn_w_down, moe_router, moe_w_gate, moe_w_up, moe_w_down):
    n_batch, seq_len, d = x.shape
    ctx_len = ctx.shape[1]
    n_lat = n_batch * seq_len
    n_ctx = n_batch * ctx_len
    depth = w_in.shape[0]
    xa = jnp.concatenate([x.reshape(n_lat, d), ctx.reshape(n_ctx, d)], axis=0)

    cc = jnp.concatenate([c, c_ctx[None, :]], axis=0)
    pad_rows = (-cc.shape[0]) % 8
    cc = jnp.pad(cc, ((0, pad_rows), (0, 0)))
    mods_all = _adaln(cc, w_ada, b_ada)[:, :n_batch + 1].reshape(depth, n_batch + 1, 6, d)

    cos_t, sin_t = _rope_tables(seq_len, TOKEN_TILE)
    bd_q = _block_diag_ones(Q_COLS, HEAD_DIM)
    bd_h = _block_diag_ones(HYENA_WIDTH, HYENA_WIDTH // HYENA_GROUPS)

    for i in range(depth):
        last = i == depth - 1
        j = i // 2
        mods = mods_all[i]
        lp = {
            'ssd_conv_w': ssd_conv_w[i], 'ssd_conv_b': ssd_conv_b[i], 'ssd_dt_bias': ssd_dt_bias[i],
            'hy_conv_w': hy_conv_w[i], 'hy_conv_b': hy_conv_b[i], 'hy_w1': hy_w1[i], 'hy_b1': hy_b1[i],
            'hy_f1': hy_f1[i], 'hy_w2': hy_w2[i], 'hy_b2': hy_b2[i], 'hy_f2': hy_f2[i], 'hy_w3': hy_w3[i],
            'hy_bias': hy_bias[i],
        }
        wi = w_in[i]
        c_dt = QKV_W + ZX_W
        w_cat = jnp.concatenate([wi[:, :c_dt], wi[:, c_dt + SSD_DT_COLS:], wi[:, c_dt:c_dt + SSD_DT_COLS],
                                 jnp.zeros((d, LANES - SSD_DT_COLS), F32)], axis=1).astype(BF16)
        qg = jnp.tile(q_norm[i], Q_COLS // HEAD_DIM)[None, :]
        kg = jnp.tile(k_norm[i], KV_COLS // HEAD_DIM)[None, :]
        q, k, v, zx, hy, dtp = _inproj(xa, mods, norm1[i][None, :], w_cat, cos_t, sin_t, qg, kg, bd_q,
                                       n_batch, seq_len)

        att_l = _attention(att_sinks[i], q, k, v, n_batch, seq_len, ctx_len, True)

        def split(a, w):
            return a[:n_lat].reshape(n_batch, seq_len, w), a[n_lat:].reshape(n_batch, ctx_len, w)

        zx_l, zx_c = split(zx, ZX_W)
        dt_l, dt_c = split(dtp, LANES)
        hy_l, hy_c = split(hy, HY_COLS)
        ssd_l, ssd_c = _ssd_bidirectional(
            _ssd_prep(zx_l[..., SSD_WIDTH:], dt_l[..., :SSD_DT_COLS], lp),
            _ssd_prep(zx_c[..., SSD_WIDTH:], dt_c[..., :SSD_DT_COLS], lp),
            ssd_a_log[i], ssd_d[i], not last)
        hyo_l = _hyena_operator(hy_l, lp)
        if last:
            att = att_l
            sy = ssd_l.reshape(n_lat, SSD_WIDTH)
            hyo = hyo_l.reshape(n_lat, HYENA_WIDTH)
            n_rows = n_lat
        else:
            att_c = _attention(att_sinks[i], q, k, v, n_batch, seq_len, ctx_len, False)
            att = jnp.concatenate([att_l, att_c], axis=0)
            sy = jnp.concatenate([ssd_l.reshape(n_lat, SSD_WIDTH), ssd_c.reshape(n_ctx, SSD_WIDTH)], axis=0)
            hyo = jnp.concatenate([hyo_l.reshape(n_lat, HYENA_WIDTH),
                                   _hyena_operator(hy_c, lp).reshape(n_ctx, HYENA_WIDTH)], axis=0)
            n_rows = n_lat + n_ctx
        xa = _merge(att, sy, zx, hyo, xa, mods, att_out_norm[i][None, :], ssd_norm[i][None, :],
                    hy_out_norm[i][None, :], bd_h, w_out[i].astype(BF16), n_batch, seq_len)

        g2 = norm2[i][None, :]
        if i % 2 == 0:
            xa = _ffn(xa, mods, g2, _chunked_in(ffn_w_gate[j]), _chunked_in(ffn_w_up[j]),
                      _chunked_out(ffn_w_down[j]), n_batch, seq_len)
        else:
            r_pad = jnp.pad(moe_router[j], ((0, 0), (0, LANES - N_EXPERTS)))
            xa = _moe(xa, mods, g2, r_pad, _chunked_in(moe_w_gate[j]), _chunked_in(moe_w_up[j]),
                      _chunked_out(moe_w_down[j]), n_rows, n_batch, seq_len)
    return xa[:n_lat].reshape(n_batch, seq_len, d)
```

```python
import functools
import math

import jax
import jax.numpy as jnp
from jax import lax
from jax.experimental import pallas as pl
from jax.experimental.pallas import tpu as pltpu

F32 = jnp.float32
BF16 = jnp.bfloat16

D_MODEL = 1024
DEPTH = 4
GRID_W = 64
EPS = 1e-6
HEAD_DIM = 64
ATT_WIDTH = 512
ATT_HEADS = 8
ATT_KV_HEADS = 2
ATT_GROUP = 4
WINDOW = 128
ATT_BLOCK = 128
ROPE_THETA = 10000.0
ROPE_AXIS_DIM = 32
SSD_WIDTH = 256
SSD_HEAD_DIM = 64
SSD_HEADS = 4
SSD_STATE = 64
SSD_GROUPS = 2
SSD_CHUNK = 128
HYENA_WIDTH = 256
HYENA_GROUPS = 4
HYENA_ORDER = 2
HYENA_POS_DIM = 33
HYENA_FAST_DECAY = 0.3
HYENA_SLOW_DECAY = 1.5
HYENA_DECAY_TARGET = 1e-2
Q_COLS = 512
KV_COLS = 128
SSD_XBC_COLS = 512
SSD_DT_COLS = 8
HY_COLS = 768
FFN_DIM = 2816
N_EXPERTS = 8
FFN_CHUNK = 256
N_FFN_CHUNKS = FFN_DIM // FFN_CHUNK
LANES = 128
QKV_W = Q_COLS + 2 * KV_COLS
ZX_W = SSD_WIDTH + SSD_XBC_COLS
PROJ_PAD = QKV_W + ZX_W + HY_COLS + LANES
VMEM_LIMIT = 56 * 1024 * 1024
TOKEN_TILE = 512
EXPERT_TILE = 512


def _cparams(n_axes):
    return pltpu.CompilerParams(dimension_semantics=("arbitrary",) * n_axes,
                                vmem_limit_bytes=VMEM_LIMIT)


def _silu(v):
    return v / (1.0 + jnp.exp(-v))


def _modnorm(x, g, scale, shift):
    ms = jnp.mean(x * x, axis=-1, keepdims=True)
    return x * lax.rsqrt(ms + EPS) * g * (1.0 + scale) + shift


def _segsum(t, bd):
    hi = t.astype(BF16)
    lo = (t - hi.astype(F32)).astype(BF16)
    return (jnp.dot(hi, bd, preferred_element_type=F32)
            + jnp.dot(lo, bd, preferred_element_type=F32))


def _mod_index(tiles_per_batch, n_batch):
    return lambda i: (jnp.minimum(i // tiles_per_batch, n_batch), 0, 0)


def _adaln_kernel(c_ref, w_ref, b_ref, o_ref):
    s = _silu(c_ref[...]).astype(BF16)
    o_ref[...] = jnp.dot(s, w_ref[...].astype(BF16), preferred_element_type=F32) + b_ref[...]


def _adaln(cc, w_ada, b_ada):
    depth, d, n = w_ada.shape
    r = cc.shape[0]
    tn = 512
    return pl.pallas_call(
        _adaln_kernel,
        grid=(depth, n // tn),
        in_specs=[pl.BlockSpec((r, d), lambda l, j: (0, 0)),
                  pl.BlockSpec((None, d, tn), lambda l, j: (l, 0, j)),
                  pl.BlockSpec((None, 1, tn), lambda l, j: (l, 0, j))],
        out_specs=pl.BlockSpec((None, r, tn), lambda l, j: (l, 0, j)),
        out_shape=jax.ShapeDtypeStruct((depth, r, n), F32),
        compiler_params=_cparams(2),
        name="adaln",
    )(cc, w_ada, b_ada.reshape(depth, 1, n))


def _inproj_kernel(x_ref, mod_ref, g1_ref, w_ref, cos_ref, sin_ref, qg_ref, kg_ref, bd_ref,
                   q_ref, k_ref, v_ref, z_ref, xbc_ref, hyt_ref, dt_ref):
    x = x_ref[...]
    h = _modnorm(x, g1_ref[...], mod_ref[0, 1:2, :], mod_ref[0, 0:1, :]).astype(BF16)
    pq = jnp.dot(h, w_ref[:, 0:QKV_W], preferred_element_type=F32)
    cos = cos_ref[...]
    sin = sin_ref[...]
    lane = lax.broadcasted_iota(jnp.int32, (1, LANES), 1)
    first_half = (lane % 32) < 16

    def rope(t):
        partner = jnp.where(first_half, pltpu.roll(t, LANES - 16, 1), pltpu.roll(t, 16, 1))
        return t * cos + partner * sin

    q = pq[:, 0:Q_COLS]
    qn = q * lax.rsqrt(_segsum(q * q, bd_ref[...]) * (1.0 / HEAD_DIM) + EPS) * qg_ref[...]
    scale = HEAD_DIM ** -0.5
    for j in range(Q_COLS // LANES):
        q_ref[:, LANES * j:LANES * (j + 1)] = (rope(qn[:, LANES * j:LANES * (j + 1)]) * scale).astype(BF16)
    k = pq[:, Q_COLS:Q_COLS + KV_COLS]
    kn = k * lax.rsqrt(_segsum(k * k, bd_ref[0:KV_COLS, 0:KV_COLS]) * (1.0 / HEAD_DIM) + EPS) * kg_ref[...]
    k_ref[...] = rope(kn).astype(BF16)
    v_ref[...] = pq[:, Q_COLS + KV_COLS:QKV_W].astype(BF16)
    zx = jnp.dot(h, w_ref[:, QKV_W:QKV_W + ZX_W], preferred_element_type=F32)
    z_ref[...] = zx[:, 0:SSD_WIDTH]
    xbc_ref[...] = zx[:, SSD_WIDTH:ZX_W]
    hyt_ref[...] = jnp.dot(h, w_ref[:, QKV_W + ZX_W:QKV_W + ZX_W + HY_COLS], preferred_element_type=F32).T
    dt_ref[...] = jnp.dot(h, w_ref[:, QKV_W + ZX_W + HY_COLS:PROJ_PAD], preferred_element_type=F32)


def _inproj(xa, mods, g1, w_cat, cos_t, sin_t, qg, kg, bd, n_batch, seq_len):
    t, d = xa.shape
    tm = TOKEN_TILE
    tpb = seq_len // tm
    n_lat = n_batch * tpb
    rope_idx = lambda i: (jnp.where(i < n_lat, i % tpb, tpb), 0)
    row = lambda w: pl.BlockSpec((tm, w), lambda i: (i, 0))
    const = lambda a: pl.BlockSpec(a.shape, lambda i: (0,) * a.ndim)
    return pl.pallas_call(
        _inproj_kernel,
        grid=(t // tm,),
        in_specs=[row(d),
                  pl.BlockSpec((1, 6, d), _mod_index(tpb, n_batch)),
                  const(g1), const(w_cat),
                  pl.BlockSpec((tm, LANES), rope_idx), pl.BlockSpec((tm, LANES), rope_idx),
                  const(qg), const(kg), const(bd)],
        out_specs=[row(Q_COLS), row(KV_COLS), row(KV_COLS), row(SSD_WIDTH), row(SSD_XBC_COLS),
                   pl.BlockSpec((HY_COLS, tm), lambda i: (0, i)), row(LANES)],
        out_shape=[jax.ShapeDtypeStruct((t, Q_COLS), BF16),
                   jax.ShapeDtypeStruct((t, KV_COLS), BF16),
                   jax.ShapeDtypeStruct((t, KV_COLS), BF16),
                   jax.ShapeDtypeStruct((t, SSD_WIDTH), F32),
                   jax.ShapeDtypeStruct((t, SSD_XBC_COLS), F32),
                   jax.ShapeDtypeStruct((HY_COLS, t), F32),
                   jax.ShapeDtypeStruct((t, LANES), F32)],
        compiler_params=_cparams(1),
        name="inproj",
    )(xa, mods, g1, w_cat, cos_t, sin_t, qg, kg, bd)


def _attn_kernel(sink_ref, q_ref, *refs, n_q, band):
    if band:
        k_ref, v_ref, kc_ref, vc_ref, o_ref = refs
        seq_len = k_ref.shape[0]
    else:
        kc_ref, vc_ref, o_ref = refs
    qb = ATT_BLOCK
    rows = ATT_GROUP * qb
    row_id = lax.broadcasted_iota(jnp.int32, (rows, 1), 0)
    if band:
        rel0 = (lax.broadcasted_iota(jnp.int32, (rows, band), 1)
                - lax.broadcasted_iota(jnp.int32, (rows, band), 0) % qb)
    kc = kc_ref[...]
    vc = vc_ref[...]
    nt = (((1,), (1,)), ((), ()))

    def block(i, carry):
        q0 = pl.multiple_of(i * qb, qb)
        qs = q_ref[pl.ds(q0, qb), :]
        if band:
            k0 = pl.multiple_of(jnp.clip(q0 - WINDOW, 0, seq_len - band), qb)
            kb = k_ref[pl.ds(k0, band), :]
            vb = v_ref[pl.ds(k0, band), :]
            valid = jnp.abs(rel0 + (k0 - q0)) <= WINDOW
        for j in range(ATT_KV_HEADS):
            hs = slice(j * HEAD_DIM, (j + 1) * HEAD_DIM)
            qh = jnp.concatenate(
                [qs[:, (ATT_GROUP * j + g) * HEAD_DIM:(ATT_GROUP * j + g + 1) * HEAD_DIM]
                 for g in range(ATT_GROUP)], axis=0)
            snk = jnp.zeros((rows, 1), F32)
            for g in range(ATT_GROUP):
                snk = jnp.where(row_id // qb == g, sink_ref[ATT_GROUP * j + g], snk)
            s_ctx = lax.dot_general(qh, kc[:, hs], nt, preferred_element_type=F32)
            m = jnp.maximum(jnp.max(s_ctx, axis=-1, keepdims=True), snk)
            if band:
                s_loc = lax.dot_general(qh, kb[:, hs], nt, preferred_element_type=F32)
                s_loc = jnp.where(valid, s_loc, -jnp.inf)
                m = jnp.maximum(m, jnp.max(s_loc, axis=-1, keepdims=True))
            p_ctx = jnp.exp(s_ctx - m)
            den = jnp.sum(p_ctx, axis=-1, keepdims=True) + jnp.exp(snk - m)
            o = jnp.dot(p_ctx.astype(BF16), vc[:, hs], preferred_element_type=F32)
            if band:
                p_loc = jnp.exp(s_loc - m)
                den = den + jnp.sum(p_loc, axis=-1, keepdims=True)
                o = o + jnp.dot(p_loc.astype(BF16), vb[:, hs], preferred_element_type=F32)
            o = o / den
            for g in range(ATT_GROUP):
                c0 = (ATT_GROUP * j + g) * HEAD_DIM
                o_ref[pl.ds(q0, qb), c0:c0 + HEAD_DIM] = o[g * qb:(g + 1) * qb, :]
        return carry

    lax.fori_loop(0, n_q, block, 0)


def _attention(sinks, q, k, v, n_batch, seq_len, ctx_len, latent):
    ctx_blk0 = n_batch * seq_len // ctx_len
    ctx_spec = pl.BlockSpec((ctx_len, KV_COLS), lambda b: (ctx_blk0 + b, 0))
    smem = pl.BlockSpec(memory_space=pltpu.SMEM)
    if latent:
        rows = seq_len
        band = ATT_BLOCK + 2 * WINDOW
        in_specs = [smem, pl.BlockSpec((rows, Q_COLS), lambda b: (b, 0)),
                    pl.BlockSpec((rows, KV_COLS), lambda b: (b, 0)),
                    pl.BlockSpec((rows, KV_COLS), lambda b: (b, 0)), ctx_spec, ctx_spec]
        args = (sinks, q, k, v, k, v)
    else:
        rows = ctx_len
        band = 0
        in_specs = [smem, pl.BlockSpec((rows, Q_COLS), lambda b: (ctx_blk0 + b, 0)), ctx_spec, ctx_spec]
        args = (sinks, q, k, v)
    return pl.pallas_call(
        functools.partial(_attn_kernel, n_q=rows // ATT_BLOCK, band=band),
        grid=(n_batch,),
        in_specs=in_specs,
        out_specs=pl.BlockSpec((rows, ATT_WIDTH), lambda b: (b, 0)),
        out_shape=jax.ShapeDtypeStruct((n_batch * rows, ATT_WIDTH), F32),
        compiler_params=_cparams(1),
        name="attn_latent" if latent else "attn_ctx",
    )(*args)


def _merge_kernel(att_ref, sy_ref, z_ref, hy_ref, x_ref, mod_ref, ga_ref, gs_ref, gh_ref, bd_ref, w_ref, o_ref):
    att = att_ref[...]
    a = att * lax.rsqrt(jnp.mean(att * att, axis=-1, keepdims=True) + EPS) * ga_ref[...]
    s = sy_ref[...] * _silu(z_ref[...])
    s = s * lax.rsqrt(jnp.mean(s * s, axis=-1, keepdims=True) + EPS) * gs_ref[...]
    hy = hy_ref[...].T
    hn = hy * lax.rsqrt(_segsum(hy * hy, bd_ref[...]) * (1.0 / (HYENA_WIDTH // HYENA_GROUPS)) + EPS) * gh_ref[...]
    y = jnp.dot(a.astype(BF16), w_ref[0:ATT_WIDTH, :], preferred_element_type=F32)
    y = y + jnp.dot(s.astype(BF16), w_ref[ATT_WIDTH:ATT_WIDTH + SSD_WIDTH, :], preferred_element_type=F32)
    y = y + jnp.dot(hn.astype(BF16), w_ref[ATT_WIDTH + SSD_WIDTH:, :], preferred_element_type=F32)
    o_ref[...] = x_ref[...] + mod_ref[0, 2:3, :] * y


def _merge(att, sy, zx, hy, xa, mods, ga, gs, gh, bd, w_out, n_batch, seq_len):
    t = att.shape[0]
    d = xa.shape[1]
    tm = TOKEN_TILE
    row = lambda w: pl.BlockSpec((tm, w), lambda i: (i, 0))
    const = lambda a: pl.BlockSpec(a.shape, lambda i: (0,) * a.ndim)
    return pl.pallas_call(
        _merge_kernel,
        grid=(t // tm,),
        in_specs=[row(ATT_WIDTH), row(SSD_WIDTH), row(SSD_WIDTH),
                  pl.BlockSpec((HYENA_WIDTH, tm), lambda i: (0, i)), row(d),
                  pl.BlockSpec((1, 6, d), _mod_index(seq_len // tm, n_batch)),
                  const(ga), const(gs), const(gh), const(bd), const(w_out)],
        out_specs=row(d),
        out_shape=jax.ShapeDtypeStruct((t, d), F32),
        compiler_params=_cparams(1),
        name="merge_outproj",
    )(att, sy, zx, hy, xa, mods, ga, gs, gh, bd, w_out)


def _swiglu_accumulate(h, wg_ref, wu_ref, wd_ref, acc_ref):
    acc_ref[...] = jnp.zeros_like(acc_ref)

    def chunk(c, carry):
        g = jnp.dot(h, wg_ref[c], preferred_element_type=F32)
        u = jnp.dot(h, wu_ref[c], preferred_element_type=F32)
        a = (_silu(g) * u).astype(BF16)
        acc_ref[...] += jnp.dot(a, wd_ref[c], preferred_element_type=F32)
        return carry

    lax.fori_loop(0, N_FFN_CHUNKS, chunk, 0)


def _ffn_kernel(x_ref, mod_ref, g2_ref, wg_ref, wu_ref, wd_ref, o_ref, acc_ref):
    x = x_ref[...]
    h = _modnorm(x, g2_ref[...], mod_ref[0, 4:5, :], mod_ref[0, 3:4, :]).astype(BF16)
    _swiglu_accumulate(h, wg_ref, wu_ref, wd_ref, acc_ref)
    o_ref[...] = x + mod_ref[0, 5:6, :] * acc_ref[...]


def _ffn(xa, mods, g2, wg, wu, wd, n_batch, seq_len):
    t, d = xa.shape
    tm = TOKEN_TILE
    row = pl.BlockSpec((tm, d), lambda i: (i, 0))
    resident = lambda a: pl.BlockSpec(a.shape, lambda i: (0,) * a.ndim, pipeline_mode=pl.Buffered(1))
    return pl.pallas_call(
        _ffn_kernel,
        grid=(t // tm,),
        in_specs=[row, pl.BlockSpec((1, 6, d), _mod_index(seq_len // tm, n_batch)),
                  pl.BlockSpec(g2.shape, lambda i: (0, 0)), resident(wg), resident(wu), resident(wd)],
        out_specs=row,
        out_shape=jax.ShapeDtypeStruct((t, d), F32),
        scratch_shapes=[pltpu.VMEM((tm, d), F32)],
        compiler_params=_cparams(1),
        name="ffn",
    )(xa, mods, g2, wg, wu, wd)


def _router_kernel(x_ref, mod_ref, g2_ref, r_ref, h_ref, idx_ref, wt_ref):
    h = _modnorm(x_ref[...], g2_ref[...], mod_ref[0, 4:5, :], mod_ref[0, 3:4, :])
    h_ref[...] = h
    logits = jnp.dot(h, r_ref[...], precision=lax.Precision.HIGHEST, preferred_element_type=F32)
    lane = lax.broadcasted_iota(jnp.int32, logits.shape, 1)
    neg = -jnp.inf
    l1 = jnp.where(lane < N_EXPERTS, logits, neg)
    m1 = jnp.max(l1, axis=-1, keepdims=True)
    i1 = jnp.min(jnp.where(l1 == m1, lane, LANES), axis=-1, keepdims=True)
    l2 = jnp.where(lane == i1, neg, l1)
    m2 = jnp.max(l2, axis=-1, keepdims=True)
    i2 = jnp.min(jnp.where(l2 == m2, lane, LANES), axis=-1, keepdims=True)
    e = jnp.exp(m2 - m1)
    w1 = 1.0 / (1.0 + e)
    w2 = e / (1.0 + e)
    idx_ref[...] = jnp.where(lane == 0, i1, jnp.where(lane == 1, i2, 0))
    wt_ref[...] = jnp.where(lane == 0, w1, jnp.where(lane == 1, w2, 0.0))


def _router(xa, mods, g2, r_pad, n_rows, n_batch, seq_len):
    d = xa.shape[1]
    tm = TOKEN_TILE
    row = lambda w: pl.BlockSpec((tm, w), lambda i: (i, 0))
    return pl.pallas_call(
        _router_kernel,
        grid=(n_rows // tm,),
        in_specs=[row(d), pl.BlockSpec((1, 6, d), _mod_index(seq_len // tm, n_batch)),
                  pl.BlockSpec(g2.shape, lambda i: (0, 0)), pl.BlockSpec(r_pad.shape, lambda i: (0, 0))],
        out_specs=[row(d), row(LANES), row(LANES)],
        out_shape=[jax.ShapeDtypeStruct((n_rows, d), F32),
                   jax.ShapeDtypeStruct((n_rows, LANES), jnp.int32),
                   jax.ShapeDtypeStruct((n_rows, LANES), F32)],
        compiler_params=_cparams(1),
        name="moe_router",
    )(xa, mods, g2, r_pad)


def _row_copy(src, src_row, dst, dst_row, sem):
    return pltpu.make_async_copy(src.at[pl.ds(src_row, 1), :], dst.at[pl.ds(dst_row, 1), :], sem)


def _dispatch_kernel(dest_hbm, h_ref, xs_in, xs_out, dest_smem, sem_idx, sem_rows):
    del xs_in
    i = pl.program_id(0)
    tm = h_ref.shape[0]
    idx_copy = pltpu.make_async_copy(dest_hbm.at[i], dest_smem, sem_idx)
    idx_copy.start()
    idx_copy.wait()

    def issue(r, carry):
        _row_copy(h_ref, r, xs_out, dest_smem[2 * r], sem_rows).start()
        _row_copy(h_ref, r, xs_out, dest_smem[2 * r + 1], sem_rows).start()
        return carry

    lax.fori_loop(0, tm, issue, 0)

    def drain(r, carry):
        _row_copy(h_ref, 0, xs_out, 0, sem_rows).wait()
        return carry

    lax.fori_loop(0, 2 * tm, drain, 0)


def _dispatch(dest, h, xs0):
    n_rows, d = h.shape
    tm = TOKEN_TILE
    return pl.pallas_call(
        _dispatch_kernel,
        grid=(n_rows // tm,),
        in_specs=[pl.BlockSpec(memory_space=pl.ANY), pl.BlockSpec((tm, d), lambda i: (i, 0)),
                  pl.BlockSpec(memory_space=pl.ANY)],
        out_specs=pl.BlockSpec(memory_space=pl.ANY),
        out_shape=jax.ShapeDtypeStruct(xs0.shape, xs0.dtype),
        scratch_shapes=[pltpu.SMEM((2 * tm,), jnp.int32), pltpu.SemaphoreType.DMA(()),
                        pltpu.SemaphoreType.DMA(())],
        input_output_aliases={2: 0},
        compiler_params=pltpu.CompilerParams(dimension_semantics=("arbitrary",), vmem_limit_bytes=VMEM_LIMIT,
                                             has_side_effects=True),
        name="moe_dispatch",
    )(dest.reshape(n_rows // tm, 2 * tm), h, xs0)


def _expert_kernel(te_ref, nused_ref, xs_ref, wg_ref, wu_ref, wd_ref, o_ref, acc_ref):
    del te_ref
    live = pl.program_id(0) < nused_ref[0]

    @pl.when(live)
    def _():
        _swiglu_accumulate(xs_ref[...].astype(BF16), wg_ref, wu_ref, wd_ref, acc_ref)
        o_ref[...] = acc_ref[...]

    @pl.when(jnp.logical_not(live))
    def _():
        o_ref[...] = jnp.zeros_like(o_ref)


def _experts(tile_expert, n_used, xs, wg, wu, wd):
    s, d = xs.shape
    tm = EXPERT_TILE
    row = pl.BlockSpec((tm, d), lambda i, te, nu: (i, 0))
    wspec = lambda a: pl.BlockSpec((None,) + a.shape[1:], lambda i, te, nu: (te[i], 0, 0, 0))
    return pl.pallas_call(
        _expert_kernel,
        grid_spec=pltpu.PrefetchScalarGridSpec(
            num_scalar_prefetch=2,
            grid=(s // tm,),
            in_specs=[row, wspec(wg), wspec(wu), wspec(wd)],
            out_specs=row,
            scratch_shapes=[pltpu.VMEM((tm, d), F32)]),
        out_shape=jax.ShapeDtypeStruct((s, d), F32),
        compiler_params=_cparams(1),
        name="moe_experts",
    )(tile_expert, n_used, xs, wg, wu, wd)


def _combine_kernel(dest_hbm, eo_hbm, x_ref, wt_ref, mod_ref, o_ref, dest_smem, buf, sem_idx, sem_rows):
    i = pl.program_id(0)
    tm = x_ref.shape[0]
    idx_copy = pltpu.make_async_copy(dest_hbm.at[i], dest_smem, sem_idx)
    idx_copy.start()
    idx_copy.wait()

    def issue(r, carry):
        _row_copy(eo_hbm, dest_smem[2 * r], buf.at[0], r, sem_rows).start()
        _row_copy(eo_hbm, dest_smem[2 * r + 1], buf.at[1], r, sem_rows).start()
        return carry

    lax.fori_loop(0, tm, issue, 0)

    def drain(r, carry):
        _row_copy(eo_hbm, 0, buf.at[0], 0, sem_rows).wait()
        return carry

    lax.fori_loop(0, 2 * tm, drain, 0)
    wt = wt_ref[...]
    y = wt[:, 0:1] * buf[0] + wt[:, 1:2] * buf[1]
    o_ref[...] = x_ref[...] + mod_ref[0, 5:6, :] * y


def _combine(dest, eo, xa, wts, mods, n_rows, n_batch, seq_len):
    d = xa.shape[1]
    tm = TOKEN_TILE
    row = lambda w: pl.BlockSpec((tm, w), lambda i: (i, 0))
    return pl.pallas_call(
        _combine_kernel,
        grid=(n_rows // tm,),
        in_specs=[pl.BlockSpec(memory_space=pl.ANY), pl.BlockSpec(memory_space=pl.ANY), row(d), row(LANES),
                  pl.BlockSpec((1, 6, d), _mod_index(seq_len // tm, n_batch))],
        out_specs=row(d),
        out_shape=jax.ShapeDtypeStruct((n_rows, d), F32),
        scratch_shapes=[pltpu.SMEM((2 * tm,), jnp.int32), pltpu.VMEM((2, tm, d), F32),
                        pltpu.SemaphoreType.DMA(()), pltpu.SemaphoreType.DMA(())],
        compiler_params=_cparams(1),
        name="moe_combine",
    )(dest.reshape(n_rows // tm, 2 * tm), eo, xa, wts, mods)


def _moe(xa, mods, g2, r_pad, wg, wu, wd, n_rows, n_batch, seq_len):
    h, idx, wts = _router(xa, mods, g2, r_pad, n_rows, n_batch, seq_len)
    tm = EXPERT_TILE
    e_flat = idx[:, :2].reshape(-1)
    onehot = (e_flat[:, None] == jnp.arange(N_EXPERTS, dtype=jnp.int32)[None, :]).astype(jnp.int32)
    csum = jnp.cumsum(onehot, axis=0)
    counts = csum[-1]
    rank = jnp.sum(onehot * csum, axis=1) - 1
    padded = ((counts + tm - 1) // tm) * tm
    ends = jnp.cumsum(padded)
    starts = ends - padded
    dest = (jnp.sum(onehot * starts[None, :], axis=1) + rank).astype(jnp.int32)
    n_slots = 2 * n_rows + N_EXPERTS * tm
    tile_start = jnp.arange(n_slots // tm, dtype=jnp.int32) * tm
    tile_expert = jnp.minimum(jnp.sum((tile_start[:, None] >= ends[None, :]).astype(jnp.int32), axis=1),
                              N_EXPERTS - 1).astype(jnp.int32)
    n_used = (ends[-1:] // tm).astype(jnp.int32)
    xs = _dispatch(dest, h, jnp.zeros((n_slots, xa.shape[1]), F32))
    eo = _experts(tile_expert, n_used, xs, wg, wu, wd)
    return _combine(dest, eo, xa, wts, mods, n_rows, n_batch, seq_len)


def _softplus(v):
    return jnp.maximum(v, 0.0) + jnp.log1p(jnp.exp(-jnp.abs(v)))


def _ssd_kernel(xl_ref, dl_ref, xc_ref, dc_ref, cw_ref, cb_ref, dtb_ref, alog_ref, dsk_ref, *rest, want_ctx):
    if want_ctx:
        yl_ref, yc_ref, xs_l, dt_l, xs_c, dt_c, st_ref = rest
    else:
        yl_ref, xs_l, dt_l, xs_c, dt_c, st_ref = rest
        yc_ref = None
    ck = SSD_CHUNK
    hp = SSD_HEAD_DIM
    ns = SSD_STATE
    row = lax.broadcasted_iota(jnp.int32, (ck, 1), 0)
    li = lax.broadcasted_iota(jnp.int32, (ck, ck), 0)
    si = lax.broadcasted_iota(jnp.int32, (ck, ck), 1)
    masks = (si <= li, si >= li)
    tris = (masks[0].astype(F32), masks[1].astype(F32))
    a_row = -jnp.exp(alog_ref[...])
    dskip = dsk_ref[...]
    nt = (((1,), (1,)), ((), ()))

    def prep(raw_ref, dtraw_ref, xs_s, dt_s):
        n = raw_ref.shape[0]
        nk = n // ck

        def body(k, carry):
            r0 = pl.multiple_of(k * ck, ck)
            a = raw_ref[pl.ds(r0, ck), :]
            top = raw_ref[pl.ds(pl.multiple_of(jnp.maximum(r0 - 8, 0), 8), 8), :][7:8, :]
            bot = raw_ref[pl.ds(pl.multiple_of(jnp.minimum(r0 + ck, n - 8), 8), 8), :][0:1, :]
            top = jnp.where(k > 0, top, 0.0)
            bot = jnp.where(k < nk - 1, bot, 0.0)
            prev = jnp.where(row == 0, top, pltpu.roll(a, 1, 0))
            nxt = jnp.where(row == ck - 1, bot, pltpu.roll(a, ck - 1, 0))
            xs_s[pl.ds(r0, ck), :] = _silu(prev * cw_ref[0:1, :] + a * cw_ref[1:2, :] + nxt * cw_ref[2:3, :]
                                           + cb_ref[...])
            dt_s[pl.ds(r0, ck), :] = _softplus(dtraw_ref[pl.ds(r0, ck), :] + dtb_ref[...])
            return carry

        lax.fori_loop(0, nk, body, 0)

    def run(xs_s, dt_s, y_ref, dr, first):
        nk = xs_s.shape[0] // ck

        def body(k, carry):
            kk = k if dr == 0 else nk - 1 - k
            r0 = pl.multiple_of(kk * ck, ck)
            xc = xs_s[pl.ds(r0, ck), :]
            dtc = dt_s[pl.ds(r0, ck), :]
            la = jnp.dot(tris[dr], dtc * a_row, precision=lax.Precision.HIGHEST, preferred_element_type=F32)
            la_t = la.T
            la_end = la[ck - 1:ck, :] if dr == 0 else la[0:1, :]
            ys = []
            for g in range(SSD_GROUPS):
                b_g = xc[:, SSD_WIDTH + g * ns:SSD_WIDTH + (g + 1) * ns].astype(BF16)
                c0 = SSD_WIDTH + SSD_GROUPS * ns + g * ns
                c_g = xc[:, c0:c0 + ns].astype(BF16)
                if y_ref is not None:
                    scores = lax.dot_general(c_g, b_g, nt, preferred_element_type=F32)
                for hh in range(SSD_HEADS // SSD_GROUPS):
                    h = g * (SSD_HEADS // SSD_GROUPS) + hh
                    col = dr * SSD_HEADS + h
                    xdt = xc[:, h * hp:(h + 1) * hp] * dtc[:, col:col + 1]
                    la_col = la[:, col:col + 1]
                    st = st_ref[col]
                    if y_ref is not None:
                        decay = jnp.exp(jnp.where(masks[dr], la_col - la_t[col:col + 1, :], -jnp.inf))
                        y = jnp.dot((scores * decay).astype(BF16), xdt.astype(BF16), preferred_element_type=F32)
                        y = y + lax.dot_general(c_g, st.astype(BF16), nt,
                                                preferred_element_type=F32) * jnp.exp(la_col)
                        ys.append(y)
                    le = la_end[:, col:col + 1]
                    xw_t = (xdt * jnp.exp(le - la_col)).T.astype(BF16)
                    st_ref[col] = st * jnp.exp(le) + jnp.dot(xw_t, b_g, preferred_element_type=F32)
            if y_ref is not None:
                y = jnp.concatenate(ys, axis=1)
                if first:
                    y_ref[pl.ds(r0, ck), :] = xc[:, 0:SSD_WIDTH] * dskip + y
                else:
                    y_ref[pl.ds(r0, ck), :] += y
            return carry

        lax.fori_loop(0, nk, body, 0)

    prep(xl_ref, dl_ref, xs_l, dt_l)
    prep(xc_ref, dc_ref, xs_c, dt_c)
    st_ref[...] = jnp.zeros_like(st_ref)
    for dr in range(2):
        run(xs_c, dt_c, yc_ref, dr, dr == 0)
        run(xs_l, dt_l, yl_ref, dr, dr == 0)


def _ssd(xbc, dtp, conv_w, conv_b, dt_bias, a_log, d_skip, n_batch, seq_len, ctx_len, want_ctx):
    ctx0 = n_batch * seq_len // ctx_len
    pad = lambda v: jnp.pad(v.reshape(1, -1), ((0, 0), (0, LANES - v.size)))
    lat = lambda w: pl.BlockSpec((seq_len, w), lambda b: (b, 0))
    ctx = lambda w: pl.BlockSpec((ctx_len, w), lambda b: (ctx0 + b, 0))
    const = lambda a: pl.BlockSpec(a.shape, lambda b: (0,) * a.ndim)
    consts = (conv_w, conv_b.reshape(1, -1), pad(dt_bias), pad(a_log),
              jnp.repeat(d_skip, SSD_HEAD_DIM).reshape(1, -1))
    out_specs = [lat(SSD_WIDTH)]
    out_shape = [jax.ShapeDtypeStruct((n_batch * seq_len, SSD_WIDTH), F32)]
    if want_ctx:
        out_specs.append(pl.BlockSpec((ctx_len, SSD_WIDTH), lambda b: (b, 0)))
        out_shape.append(jax.ShapeDtypeStruct((n_batch * ctx_len, SSD_WIDTH), F32))
    return pl.pallas_call(
        functools.partial(_ssd_kernel, want_ctx=want_ctx),
        grid=(n_batch,),
        in_specs=[lat(SSD_XBC_COLS), lat(LANES), ctx(SSD_XBC_COLS), ctx(LANES)] + [const(a) for a in consts],
        out_specs=out_specs,
        out_shape=out_shape,
        scratch_shapes=[pltpu.VMEM((seq_len, SSD_XBC_COLS), F32), pltpu.VMEM((seq_len, LANES), F32),
                        pltpu.VMEM((ctx_len, SSD_XBC_COLS), F32), pltpu.VMEM((ctx_len, LANES), F32),
                        pltpu.VMEM((2 * SSD_HEADS, SSD_HEAD_DIM, SSD_STATE), F32)],
        compiler_params=_cparams(1),
        name="ssd",
    )(xbc, dtp, xbc, dtp, *consts)


HY_BLOCK = 256
HY_CH_STEP = 8


def _hyena_tables(seq_len):
    nj = 2 * seq_len
    lag = jnp.arange(nj, dtype=jnp.int32) - seq_len
    dist = jnp.abs(lag)
    pos = jnp.minimum(dist, seq_len - 1)
    t = jnp.linspace(0.0, 1.0, seq_len, dtype=F32)[pos]
    w = ((2.0 * math.pi / seq_len) * jnp.arange(seq_len, dtype=F32))[pos]
    bands = (HYENA_POS_DIM - 1) // 2
    freqs = jnp.linspace(1e-4, bands - 1, bands, dtype=F32)[None, :]
    z = jnp.concatenate([t[:, None], jnp.cos(freqs * w[:, None]), -jnp.sin(freqs * w[:, None])], axis=-1)
    zt = jnp.pad(z.T, ((0, (-HYENA_POS_DIM) % 8), (0, 0)))
    deltas = jnp.abs(jnp.linspace(math.log(HYENA_DECAY_TARGET) / HYENA_SLOW_DECAY,
                                  math.log(HYENA_DECAY_TARGET) / HYENA_FAST_DECAY, HYENA_WIDTH, dtype=F32))
    dec = jnp.exp(-t[None, :] * deltas[:, None]) * (dist < seq_len).astype(F32)[None, :]
    fwd = (lag >= 0).astype(F32)[None, :]
    return zt, dec, fwd


def _hyfilt_kernel(zt_ref, dec_ref, fwd_ref, w1_ref, b1_ref, f1_ref, w2_ref, b2_ref, f2_ref, w3_ref, o_ref):
    hi = lax.Precision.HIGHEST
    h = jnp.sin(f1_ref[...] * (jnp.dot(w1_ref[...], zt_ref[...], precision=hi, preferred_element_type=F32)
                               + b1_ref[...]))
    h = jnp.sin(f2_ref[...] * (jnp.dot(w2_ref[...], h, precision=hi, preferred_element_type=F32) + b2_ref[...]))
    hw = jnp.dot(w3_ref[...], h, precision=hi, preferred_element_type=F32)
    fwd = fwd_ref[...] > 0.5
    dec = dec_ref[...]
    nw = HYENA_WIDTH
    for o in range(HYENA_ORDER):
        o_ref[o] = jnp.where(fwd, hw[o * nw:(o + 1) * nw], hw[(HYENA_ORDER + o) * nw:(HYENA_ORDER + o + 1) * nw]) * dec


def _hyena_filters(tables, w1, b1, f1, w2, b2, f2, w3):
    zt, dec, fwd = tables
    nj = zt.shape[1]
    tj = 512
    col = lambda v: v.reshape(-1, 1)
    w1t = jnp.pad(w1.T, ((0, 0), (0, zt.shape[0] - w1.shape[0])))
    consts = (w1t, col(b1), col(f1), w2.T, col(b2), col(f2), w3.T)
    lanes = lambda a: pl.BlockSpec((a.shape[0], tj), lambda j: (0, j))
    const = lambda a: pl.BlockSpec(a.shape, lambda j: (0, 0))
    return pl.pallas_call(
        _hyfilt_kernel,
        grid=(nj // tj,),
        in_specs=[lanes(zt), lanes(dec), lanes(fwd)] + [const(a) for a in consts],
        out_specs=pl.BlockSpec((HYENA_ORDER, HYENA_WIDTH, tj), lambda j: (0, 0, j)),
        out_shape=jax.ShapeDtypeStruct((HYENA_ORDER, HYENA_WIDTH, nj), F32),
        compiler_params=_cparams(1),
        name="hyena_filters",
    )(zt, dec, fwd, *consts)


def _hyconv_kernel(cw_ref, cb_ref, hb_ref, v_ref, x1_ref, x2_ref, kf_ref, o_ref):
    n_ch, n_b, seq_len = v_ref.shape
    nb = seq_len // HY_BLOCK
    blk = HY_BLOCK
    c_base = pl.program_id(0) * n_ch
    lane = lax.broadcasted_iota(jnp.int32, (1, seq_len), 1)

    def sconv(x, ch):
        prev = jnp.where(lane == 0, 0.0, pltpu.roll(x, 1, 1))
        nxt = jnp.where(lane == seq_len - 1, 0.0, pltpu.roll(x, seq_len - 1, 1))
        return prev * cw_ref[0, ch] + x * cw_ref[1, ch] + nxt * cw_ref[2, ch] + cb_ref[ch]

    def long_conv(vals, kf_row):
        skew = pltpu.roll(jnp.broadcast_to(kf_row, (blk, 2 * seq_len)), 0, 1, stride=1, stride_axis=0)
        vb = vals.astype(BF16)
        acc = [None] * nb
        for d in range(-(nb - 1), nb):
            tt = skew[:, seq_len + d * blk:seq_len + (d + 1) * blk].astype(BF16)
            sis = list(range(max(0, -d), min(nb, nb - d)))
            lhs = [vb[:, s * blk:(s + 1) * blk] for s in sis]
            lhs = lhs[0] if len(lhs) == 1 else jnp.concatenate(lhs, axis=0)
            out = jnp.dot(lhs, tt, preferred_element_type=F32)
            for idx, s in enumerate(sis):
                piece = out[idx * n_b:(idx + 1) * n_b]
                acc[s + d] = piece if acc[s + d] is None else acc[s + d] + piece
        return acc[0] if nb == 1 else jnp.concatenate(acc, axis=1)

    def channel(cc, carry):
        ch = c_base + cc
        v = sconv(v_ref[cc], ch)
        x1 = sconv(x1_ref[cc], HYENA_WIDTH + ch)
        x2 = sconv(x2_ref[cc], 2 * HYENA_WIDTH + ch)
        z = x1 * (long_conv(v, kf_ref[0, cc]) + v * hb_ref[0, ch])
        o_ref[cc] = x2 * (long_conv(z, kf_ref[1, cc]) + z * hb_ref[1, ch])
        return carry

    lax.fori_loop(0, n_ch, channel, 0)


def _hyena_conv(hyt, kf, conv_w, conv_b, hy_bias, n_batch, seq_len, row0):
    t = hyt.shape[1]
    cs = HY_CH_STEP
    nw = HYENA_WIDTH
    view = hyt.reshape(3 * nw, t // seq_len, seq_len)
    blk0 = row0 // n_batch
    stream = lambda k: pl.BlockSpec((cs, n_batch, seq_len), lambda c: (k * (nw // cs) + c, blk0, 0))
    smem = pl.BlockSpec(memory_space=pltpu.SMEM)
    return pl.pallas_call(
        _hyconv_kernel,
        grid=(nw // cs,),
        in_specs=[smem, smem, smem, stream(0), stream(1), stream(2),
                  pl.BlockSpec((HYENA_ORDER, cs, 1, 2 * seq_len), lambda c: (0, c, 0, 0))],
        out_specs=pl.BlockSpec((cs, n_batch, seq_len), lambda c: (c, 0, 0)),
        out_shape=jax.ShapeDtypeStruct((nw, n_batch, seq_len), F32),
        compiler_params=_cparams(1),
        name="hyena_conv",
    )(conv_w, conv_b, hy_bias, view, view, view, kf.reshape(HYENA_ORDER, nw, 1, 2 * seq_len))


def _rope_tables(seq_len, extra):
    rows = seq_len // GRID_W
    row = jnp.repeat(jnp.arange(rows, dtype=F32), GRID_W)
    col = jnp.tile(jnp.arange(GRID_W, dtype=F32), rows)
    inv = ROPE_THETA ** (-jnp.arange(0, ROPE_AXIS_DIM, 2, dtype=F32) / ROPE_AXIS_DIM)
    ang = jnp.stack([row[:, None] * inv, col[:, None] * inv], axis=1)
    cos = jnp.cos(ang)
    sin = jnp.sin(ang)
    cos_h = jnp.concatenate([cos, cos], axis=-1).reshape(seq_len, HEAD_DIM)
    sin_h = jnp.concatenate([-sin, sin], axis=-1).reshape(seq_len, HEAD_DIM)
    cos_t = jnp.concatenate([jnp.tile(cos_h, (1, LANES // HEAD_DIM)), jnp.ones((extra, LANES), F32)], axis=0)
    sin_t = jnp.concatenate([jnp.tile(sin_h, (1, LANES // HEAD_DIM)), jnp.zeros((extra, LANES), F32)], axis=0)
    return cos_t, sin_t


def _block_diag_ones(n, seg):
    i = jnp.arange(n) // seg
    return (i[:, None] == i[None, :]).astype(BF16)


def _chunked_in(w):
    *lead, d, f = w.shape
    w = w.reshape(*lead, d, f // FFN_CHUNK, FFN_CHUNK)
    return jnp.moveaxis(w, -2, -3).astype(BF16)


def _chunked_out(w):
    *lead, f, d = w.shape
    return w.reshape(*lead, f // FFN_CHUNK, FFN_CHUNK, d).astype(BF16)


def kernel(x, c, ctx, c_ctx, w_ada, b_ada, norm1, norm2, w_in, w_out, q_norm, k_norm, att_sinks, att_out_norm, ssd_conv_w, ssd_conv_b, ssd_dt_bias, ssd_a_log, ssd_d, ssd_norm, hy_conv_w, hy_conv_b, hy_w1, hy_b1, hy_f1, hy_w2, hy_b2, hy_f2, hy_w3, hy_bias, hy_out_norm, ffn_w_gate, ffn_w_up, ffn_w_down, moe_router, moe_w_gate, moe_w_up, moe_w_down):
    n_batch, seq_len, d = x.shape
    ctx_len = ctx.shape[1]
    n_lat = n_batch * seq_len
    n_ctx = n_batch * ctx_len
    depth = w_in.shape[0]
    xa = jnp.concatenate([x.reshape(n_lat, d), ctx.reshape(n_ctx, d)], axis=0)

    cc = jnp.concatenate([c, c_ctx[None, :]], axis=0)
    pad_rows = (-cc.shape[0]) % 8
    cc = jnp.pad(cc, ((0, pad_rows), (0, 0)))
    mods_all = _adaln(cc, w_ada, b_ada)[:, :n_batch + 1].reshape(depth, n_batch + 1, 6, d)

    cos_t, sin_t = _rope_tables(seq_len, TOKEN_TILE)
    bd_q = _block_diag_ones(Q_COLS, HEAD_DIM)
    bd_h = _block_diag_ones(HYENA_WIDTH, HYENA_WIDTH // HYENA_GROUPS)
    hy_tab_l = _hyena_tables(seq_len)
    hy_tab_c = _hyena_tables(ctx_len)

    for i in range(depth):
        last = i == depth - 1
        j = i // 2
        mods = mods_all[i]
        wi = w_in[i]
        c_dt = QKV_W + ZX_W
        w_cat = jnp.concatenate([wi[:, :c_dt], wi[:, c_dt + SSD_DT_COLS:], wi[:, c_dt:c_dt + SSD_DT_COLS],
                                 jnp.zeros((d, LANES - SSD_DT_COLS), F32)], axis=1).astype(BF16)
        qg = jnp.tile(q_norm[i], Q_COLS // HEAD_DIM)[None, :]
        kg = jnp.tile(k_norm[i], KV_COLS // HEAD_DIM)[None, :]
        q, k, v, z, xbc, hyt, dtp = _inproj(xa, mods, norm1[i][None, :], w_cat, cos_t, sin_t, qg, kg, bd_q,
                                            n_batch, seq_len)

        att_l = _attention(att_sinks[i], q, k, v, n_batch, seq_len, ctx_len, True)
        ssd_out = _ssd(xbc, dtp, ssd_conv_w[i], ssd_conv_b[i], ssd_dt_bias[i], ssd_a_log[i], ssd_d[i],
                       n_batch, seq_len, ctx_len, not last)
        filt = (hy_w1[i], hy_b1[i], hy_f1[i], hy_w2[i], hy_b2[i], hy_f2[i], hy_w3[i])
        hyo_l = _hyena_conv(hyt, _hyena_filters(hy_tab_l, *filt), hy_conv_w[i], hy_conv_b[i], hy_bias[i],
                            n_batch, seq_len, 0).reshape(HYENA_WIDTH, n_lat)
        if last:
            att = att_l
            sy = ssd_out[0]
            hyo = hyo_l
            n_rows = n_lat
        else:
            att_c = _attention(att_sinks[i], q, k, v, n_batch, seq_len, ctx_len, False)
            att = jnp.concatenate([att_l, att_c], axis=0)
            sy = jnp.concatenate(ssd_out, axis=0)
            hyo_c = _hyena_conv(hyt, _hyena_filters(hy_tab_c, *filt), hy_conv_w[i], hy_conv_b[i], hy_bias[i],
                                n_batch, ctx_len, n_lat // ctx_len).reshape(HYENA_WIDTH, n_ctx)
            hyo = jnp.concatenate([hyo_l, hyo_c], axis=1)
            n_rows = n_lat + n_ctx
        xa = _merge(att, sy, z, hyo, xa, mods, att_out_norm[i][None, :], ssd_norm[i][None, :],
                    hy_out_norm[i][None, :], bd_h, w_out[i].astype(BF16), n_batch, seq_len)

        g2 = norm2[i][None, :]
        if i % 2 == 0:
            xa = _ffn(xa, mods, g2, _chunked_in(ffn_w_gate[j]), _chunked_in(ffn_w_up[j]),
                      _chunked_out(ffn_w_down[j]), n_batch, seq_len)
        else:
            r_pad = jnp.pad(moe_router[j], ((0, 0), (0, LANES - N_EXPERTS)))
            xa = _moe(xa, mods, g2, r_pad, _chunked_in(moe_w_gate[j]), _chunked_in(moe_w_up[j]),
                      _chunked_out(moe_w_down[j]), n_rows, n_batch, seq_len)
    return xa[:n_lat].reshape(n_batch, seq_len, d)
```

```python
import functools
import math

import jax
import jax.numpy as jnp
from jax import lax
from jax.experimental import pallas as pl
from jax.experimental.pallas import tpu as pltpu

F32 = jnp.float32
BF16 = jnp.bfloat16

D_MODEL = 1024
DEPTH = 4
GRID_W = 64
EPS = 1e-6
HEAD_DIM = 64
ATT_WIDTH = 512
ATT_HEADS = 8
ATT_KV_HEADS = 2
ATT_GROUP = 4
WINDOW = 128
ATT_BLOCK = 128
ROPE_THETA = 10000.0
ROPE_AXIS_DIM = 32
SSD_WIDTH = 256
SSD_HEAD_DIM = 64
SSD_HEADS = 4
SSD_STATE = 64
SSD_GROUPS = 2
SSD_CHUNK = 128
HYENA_WIDTH = 256
HYENA_GROUPS = 4
HYENA_ORDER = 2
HYENA_POS_DIM = 33
HYENA_FAST_DECAY = 0.3
HYENA_SLOW_DECAY = 1.5
HYENA_DECAY_TARGET = 1e-2
Q_COLS = 512
KV_COLS = 128
SSD_XBC_COLS = 512
SSD_DT_COLS = 8
HY_COLS = 768
FFN_DIM = 2816
N_EXPERTS = 8
FFN_CHUNK = 256
N_FFN_CHUNKS = FFN_DIM // FFN_CHUNK
LANES = 128
QKV_W = Q_COLS + 2 * KV_COLS
ZX_W = SSD_WIDTH + SSD_XBC_COLS
PROJ_PAD = QKV_W + ZX_W + HY_COLS + LANES
VMEM_LIMIT = 56 * 1024 * 1024
TOKEN_TILE = 512
EXPERT_TILE = 512


def _cparams(n_axes):
    return pltpu.CompilerParams(dimension_semantics=("arbitrary",) * n_axes,
                                vmem_limit_bytes=VMEM_LIMIT)


def _silu(v):
    return v / (1.0 + jnp.exp(-v))


def _modnorm(x, g, scale, shift):
    ms = jnp.mean(x * x, axis=-1, keepdims=True)
    return x * lax.rsqrt(ms + EPS) * g * (1.0 + scale) + shift


def _segsum(t, bd):
    hi = t.astype(BF16)
    lo = (t - hi.astype(F32)).astype(BF16)
    return (jnp.dot(hi, bd, preferred_element_type=F32)
            + jnp.dot(lo, bd, preferred_element_type=F32))


def _mod_index(tiles_per_batch, n_batch):
    return lambda i: (jnp.minimum(i // tiles_per_batch, n_batch), 0, 0)


def _adaln_kernel(c_ref, w_ref, b_ref, o_ref):
    s = _silu(c_ref[...]).astype(BF16)
    o_ref[...] = jnp.dot(s, w_ref[...].astype(BF16), preferred_element_type=F32) + b_ref[...]


def _adaln(cc, w_ada, b_ada):
    depth, d, n = w_ada.shape
    r = cc.shape[0]
    tn = 512
    return pl.pallas_call(
        _adaln_kernel,
        grid=(depth, n // tn),
        in_specs=[pl.BlockSpec((r, d), lambda l, j: (0, 0)),
                  pl.BlockSpec((None, d, tn), lambda l, j: (l, 0, j)),
                  pl.BlockSpec((None, 1, tn), lambda l, j: (l, 0, j))],
        out_specs=pl.BlockSpec((None, r, tn), lambda l, j: (l, 0, j)),
        out_shape=jax.ShapeDtypeStruct((depth, r, n), F32),
        compiler_params=_cparams(2),
        name="adaln",
    )(cc, w_ada, b_ada.reshape(depth, 1, n))


def _inproj_kernel(x_ref, mod_ref, g1_ref, w_ref, cos_ref, sin_ref, qg_ref, kg_ref, bd_ref,
                   q_ref, k_ref, v_ref, z_ref, xbc_ref, hyt_ref, dt_ref):
    x = x_ref[...]
    h = _modnorm(x, g1_ref[...], mod_ref[0, 1:2, :], mod_ref[0, 0:1, :]).astype(BF16)
    pq = jnp.dot(h, w_ref[:, 0:QKV_W], preferred_element_type=F32)
    cos = cos_ref[...]
    sin = sin_ref[...]
    lane = lax.broadcasted_iota(jnp.int32, (1, LANES), 1)
    first_half = (lane % 32) < 16

    def rope(t):
        partner = jnp.where(first_half, pltpu.roll(t, LANES - 16, 1), pltpu.roll(t, 16, 1))
        return t * cos + partner * sin

    q = pq[:, 0:Q_COLS]
    qn = q * lax.rsqrt(_segsum(q * q, bd_ref[...]) * (1.0 / HEAD_DIM) + EPS) * qg_ref[...]
    scale = HEAD_DIM ** -0.5
    for j in range(Q_COLS // LANES):
        pair = (rope(qn[:, LANES * j:LANES * (j + 1)]) * scale).astype(BF16)
        q_ref[2 * j] = pair[:, 0:HEAD_DIM]
        q_ref[2 * j + 1] = pair[:, HEAD_DIM:LANES]
    k = pq[:, Q_COLS:Q_COLS + KV_COLS]
    kn = k * lax.rsqrt(_segsum(k * k, bd_ref[0:KV_COLS, 0:KV_COLS]) * (1.0 / HEAD_DIM) + EPS) * kg_ref[...]
    kt = rope(kn).T.astype(BF16)
    vv = pq[:, Q_COLS + KV_COLS:QKV_W].astype(BF16)
    for j in range(ATT_KV_HEADS):
        k_ref[j] = kt[j * HEAD_DIM:(j + 1) * HEAD_DIM, :]
        v_ref[j] = vv[:, j * HEAD_DIM:(j + 1) * HEAD_DIM]
    zx = jnp.dot(h, w_ref[:, QKV_W:QKV_W + ZX_W], preferred_element_type=F32)
    z_ref[...] = zx[:, 0:SSD_WIDTH]
    xbc_ref[...] = zx[:, SSD_WIDTH:ZX_W]
    hyt_ref[...] = jnp.dot(h, w_ref[:, QKV_W + ZX_W:QKV_W + ZX_W + HY_COLS], preferred_element_type=F32).T
    dt_ref[...] = jnp.dot(h, w_ref[:, QKV_W + ZX_W + HY_COLS:PROJ_PAD], preferred_element_type=F32)


def _inproj(xa, mods, g1, w_cat, cos_t, sin_t, qg, kg, bd, n_batch, seq_len):
    t, d = xa.shape
    tm = TOKEN_TILE
    tpb = seq_len // tm
    n_lat = n_batch * tpb
    rope_idx = lambda i: (jnp.where(i < n_lat, i % tpb, tpb), 0)
    row = lambda w: pl.BlockSpec((tm, w), lambda i: (i, 0))
    heads = lambda nh: pl.BlockSpec((nh, tm, HEAD_DIM), lambda i: (0, i, 0))
    const = lambda a: pl.BlockSpec(a.shape, lambda i: (0,) * a.ndim)
    return pl.pallas_call(
        _inproj_kernel,
        grid=(t // tm,),
        in_specs=[row(d),
                  pl.BlockSpec((1, 6, d), _mod_index(tpb, n_batch)),
                  const(g1), const(w_cat),
                  pl.BlockSpec((tm, LANES), rope_idx), pl.BlockSpec((tm, LANES), rope_idx),
                  const(qg), const(kg), const(bd)],
        out_specs=[heads(ATT_HEADS), pl.BlockSpec((ATT_KV_HEADS, HEAD_DIM, tm), lambda i: (0, 0, i)),
                   heads(ATT_KV_HEADS), row(SSD_WIDTH), row(SSD_XBC_COLS),
                   pl.BlockSpec((HY_COLS, tm), lambda i: (0, i)), row(LANES)],
        out_shape=[jax.ShapeDtypeStruct((ATT_HEADS, t, HEAD_DIM), BF16),
                   jax.ShapeDtypeStruct((ATT_KV_HEADS, HEAD_DIM, t), BF16),
                   jax.ShapeDtypeStruct((ATT_KV_HEADS, t, HEAD_DIM), BF16),
                   jax.ShapeDtypeStruct((t, SSD_WIDTH), F32),
                   jax.ShapeDtypeStruct((t, SSD_XBC_COLS), F32),
                   jax.ShapeDtypeStruct((HY_COLS, t), F32),
                   jax.ShapeDtypeStruct((t, LANES), F32)],
        compiler_params=_cparams(1),
        name="inproj",
    )(xa, mods, g1, w_cat, cos_t, sin_t, qg, kg, bd)


def _attn_kernel(sink_ref, q_ref, *refs, n_q, band):
    if band:
        k_ref, v_ref, kc_ref, vc_ref, o_ref, bias_ref = refs
        seq_len = v_ref.shape[1]
        assert n_q >= 3
    else:
        kc_ref, vc_ref, o_ref = refs
    qb = ATT_BLOCK
    rows = ATT_GROUP * qb
    row_id = lax.broadcasted_iota(jnp.int32, (rows, 1), 0)
    nt = (((1,), (1,)), ((), ()))

    if band:
        @pl.when(pl.program_id(0) == 0)
        def _():
            rel0 = (lax.broadcasted_iota(jnp.int32, (rows, band), 1)
                    - lax.broadcasted_iota(jnp.int32, (rows, band), 0) % qb)
            for var in range(3):
                bias_ref[var] = jnp.where(jnp.abs(rel0 - var * WINDOW) <= WINDOW, 0.0, -jnp.inf)

    for j in range(ATT_KV_HEADS):
        kc = kc_ref[j]
        vc = vc_ref[j]
        snk = jnp.zeros((rows, 1), F32)
        for g in range(ATT_GROUP):
            snk = jnp.where(row_id // qb == g, sink_ref[ATT_GROUP * j + g], snk)

        def block(i, carry, j=j, kc=kc, vc=vc, snk=snk):
            q0 = pl.multiple_of(i * qb, qb)
            qh = jnp.concatenate([q_ref[ATT_GROUP * j + g, pl.ds(q0, qb), :] for g in range(ATT_GROUP)],
                                 axis=0)
            s_ctx = jnp.dot(qh, kc, preferred_element_type=F32)
            m = jnp.maximum(jnp.max(s_ctx, axis=-1, keepdims=True), snk)
            if band:
                k0 = pl.multiple_of(jnp.clip(q0 - WINDOW, 0, seq_len - band), qb)
                var = jnp.where(i == 0, 0, jnp.where(i == n_q - 1, 2, 1))
                s_loc = jnp.dot(qh, k_ref[j, :, pl.ds(k0, band)], preferred_element_type=F32) + bias_ref[var]
                m = jnp.maximum(m, jnp.max(s_loc, axis=-1, keepdims=True))
            p_ctx = jnp.exp(s_ctx - m)
            den = jnp.sum(p_ctx, axis=-1, keepdims=True) + jnp.exp(snk - m)
            o = jnp.dot(p_ctx.astype(BF16), vc, preferred_element_type=F32)
            if band:
                p_loc = jnp.exp(s_loc - m)
                den = den + jnp.sum(p_loc, axis=-1, keepdims=True)
                o = o + jnp.dot(p_loc.astype(BF16), v_ref[j, pl.ds(k0, band), :], preferred_element_type=F32)
            o = o / den
            for g in range(ATT_GROUP):
                c0 = (ATT_GROUP * j + g) * HEAD_DIM
                o_ref[pl.ds(q0, qb), c0:c0 + HEAD_DIM] = o[g * qb:(g + 1) * qb, :]
            return carry

        lax.fori_loop(0, n_q, block, 0, unroll=2)


def _attention(sinks, q, k, v, n_batch, seq_len, ctx_len, latent):
    ctx_blk0 = n_batch * seq_len // ctx_len
    kc_spec = pl.BlockSpec((ATT_KV_HEADS, HEAD_DIM, ctx_len), lambda b: (0, 0, ctx_blk0 + b))
    vc_spec = pl.BlockSpec((ATT_KV_HEADS, ctx_len, HEAD_DIM), lambda b: (0, ctx_blk0 + b, 0))
    smem = pl.BlockSpec(memory_space=pltpu.SMEM)
    scratch = []
    if latent:
        rows = seq_len
        band = ATT_BLOCK + 2 * WINDOW
        in_specs = [smem, pl.BlockSpec((ATT_HEADS, rows, HEAD_DIM), lambda b: (0, b, 0)),
                    pl.BlockSpec((ATT_KV_HEADS, HEAD_DIM, rows), lambda b: (0, 0, b)),
                    pl.BlockSpec((ATT_KV_HEADS, rows, HEAD_DIM), lambda b: (0, b, 0)), kc_spec, vc_spec]
        args = (sinks, q, k, v, k, v)
        scratch = [pltpu.VMEM((3, ATT_GROUP * ATT_BLOCK, band), F32)]
    else:
        rows = ctx_len
        band = 0
        in_specs = [smem, pl.BlockSpec((ATT_HEADS, rows, HEAD_DIM), lambda b: (0, ctx_blk0 + b, 0)),
                    kc_spec, vc_spec]
        args = (sinks, q, k, v)
    return pl.pallas_call(
        functools.partial(_attn_kernel, n_q=rows // ATT_BLOCK, band=band),
        grid=(n_batch,),
        in_specs=in_specs,
        out_specs=pl.BlockSpec((rows, ATT_WIDTH), lambda b: (b, 0)),
        out_shape=jax.ShapeDtypeStruct((n_batch * rows, ATT_WIDTH), F32),
        scratch_shapes=scratch,
        compiler_params=_cparams(1),
        name="attn_latent" if latent else "attn_ctx",
    )(*args)


def _merge_kernel(*refs, n_lat_tiles, with_ctx):
    if with_ctx:
        (attl_ref, attc_ref, syl_ref, syc_ref, hyl_ref, hyc_ref, z_ref, x_ref, mod_ref,
         ga_ref, gs_ref, gh_ref, bd_ref, w_ref, o_ref) = refs
        is_lat = pl.program_id(0) < n_lat_tiles
        att = jnp.where(is_lat, attl_ref[...], attc_ref[...])
        sy = jnp.where(is_lat, syl_ref[...], syc_ref[...])
        hy_t = jnp.where(is_lat, hyl_ref[...], hyc_ref[...])
    else:
        attl_ref, syl_ref, hyl_ref, z_ref, x_ref, mod_ref, ga_ref, gs_ref, gh_ref, bd_ref, w_ref, o_ref = refs
        att = attl_ref[...]
        sy = syl_ref[...]
        hy_t = hyl_ref[...]
    a = att * lax.rsqrt(jnp.mean(att * att, axis=-1, keepdims=True) + EPS) * ga_ref[...]
    s = sy * _silu(z_ref[...])
    s = s * lax.rsqrt(jnp.mean(s * s, axis=-1, keepdims=True) + EPS) * gs_ref[...]
    hy = hy_t.T
    hn = hy * lax.rsqrt(_segsum(hy * hy, bd_ref[...]) * (1.0 / (HYENA_WIDTH // HYENA_GROUPS)) + EPS) * gh_ref[...]
    y = jnp.dot(a.astype(BF16), w_ref[0:ATT_WIDTH, :], preferred_element_type=F32)
    y = y + jnp.dot(s.astype(BF16), w_ref[ATT_WIDTH:ATT_WIDTH + SSD_WIDTH, :], preferred_element_type=F32)
    y = y + jnp.dot(hn.astype(BF16), w_ref[ATT_WIDTH + SSD_WIDTH:, :], preferred_element_type=F32)
    o_ref[...] = x_ref[...] + mod_ref[0, 2:3, :] * y


def _merge(att, sy, hy, z, xa, mods, ga, gs, gh, bd, w_out, n_batch, seq_len):
    d = xa.shape[1]
    tm = TOKEN_TILE
    nl = att[0].shape[0] // tm
    with_ctx = att[1] is not None
    t = att[0].shape[0] + (att[1].shape[0] if with_ctx else 0)
    row = lambda w: pl.BlockSpec((tm, w), lambda i: (i, 0))
    lat_row = lambda w: pl.BlockSpec((tm, w), lambda i: (jnp.minimum(i, nl - 1), 0))
    ctx_row = lambda w: pl.BlockSpec((tm, w), lambda i: (jnp.maximum(i - nl, 0), 0))
    const = lambda a: pl.BlockSpec(a.shape, lambda i: (0,) * a.ndim)
    if with_ctx:
        streams = [att[0], att[1], sy[0], sy[1], hy[0], hy[1]]
        specs = [lat_row(ATT_WIDTH), ctx_row(ATT_WIDTH), lat_row(SSD_WIDTH), ctx_row(SSD_WIDTH),
                 pl.BlockSpec((HYENA_WIDTH, tm), lambda i: (0, jnp.minimum(i, nl - 1))),
                 pl.BlockSpec((HYENA_WIDTH, tm), lambda i: (0, jnp.maximum(i - nl, 0)))]
    else:
        streams = [att[0], sy[0], hy[0]]
        specs = [row(ATT_WIDTH), row(SSD_WIDTH), pl.BlockSpec((HYENA_WIDTH, tm), lambda i: (0, i))]
    return pl.pallas_call(
        functools.partial(_merge_kernel, n_lat_tiles=nl, with_ctx=with_ctx),
        grid=(t // tm,),
        in_specs=specs + [row(SSD_WIDTH), row(d),
                          pl.BlockSpec((1, 6, d), _mod_index(seq_len // tm, n_batch)),
                          const(ga), const(gs), const(gh), const(bd), const(w_out)],
        out_specs=row(d),
        out_shape=jax.ShapeDtypeStruct((t, d), F32),
        compiler_params=_cparams(1),
        name="merge_outproj",
    )(*streams, z, xa, mods, ga, gs, gh, bd, w_out)


def _swiglu_accumulate(h, wg_ref, wu_ref, wd_ref, acc_ref):
    acc_ref[...] = jnp.zeros_like(acc_ref)

    def chunk(c, carry):
        g = jnp.dot(h, wg_ref[c], preferred_element_type=F32)
        u = jnp.dot(h, wu_ref[c], preferred_element_type=F32)
        a = (_silu(g) * u).astype(BF16)
        acc_ref[...] += jnp.dot(a, wd_ref[c], preferred_element_type=F32)
        return carry

    lax.fori_loop(0, N_FFN_CHUNKS, chunk, 0)


def _ffn_kernel(x_ref, mod_ref, g2_ref, wg_ref, wu_ref, wd_ref, o_ref, acc_ref):
    x = x_ref[...]
    h = _modnorm(x, g2_ref[...], mod_ref[0, 4:5, :], mod_ref[0, 3:4, :]).astype(BF16)
    _swiglu_accumulate(h, wg_ref, wu_ref, wd_ref, acc_ref)
    o_ref[...] = x + mod_ref[0, 5:6, :] * acc_ref[...]


def _ffn(xa, mods, g2, wg, wu, wd, n_batch, seq_len):
    t, d = xa.shape
    tm = TOKEN_TILE
    row = pl.BlockSpec((tm, d), lambda i: (i, 0))
    resident = lambda a: pl.BlockSpec(a.shape, lambda i: (0,) * a.ndim, pipeline_mode=pl.Buffered(1))
    return pl.pallas_call(
        _ffn_kernel,
        grid=(t // tm,),
        in_specs=[row, pl.BlockSpec((1, 6, d), _mod_index(seq_len // tm, n_batch)),
                  pl.BlockSpec(g2.shape, lambda i: (0, 0)), resident(wg), resident(wu), resident(wd)],
        out_specs=row,
        out_shape=jax.ShapeDtypeStruct((t, d), F32),
        scratch_shapes=[pltpu.VMEM((tm, d), F32)],
        compiler_params=_cparams(1),
        name="ffn",
    )(xa, mods, g2, wg, wu, wd)


def _router_kernel(x_ref, mod_ref, g2_ref, r_ref, h_ref, idx_ref, wt_ref):
    h = _modnorm(x_ref[...], g2_ref[...], mod_ref[0, 4:5, :], mod_ref[0, 3:4, :])
    h_ref[...] = h
    logits = jnp.dot(h, r_ref[...], precision=lax.Precision.HIGHEST, preferred_element_type=F32)
    lane = lax.broadcasted_iota(jnp.int32, logits.shape, 1)
    neg = -jnp.inf
    l1 = jnp.where(lane < N_EXPERTS, logits, neg)
    m1 = jnp.max(l1, axis=-1, keepdims=True)
    i1 = jnp.min(jnp.where(l1 == m1, lane, LANES), axis=-1, keepdims=True)
    l2 = jnp.where(lane == i1, neg, l1)
    m2 = jnp.max(l2, axis=-1, keepdims=True)
    i2 = jnp.min(jnp.where(l2 == m2, lane, LANES), axis=-1, keepdims=True)
    e = jnp.exp(m2 - m1)
    w1 = 1.0 / (1.0 + e)
    w2 = e / (1.0 + e)
    idx_ref[...] = jnp.where(lane == 0, i1, jnp.where(lane == 1, i2, 0))
    wt_ref[...] = jnp.where(lane == 0, w1, jnp.where(lane == 1, w2, 0.0))


def _router(xa, mods, g2, r_pad, n_rows, n_batch, seq_len):
    d = xa.shape[1]
    tm = TOKEN_TILE
    row = lambda w: pl.BlockSpec((tm, w), lambda i: (i, 0))
    return pl.pallas_call(
        _router_kernel,
        grid=(n_rows // tm,),
        in_specs=[row(d), pl.BlockSpec((1, 6, d), _mod_index(seq_len // tm, n_batch)),
                  pl.BlockSpec(g2.shape, lambda i: (0, 0)), pl.BlockSpec(r_pad.shape, lambda i: (0, 0))],
        out_specs=[row(d), row(LANES), row(LANES)],
        out_shape=[jax.ShapeDtypeStruct((n_rows, d), F32),
                   jax.ShapeDtypeStruct((n_rows, LANES), jnp.int32),
                   jax.ShapeDtypeStruct((n_rows, LANES), F32)],
        compiler_params=_cparams(1),
        name="moe_router",
    )(xa, mods, g2, r_pad)


def _row_copy(src, src_row, dst, dst_row, sem):
    return pltpu.make_async_copy(src.at[pl.ds(src_row, 1), :], dst.at[pl.ds(dst_row, 1), :], sem)


DMA_ISSUE_UNROLL = 8


def _idx_copy(dest_hbm, dest_smem, sem_idx, tile, slot):
    n = dest_hbm.shape[1]
    half = dest_smem.at[pl.ds(pl.multiple_of(slot * n, n), n)]
    return pltpu.make_async_copy(dest_hbm.at[tile], half, sem_idx.at[slot])


def _dispatch_kernel(dest_hbm, h_ref, xs_in, xs_out, dest_smem, sem_idx, sem_rows):
    del xs_in
    i = pl.program_id(0)
    n = pl.num_programs(0)
    tm = h_ref.shape[0]
    slot = i % 2

    @pl.when(i == 0)
    def _():
        _idx_copy(dest_hbm, dest_smem, sem_idx, 0, 0).start()

    @pl.when(i + 1 < n)
    def _():
        _idx_copy(dest_hbm, dest_smem, sem_idx, i + 1, 1 - slot).start()

    _idx_copy(dest_hbm, dest_smem, sem_idx, i, slot).wait()

    base = slot * (2 * tm)

    def issue(r, carry):
        _row_copy(h_ref, r, xs_out, dest_smem[base + 2 * r], sem_rows).start()
        _row_copy(h_ref, r, xs_out, dest_smem[base + 2 * r + 1], sem_rows).start()
        return carry

    lax.fori_loop(0, tm, issue, 0, unroll=DMA_ISSUE_UNROLL)
    for _ in range(2):
        pltpu.make_async_copy(h_ref, xs_out.at[pl.ds(0, tm), :], sem_rows).wait()


def _dispatch(dest, h, xs0):
    n_rows, d = h.shape
    tm = TOKEN_TILE
    return pl.pallas_call(
        _dispatch_kernel,
        grid=(n_rows // tm,),
        in_specs=[pl.BlockSpec(memory_space=pl.ANY), pl.BlockSpec((tm, d), lambda i: (i, 0)),
                  pl.BlockSpec(memory_space=pl.ANY)],
        out_specs=pl.BlockSpec(memory_space=pl.ANY),
        out_shape=jax.ShapeDtypeStruct(xs0.shape, xs0.dtype),
        scratch_shapes=[pltpu.SMEM((4 * tm,), jnp.int32), pltpu.SemaphoreType.DMA((2,)),
                        pltpu.SemaphoreType.DMA(())],
        input_output_aliases={2: 0},
        compiler_params=pltpu.CompilerParams(dimension_semantics=("arbitrary",), vmem_limit_bytes=VMEM_LIMIT,
                                             has_side_effects=True),
        name="moe_dispatch",
    )(dest.reshape(n_rows // tm, 2 * tm), h, xs0)


def _expert_kernel(te_ref, nused_ref, xs_ref, wg_ref, wu_ref, wd_ref, o_ref, acc_ref):
    del te_ref
    live = pl.program_id(0) < nused_ref[0]

    @pl.when(live)
    def _():
        _swiglu_accumulate(xs_ref[...].astype(BF16), wg_ref, wu_ref, wd_ref, acc_ref)
        o_ref[...] = acc_ref[...]

    @pl.when(jnp.logical_not(live))
    def _():
        o_ref[...] = jnp.zeros_like(o_ref)


def _experts(tile_expert, n_used, xs, wg, wu, wd):
    s, d = xs.shape
    tm = EXPERT_TILE
    row = pl.BlockSpec((tm, d), lambda i, te, nu: (i, 0))
    wspec = lambda a: pl.BlockSpec((None,) + a.shape[1:], lambda i, te, nu: (te[i], 0, 0, 0))
    return pl.pallas_call(
        _expert_kernel,
        grid_spec=pltpu.PrefetchScalarGridSpec(
            num_scalar_prefetch=2,
            grid=(s // tm,),
            in_specs=[row, wspec(wg), wspec(wu), wspec(wd)],
            out_specs=row,
            scratch_shapes=[pltpu.VMEM((tm, d), F32)]),
        out_shape=jax.ShapeDtypeStruct((s, d), F32),
        compiler_params=_cparams(1),
        name="moe_experts",
    )(tile_expert, n_used, xs, wg, wu, wd)


def _combine_kernel(dest_hbm, eo_hbm, x_ref, wt_ref, mod_ref, o_ref, dest_smem, buf, sem_idx, sem_rows):
    i = pl.program_id(0)
    n = pl.num_programs(0)
    tm = x_ref.shape[0]
    slot = i % 2

    def gather(s):
        base = s * (2 * tm)

        def issue(r, carry):
            _row_copy(eo_hbm, dest_smem[base + 2 * r], buf.at[s, 0], r, sem_rows.at[s]).start()
            _row_copy(eo_hbm, dest_smem[base + 2 * r + 1], buf.at[s, 1], r, sem_rows.at[s]).start()
            return carry

        lax.fori_loop(0, tm, issue, 0, unroll=DMA_ISSUE_UNROLL)

    @pl.when(i == 0)
    def _():
        first = _idx_copy(dest_hbm, dest_smem, sem_idx, 0, 0)
        first.start()
        first.wait()
        gather(0)

        @pl.when(n > 1)
        def _():
            _idx_copy(dest_hbm, dest_smem, sem_idx, 1, 1).start()

    @pl.when(i + 1 < n)
    def _():
        _idx_copy(dest_hbm, dest_smem, sem_idx, i + 1, 1 - slot).wait()

        @pl.when(i + 2 < n)
        def _():
            _idx_copy(dest_hbm, dest_smem, sem_idx, i + 2, slot).start()

        gather(1 - slot)

    for k in range(2):
        pltpu.make_async_copy(eo_hbm.at[pl.ds(0, tm), :], buf.at[slot, k], sem_rows.at[slot]).wait()
    wt = wt_ref[...]
    y = wt[:, 0:1] * buf[slot, 0] + wt[:, 1:2] * buf[slot, 1]
    o_ref[...] = x_ref[...] + mod_ref[0, 5:6, :] * y


def _combine(dest, eo, xa, wts, mods, n_rows, n_batch, seq_len):
    d = xa.shape[1]
    tm = TOKEN_TILE
    row = lambda w: pl.BlockSpec((tm, w), lambda i: (i, 0))
    return pl.pallas_call(
        _combine_kernel,
        grid=(n_rows // tm,),
        in_specs=[pl.BlockSpec(memory_space=pl.ANY), pl.BlockSpec(memory_space=pl.ANY), row(d), row(LANES),
                  pl.BlockSpec((1, 6, d), _mod_index(seq_len // tm, n_batch))],
        out_specs=row(d),
        out_shape=jax.ShapeDtypeStruct((n_rows, d), F32),
        scratch_shapes=[pltpu.SMEM((4 * tm,), jnp.int32), pltpu.VMEM((2, 2, tm, d), F32),
                        pltpu.SemaphoreType.DMA((2,)), pltpu.SemaphoreType.DMA((2,))],
        compiler_params=_cparams(1),
        name="moe_combine",
    )(dest.reshape(n_rows // tm, 2 * tm), eo, xa, wts, mods)


def _moe(xa, mods, g2, r_pad, wg, wu, wd, n_rows, n_batch, seq_len):
    h, idx, wts = _router(xa, mods, g2, r_pad, n_rows, n_batch, seq_len)
    tm = EXPERT_TILE
    e_flat = idx[:, :2].reshape(-1)
    onehot = (e_flat[:, None] == jnp.arange(N_EXPERTS, dtype=jnp.int32)[None, :]).astype(jnp.int32)
    csum = jnp.cumsum(onehot, axis=0)
    counts = csum[-1]
    rank = jnp.sum(onehot * csum, axis=1) - 1
    padded = ((counts + tm - 1) // tm) * tm
    ends = jnp.cumsum(padded)
    starts = ends - padded
    dest = (jnp.sum(onehot * starts[None, :], axis=1) + rank).astype(jnp.int32)
    n_slots = 2 * n_rows + N_EXPERTS * tm
    tile_start = jnp.arange(n_slots // tm, dtype=jnp.int32) * tm
    tile_expert = jnp.minimum(jnp.sum((tile_start[:, None] >= ends[None, :]).astype(jnp.int32), axis=1),
                              N_EXPERTS - 1).astype(jnp.int32)
    n_used = (ends[-1:] // tm).astype(jnp.int32)
    xs = _dispatch(dest, h, jnp.zeros((n_slots, xa.shape[1]), F32))
    eo = _experts(tile_expert, n_used, xs, wg, wu, wd)
    return _combine(dest, eo, xa, wts, mods, n_rows, n_batch, seq_len)


def _softplus(v):
    return jnp.maximum(v, 0.0) + jnp.log1p(jnp.exp(-jnp.abs(v)))


def _ssd_kernel(xl_ref, dl_ref, xc_ref, dc_ref, cw_ref, cb_ref, dtb_ref, alog_ref, dsk_ref, *rest, want_ctx):
    if want_ctx:
        yl_ref, yc_ref, xs_l, dt_l, xs_c, dt_c, st_ref = rest
    else:
        yl_ref, xs_l, dt_l, xs_c, dt_c, st_ref = rest
        yc_ref = None
    ck = SSD_CHUNK
    hp = SSD_HEAD_DIM
    ns = SSD_STATE
    row = lax.broadcasted_iota(jnp.int32, (ck, 1), 0)
    li = lax.broadcasted_iota(jnp.int32, (ck, ck), 0)
    si = lax.broadcasted_iota(jnp.int32, (ck, ck), 1)
    masks = (si <= li, si >= li)
    tris = (masks[0].astype(F32), masks[1].astype(F32))
    a_row = -jnp.exp(alog_ref[...])
    dskip = dsk_ref[...]
    nt = (((1,), (1,)), ((), ()))

    def prep(raw_ref, dtraw_ref, xs_s, dt_s, y_ref):
        n = raw_ref.shape[0]
        nk = n // ck

        def body(k, carry):
            r0 = pl.multiple_of(k * ck, ck)
            a = raw_ref[pl.ds(r0, ck), :]
            top = raw_ref[pl.ds(pl.multiple_of(jnp.maximum(r0 - 8, 0), 8), 8), :][7:8, :]
            bot = raw_ref[pl.ds(pl.multiple_of(jnp.minimum(r0 + ck, n - 8), 8), 8), :][0:1, :]
            top = jnp.where(k > 0, top, 0.0)
            bot = jnp.where(k < nk - 1, bot, 0.0)
            prev = jnp.where(row == 0, top, pltpu.roll(a, 1, 0))
            nxt = jnp.where(row == ck - 1, bot, pltpu.roll(a, ck - 1, 0))
            xs = _silu(prev * cw_ref[0:1, :] + a * cw_ref[1:2, :] + nxt * cw_ref[2:3, :] + cb_ref[...])
            xs_s[pl.ds(r0, ck), :] = xs
            dt_s[pl.ds(r0, ck), :] = _softplus(dtraw_ref[pl.ds(r0, ck), :] + dtb_ref[...])
            if y_ref is not None:
                y_ref[pl.ds(r0, ck), :] = xs[:, 0:SSD_WIDTH] * dskip
            return carry

        lax.fori_loop(0, nk, body, 0)

    def run(xs_s, dt_s, y_ref):
        nk = xs_s.shape[0] // ck

        def one(kk, dr):
            r0 = pl.multiple_of(kk * ck, ck)
            xc = xs_s[pl.ds(r0, ck), :]
            dtc = dt_s[pl.ds(r0, ck), :]
            la = jnp.dot(tris[dr], dtc * a_row, precision=lax.Precision.HIGHEST, preferred_element_type=F32)
            la_t = la.T
            la_end = la[ck - 1:ck, :] if dr == 0 else la[0:1, :]
            ys = []
            for g in range(SSD_GROUPS):
                b_g = xc[:, SSD_WIDTH + g * ns:SSD_WIDTH + (g + 1) * ns].astype(BF16)
                c0 = SSD_WIDTH + SSD_GROUPS * ns + g * ns
                c_g = xc[:, c0:c0 + ns].astype(BF16)
                if y_ref is not None:
                    scores = lax.dot_general(c_g, b_g, nt, preferred_element_type=F32)
                for hh in range(SSD_HEADS // SSD_GROUPS):
                    h = g * (SSD_HEADS // SSD_GROUPS) + hh
                    col = dr * SSD_HEADS + h
                    xdt = xc[:, h * hp:(h + 1) * hp] * dtc[:, col:col + 1]
                    la_col = la[:, col:col + 1]
                    st = st_ref[col]
                    if y_ref is not None:
                        decay = jnp.exp(jnp.where(masks[dr], la_col - la_t[col:col + 1, :], -jnp.inf))
                        y = jnp.dot((scores * decay).astype(BF16), xdt.astype(BF16), preferred_element_type=F32)
                        y = y + lax.dot_general(c_g, st.astype(BF16), nt,
                                                preferred_element_type=F32) * jnp.exp(la_col)
                        ys.append(y)
                    le = la_end[:, col:col + 1]
                    xw_t = (xdt * jnp.exp(le - la_col)).T.astype(BF16)
                    st_ref[col] = st * jnp.exp(le) + jnp.dot(xw_t, b_g, preferred_element_type=F32)
            if y_ref is not None:
                y_ref[pl.ds(r0, ck), :] += jnp.concatenate(ys, axis=1)

        def body(k, carry):
            one(k, 0)
            one(nk - 1 - k, 1)
            return carry

        lax.fori_loop(0, nk, body, 0)

    prep(xl_ref, dl_ref, xs_l, dt_l, yl_ref)
    prep(xc_ref, dc_ref, xs_c, dt_c, yc_ref)
    st_ref[...] = jnp.zeros_like(st_ref)
    run(xs_c, dt_c, yc_ref)
    run(xs_l, dt_l, yl_ref)


def _ssd(xbc, dtp, conv_w, conv_b, dt_bias, a_log, d_skip, n_batch, seq_len, ctx_len, want_ctx):
    ctx0 = n_batch * seq_len // ctx_len
    pad = lambda v: jnp.pad(v.reshape(1, -1), ((0, 0), (0, LANES - v.size)))
    lat = lambda w: pl.BlockSpec((seq_len, w), lambda b: (b, 0))
    ctx = lambda w: pl.BlockSpec((ctx_len, w), lambda b: (ctx0 + b, 0))
    const = lambda a: pl.BlockSpec(a.shape, lambda b: (0,) * a.ndim)
    consts = (conv_w, conv_b.reshape(1, -1), pad(dt_bias), pad(a_log),
              jnp.repeat(d_skip, SSD_HEAD_DIM).reshape(1, -1))
    out_specs = [lat(SSD_WIDTH)]
    out_shape = [jax.ShapeDtypeStruct((n_batch * seq_len, SSD_WIDTH), F32)]
    if want_ctx:
        out_specs.append(pl.BlockSpec((ctx_len, SSD_WIDTH), lambda b: (b, 0)))
        out_shape.append(jax.ShapeDtypeStruct((n_batch * ctx_len, SSD_WIDTH), F32))
    return pl.pallas_call(
        functools.partial(_ssd_kernel, want_ctx=want_ctx),
        grid=(n_batch,),
        in_specs=[lat(SSD_XBC_COLS), lat(LANES), ctx(SSD_XBC_COLS), ctx(LANES)] + [const(a) for a in consts],
        out_specs=out_specs,
        out_shape=out_shape,
        scratch_shapes=[pltpu.VMEM((seq_len, SSD_XBC_COLS), F32), pltpu.VMEM((seq_len, LANES), F32),
                        pltpu.VMEM((ctx_len, SSD_XBC_COLS), F32), pltpu.VMEM((ctx_len, LANES), F32),
                        pltpu.VMEM((2 * SSD_HEADS, SSD_HEAD_DIM, SSD_STATE), F32)],
        compiler_params=_cparams(1),
        name="ssd",
    )(xbc, dtp, xbc, dtp, *consts)


HY_BLOCK = 256
HY_CH_STEP = 8


def _hyena_tables(seq_len):
    nj = 2 * seq_len
    lag = jnp.arange(nj, dtype=jnp.int32) - seq_len
    dist = jnp.abs(lag)
    pos = jnp.minimum(dist, seq_len - 1)
    t = jnp.linspace(0.0, 1.0, seq_len, dtype=F32)[pos]
    w = ((2.0 * math.pi / seq_len) * jnp.arange(seq_len, dtype=F32))[pos]
    bands = (HYENA_POS_DIM - 1) // 2
    freqs = jnp.linspace(1e-4, bands - 1, bands, dtype=F32)[None, :]
    z = jnp.concatenate([t[:, None], jnp.cos(freqs * w[:, None]), -jnp.sin(freqs * w[:, None])], axis=-1)
    zt = jnp.pad(z.T, ((0, (-HYENA_POS_DIM) % 8), (0, 0)))
    deltas = jnp.abs(jnp.linspace(math.log(HYENA_DECAY_TARGET) / HYENA_SLOW_DECAY,
                                  math.log(HYENA_DECAY_TARGET) / HYENA_FAST_DECAY, HYENA_WIDTH, dtype=F32))
    dec = jnp.exp(-t[None, :] * deltas[:, None]) * (dist < seq_len).astype(F32)[None, :]
    fwd = (lag >= 0).astype(F32)[None, :]
    return zt, dec, fwd


def _hyfilt_kernel(zt_ref, dec_ref, fwd_ref, w1_ref, b1_ref, f1_ref, w2_ref, b2_ref, f2_ref, w3_ref, o_ref):
    hi = lax.Precision.HIGHEST
    h = jnp.sin(f1_ref[...] * (jnp.dot(w1_ref[...], zt_ref[...], precision=hi, preferred_element_type=F32)
                               + b1_ref[...]))
    h = jnp.sin(f2_ref[...] * (jnp.dot(w2_ref[...], h, precision=hi, preferred_element_type=F32) + b2_ref[...]))
    hw = jnp.dot(w3_ref[...], h, precision=hi, preferred_element_type=F32)
    fwd = fwd_ref[...] > 0.5
    dec = dec_ref[...]
    nw = HYENA_WIDTH
    for o in range(HYENA_ORDER):
        o_ref[o] = jnp.where(fwd, hw[o * nw:(o + 1) * nw], hw[(HYENA_ORDER + o) * nw:(HYENA_ORDER + o + 1) * nw]) * dec


def _hyena_filters(tables, w1, b1, f1, w2, b2, f2, w3):
    zt, dec, fwd = tables
    nj = zt.shape[1]
    tj = 512
    col = lambda v: v.reshape(-1, 1)
    w1t = jnp.pad(w1.T, ((0, 0), (0, zt.shape[0] - w1.shape[0])))
    consts = (w1t, col(b1), col(f1), w2.T, col(b2), col(f2), w3.T)
    lanes = lambda a: pl.BlockSpec((a.shape[0], tj), lambda j: (0, j))
    const = lambda a: pl.BlockSpec(a.shape, lambda j: (0, 0))
    return pl.pallas_call(
        _hyfilt_kernel,
        grid=(nj // tj,),
        in_specs=[lanes(zt), lanes(dec), lanes(fwd)] + [const(a) for a in consts],
        out_specs=pl.BlockSpec((HYENA_ORDER, HYENA_WIDTH, tj), lambda j: (0, 0, j)),
        out_shape=jax.ShapeDtypeStruct((HYENA_ORDER, HYENA_WIDTH, nj), F32),
        compiler_params=_cparams(1),
        name="hyena_filters",
    )(zt, dec, fwd, *consts)


def _hyconv_kernel(cw_ref, cb_ref, hb_ref, v_ref, x1_ref, x2_ref, kf_ref, o_ref):
    n_ch, n_b, seq_len = v_ref.shape
    nb = seq_len // HY_BLOCK
    blk = HY_BLOCK
    c_base = pl.program_id(0) * n_ch
    lane = lax.broadcasted_iota(jnp.int32, (1, seq_len), 1)

    def sconv(x, ch):
        prev = jnp.where(lane == 0, 0.0, pltpu.roll(x, 1, 1))
        nxt = jnp.where(lane == seq_len - 1, 0.0, pltpu.roll(x, seq_len - 1, 1))
        return prev * cw_ref[0, ch] + x * cw_ref[1, ch] + nxt * cw_ref[2, ch] + cb_ref[ch]

    def long_conv(vals, kf_row):
        skew = pltpu.roll(jnp.broadcast_to(kf_row, (blk, 2 * seq_len)), 0, 1, stride=1, stride_axis=0)
        vb = vals.astype(BF16)
        acc = [None] * nb
        for d in range(-(nb - 1), nb):
            tt = skew[:, seq_len + d * blk:seq_len + (d + 1) * blk].astype(BF16)
            sis = list(range(max(0, -d), min(nb, nb - d)))
            lhs = [vb[:, s * blk:(s + 1) * blk] for s in sis]
            lhs = lhs[0] if len(lhs) == 1 else jnp.concatenate(lhs, axis=0)
            out = jnp.dot(lhs, tt, preferred_element_type=F32)
            for idx, s in enumerate(sis):
                piece = out[idx * n_b:(idx + 1) * n_b]
                acc[s + d] = piece if acc[s + d] is None else acc[s + d] + piece
        return acc[0] if nb == 1 else jnp.concatenate(acc, axis=1)

    def channel(cc, carry):
        ch = c_base + cc
        v = sconv(v_ref[cc], ch)
        x1 = sconv(x1_ref[cc], HYENA_WIDTH + ch)
        x2 = sconv(x2_ref[cc], 2 * HYENA_WIDTH + ch)
        z = x1 * (long_conv(v, kf_ref[0, cc]) + v * hb_ref[0, ch])
        o_ref[cc] = x2 * (long_conv(z, kf_ref[1, cc]) + z * hb_ref[1, ch])
        return carry

    lax.fori_loop(0, n_ch, channel, 0)


def _hyena_conv(hyt, kf, conv_w, conv_b, hy_bias, n_batch, seq_len, row0):
    t = hyt.shape[1]
    cs = HY_CH_STEP
    nw = HYENA_WIDTH
    view = hyt.reshape(3 * nw, t // seq_len, seq_len)
    blk0 = row0 // n_batch
    stream = lambda k: pl.BlockSpec((cs, n_batch, seq_len), lambda c: (k * (nw // cs) + c, blk0, 0))
    smem = pl.BlockSpec(memory_space=pltpu.SMEM)
    return pl.pallas_call(
        _hyconv_kernel,
        grid=(nw // cs,),
        in_specs=[smem, smem, smem, stream(0), stream(1), stream(2),
                  pl.BlockSpec((HYENA_ORDER, cs, 1, 2 * seq_len), lambda c: (0, c, 0, 0))],
        out_specs=pl.BlockSpec((cs, n_batch, seq_len), lambda c: (c, 0, 0)),
        out_shape=jax.ShapeDtypeStruct((nw, n_batch, seq_len), F32),
        compiler_params=_cparams(1),
        name="hyena_conv",
    )(conv_w, conv_b, hy_bias, view, view, view, kf.reshape(HYENA_ORDER, nw, 1, 2 * seq_len))


def _rope_tables(seq_len, extra):
    rows = seq_len // GRID_W
    row = jnp.repeat(jnp.arange(rows, dtype=F32), GRID_W)
    col = jnp.tile(jnp.arange(GRID_W, dtype=F32), rows)
    inv = ROPE_THETA ** (-jnp.arange(0, ROPE_AXIS_DIM, 2, dtype=F32) / ROPE_AXIS_DIM)
    ang = jnp.stack([row[:, None] * inv, col[:, None] * inv], axis=1)
    cos = jnp.cos(ang)
    sin = jnp.sin(ang)
    cos_h = jnp.concatenate([cos, cos], axis=-1).reshape(seq_len, HEAD_DIM)
    sin_h = jnp.concatenate([-sin, sin], axis=-1).reshape(seq_len, HEAD_DIM)
    cos_t = jnp.concatenate([jnp.tile(cos_h, (1, LANES // HEAD_DIM)), jnp.ones((extra, LANES), F32)], axis=0)
    sin_t = jnp.concatenate([jnp.tile(sin_h, (1, LANES // HEAD_DIM)), jnp.zeros((extra, LANES), F32)], axis=0)
    return cos_t, sin_t


def _block_diag_ones(n, seg):
    i = jnp.arange(n) // seg
    return (i[:, None] == i[None, :]).astype(BF16)


def _chunked_in(w):
    *lead, d, f = w.shape
    w = w.reshape(*lead, d, f // FFN_CHUNK, FFN_CHUNK)
    return jnp.moveaxis(w, -2, -3).astype(BF16)


def _chunked_out(w):
    *lead, f, d = w.shape
    return w.reshape(*lead, f // FFN_CHUNK, FFN_CHUNK, d).astype(BF16)


def kernel(x, c, ctx, c_ctx, w_ada, b_ada, norm1, norm2, w_in, w_out, q_norm, k_norm, att_sinks, att_out_norm, ssd_conv_w, ssd_conv_b, ssd_dt_bias, ssd_a_log, ssd_d, ssd_norm, hy_conv_w, hy_conv_b, hy_w1, hy_b1, hy_f1, hy_w2, hy_b2, hy_f2, hy_w3, hy_bias, hy_out_norm, ffn_w_gate, ffn_w_up, ffn_w_down, moe_router, moe_w_gate, moe_w_up, moe_w_down):
    n_batch, seq_len, d = x.shape
    ctx_len = ctx.shape[1]
    n_lat = n_batch * seq_len
    n_ctx = n_batch * ctx_len
    depth = w_in.shape[0]
    xa = jnp.concatenate([x.reshape(n_lat, d), ctx.reshape(n_ctx, d)], axis=0)

    cc = jnp.concatenate([c, c_ctx[None, :]], axis=0)
    pad_rows = (-cc.shape[0]) % 8
    cc = jnp.pad(cc, ((0, pad_rows), (0, 0)))
    mods_all = _adaln(cc, w_ada, b_ada)[:, :n_batch + 1].reshape(depth, n_batch + 1, 6, d)

    cos_t, sin_t = _rope_tables(seq_len, TOKEN_TILE)
    bd_q = _block_diag_ones(Q_COLS, HEAD_DIM)
    bd_h = _block_diag_ones(HYENA_WIDTH, HYENA_WIDTH // HYENA_GROUPS)
    hy_tab_l = _hyena_tables(seq_len)
    hy_tab_c = _hyena_tables(ctx_len)

    for i in range(depth):
        last = i == depth - 1
        j = i // 2
        mods = mods_all[i]
        wi = w_in[i]
        c_dt = QKV_W + ZX_W
        w_cat = jnp.concatenate([wi[:, :c_dt], wi[:, c_dt + SSD_DT_COLS:], wi[:, c_dt:c_dt + SSD_DT_COLS],
                                 jnp.zeros((d, LANES - SSD_DT_COLS), F32)], axis=1).astype(BF16)
        qg = jnp.tile(q_norm[i], Q_COLS // HEAD_DIM)[None, :]
        kg = jnp.tile(k_norm[i], KV_COLS // HEAD_DIM)[None, :]
        q, k, v, z, xbc, hyt, dtp = _inproj(xa, mods, norm1[i][None, :], w_cat, cos_t, sin_t, qg, kg, bd_q,
                                            n_batch, seq_len)

        att_l = _attention(att_sinks[i], q, k, v, n_batch, seq_len, ctx_len, True)
        ssd_out = _ssd(xbc, dtp, ssd_conv_w[i], ssd_conv_b[i], ssd_dt_bias[i], ssd_a_log[i], ssd_d[i],
                       n_batch, seq_len, ctx_len, not last)
        filt = (hy_w1[i], hy_b1[i], hy_f1[i], hy_w2[i], hy_b2[i], hy_f2[i], hy_w3[i])
        hyo_l = _hyena_conv(hyt, _hyena_filters(hy_tab_l, *filt), hy_conv_w[i], hy_conv_b[i], hy_bias[i],
                            n_batch, seq_len, 0).reshape(HYENA_WIDTH, n_lat)
        if last:
            att = (att_l, None)
            sy = (ssd_out[0], None)
            hyo = (hyo_l, None)
            n_rows = n_lat
        else:
            att = (att_l, _attention(att_sinks[i], q, k, v, n_batch, seq_len, ctx_len, False))
            sy = tuple(ssd_out)
            hyo_c = _hyena_conv(hyt, _hyena_filters(hy_tab_c, *filt), hy_conv_w[i], hy_conv_b[i], hy_bias[i],
                                n_batch, ctx_len, n_lat // ctx_len).reshape(HYENA_WIDTH, n_ctx)
            hyo = (hyo_l, hyo_c)
            n_rows = n_lat + n_ctx
        xa = _merge(att, sy, hyo, z, xa, mods, att_out_norm[i][None, :], ssd_norm[i][None, :],
                    hy_out_norm[i][None, :], bd_h, w_out[i].astype(BF16), n_batch, seq_len)

        g2 = norm2[i][None, :]
        if i % 2 == 0:
            xa = _ffn(xa, mods, g2, _chunked_in(ffn_w_gate[j]), _chunked_in(ffn_w_up[j]),
                      _chunked_out(ffn_w_down[j]), n_batch, seq_len)
        else:
            r_pad = jnp.pad(moe_router[j], ((0, 0), (0, LANES - N_EXPERTS)))
            xa = _moe(xa, mods, g2, r_pad, _chunked_in(moe_w_gate[j]), _chunked_in(moe_w_up[j]),
                      _chunked_out(moe_w_down[j]), n_rows, n_batch, seq_len)
    return xa[:n_lat].reshape(n_batch, seq_len, d)
```

```python
import functools
import math

import jax
import jax.numpy as jnp
from jax import lax
from jax.experimental import pallas as pl
from jax.experimental.pallas import tpu as pltpu

F32 = jnp.float32
BF16 = jnp.bfloat16

D_MODEL = 1024
DEPTH = 4
GRID_W = 64
EPS = 1e-6
HEAD_DIM = 64
ATT_WIDTH = 512
ATT_HEADS = 8
ATT_KV_HEADS = 2
ATT_GROUP = 4
WINDOW = 128
ATT_BLOCK = 128
ROPE_THETA = 10000.0
ROPE_AXIS_DIM = 32
SSD_WIDTH = 256
SSD_HEAD_DIM = 64
SSD_HEADS = 4
SSD_STATE = 64
SSD_GROUPS = 2
SSD_CHUNK = 128
HYENA_WIDTH = 256
HYENA_GROUPS = 4
HYENA_ORDER = 2
HYENA_POS_DIM = 33
HYENA_FAST_DECAY = 0.3
HYENA_SLOW_DECAY = 1.5
HYENA_DECAY_TARGET = 1e-2
Q_COLS = 512
KV_COLS = 128
SSD_XBC_COLS = 512
SSD_DT_COLS = 8
HY_COLS = 768
FFN_DIM = 2816
N_EXPERTS = 8
FFN_CHUNK = 256
N_FFN_CHUNKS = FFN_DIM // FFN_CHUNK
LANES = 128
QKV_W = Q_COLS + 2 * KV_COLS
ZX_W = SSD_WIDTH + SSD_XBC_COLS
PROJ_PAD = QKV_W + ZX_W + HY_COLS + LANES
VMEM_LIMIT = 56 * 1024 * 1024
TOKEN_TILE = 512
EXPERT_TILE = 512


def _cparams(n_axes):
    return pltpu.CompilerParams(dimension_semantics=("arbitrary",) * n_axes,
                                vmem_limit_bytes=VMEM_LIMIT)


def _silu(v):
    return v / (1.0 + jnp.exp(-v))


def _modnorm(x, g, scale, shift):
    ms = jnp.mean(x * x, axis=-1, keepdims=True)
    return x * lax.rsqrt(ms + EPS) * g * (1.0 + scale) + shift


def _segsum(t, bd):
    hi = t.astype(BF16)
    lo = (t - hi.astype(F32)).astype(BF16)
    return (jnp.dot(hi, bd, preferred_element_type=F32)
            + jnp.dot(lo, bd, preferred_element_type=F32))


def _mod_index(tiles_per_batch, n_batch):
    return lambda i: (jnp.minimum(i // tiles_per_batch, n_batch), 0, 0)


def _adaln_kernel(c_ref, w_ref, b_ref, o_ref):
    s = _silu(c_ref[...]).astype(BF16)
    o_ref[...] = jnp.dot(s, w_ref[...].astype(BF16), preferred_element_type=F32) + b_ref[...]


def _adaln(cc, w_ada, b_ada):
    depth, d, n = w_ada.shape
    r = cc.shape[0]
    tn = 512
    return pl.pallas_call(
        _adaln_kernel,
        grid=(depth, n // tn),
        in_specs=[pl.BlockSpec((r, d), lambda l, j: (0, 0)),
                  pl.BlockSpec((None, d, tn), lambda l, j: (l, 0, j)),
                  pl.BlockSpec((None, 1, tn), lambda l, j: (l, 0, j))],
        out_specs=pl.BlockSpec((None, r, tn), lambda l, j: (l, 0, j)),
        out_shape=jax.ShapeDtypeStruct((depth, r, n), F32),
        compiler_params=_cparams(2),
        name="adaln",
    )(cc, w_ada, b_ada.reshape(depth, 1, n))


def _inproj_kernel(x_ref, mod_ref, g1_ref, w_ref, cos_ref, sin_ref, qg_ref, kg_ref, bd_ref,
                   q_ref, k_ref, v_ref, z_ref, xbc_ref, hyt_ref, dt_ref):
    x = x_ref[...]
    h = _modnorm(x, g1_ref[...], mod_ref[0, 1:2, :], mod_ref[0, 0:1, :]).astype(BF16)
    pq = jnp.dot(h, w_ref[:, 0:QKV_W], preferred_element_type=F32)
    cos = cos_ref[...]
    sin = sin_ref[...]
    lane = lax.broadcasted_iota(jnp.int32, (1, LANES), 1)
    first_half = (lane % 32) < 16

    def rope(t):
        partner = jnp.where(first_half, pltpu.roll(t, LANES - 16, 1), pltpu.roll(t, 16, 1))
        return t * cos + partner * sin

    q = pq[:, 0:Q_COLS]
    qn = q * lax.rsqrt(_segsum(q * q, bd_ref[...]) * (1.0 / HEAD_DIM) + EPS) * qg_ref[...]
    scale = HEAD_DIM ** -0.5
    for j in range(Q_COLS // LANES):
        pair = (rope(qn[:, LANES * j:LANES * (j + 1)]) * scale).astype(BF16)
        q_ref[2 * j] = pair[:, 0:HEAD_DIM]
        q_ref[2 * j + 1] = pair[:, HEAD_DIM:LANES]
    k = pq[:, Q_COLS:Q_COLS + KV_COLS]
    kn = k * lax.rsqrt(_segsum(k * k, bd_ref[0:KV_COLS, 0:KV_COLS]) * (1.0 / HEAD_DIM) + EPS) * kg_ref[...]
    kt = rope(kn).T.astype(BF16)
    vv = pq[:, Q_COLS + KV_COLS:QKV_W].astype(BF16)
    for j in range(ATT_KV_HEADS):
        k_ref[j] = kt[j * HEAD_DIM:(j + 1) * HEAD_DIM, :]
        v_ref[j] = vv[:, j * HEAD_DIM:(j + 1) * HEAD_DIM]
    zx = jnp.dot(h, w_ref[:, QKV_W:QKV_W + ZX_W], preferred_element_type=F32)
    z_ref[...] = zx[:, 0:SSD_WIDTH]
    xbc_ref[...] = zx[:, SSD_WIDTH:ZX_W]
    hyt_ref[...] = jnp.dot(h, w_ref[:, QKV_W + ZX_W:QKV_W + ZX_W + HY_COLS], preferred_element_type=F32).T
    dt_ref[...] = jnp.dot(h, w_ref[:, QKV_W + ZX_W + HY_COLS:PROJ_PAD], preferred_element_type=F32)


def _inproj(xa, mods, g1, w_cat, cos_t, sin_t, qg, kg, bd, n_batch, seq_len):
    t, d = xa.shape
    tm = TOKEN_TILE
    tpb = seq_len // tm
    n_lat = n_batch * tpb
    rope_idx = lambda i: (jnp.where(i < n_lat, i % tpb, tpb), 0)
    row = lambda w: pl.BlockSpec((tm, w), lambda i: (i, 0))
    heads = lambda nh: pl.BlockSpec((nh, tm, HEAD_DIM), lambda i: (0, i, 0))
    const = lambda a: pl.BlockSpec(a.shape, lambda i: (0,) * a.ndim)
    return pl.pallas_call(
        _inproj_kernel,
        grid=(t // tm,),
        in_specs=[row(d),
                  pl.BlockSpec((1, 6, d), _mod_index(tpb, n_batch)),
                  const(g1), const(w_cat),
                  pl.BlockSpec((tm, LANES), rope_idx), pl.BlockSpec((tm, LANES), rope_idx),
                  const(qg), const(kg), const(bd)],
        out_specs=[heads(ATT_HEADS), pl.BlockSpec((ATT_KV_HEADS, HEAD_DIM, tm), lambda i: (0, 0, i)),
                   heads(ATT_KV_HEADS), row(SSD_WIDTH), row(SSD_XBC_COLS),
                   pl.BlockSpec((HY_COLS, tm), lambda i: (0, i)), row(LANES)],
        out_shape=[jax.ShapeDtypeStruct((ATT_HEADS, t, HEAD_DIM), BF16),
                   jax.ShapeDtypeStruct((ATT_KV_HEADS, HEAD_DIM, t), BF16),
                   jax.ShapeDtypeStruct((ATT_KV_HEADS, t, HEAD_DIM), BF16),
                   jax.ShapeDtypeStruct((t, SSD_WIDTH), F32),
                   jax.ShapeDtypeStruct((t, SSD_XBC_COLS), F32),
                   jax.ShapeDtypeStruct((HY_COLS, t), F32),
                   jax.ShapeDtypeStruct((t, LANES), F32)],
        compiler_params=_cparams(1),
        name="inproj",
    )(xa, mods, g1, w_cat, cos_t, sin_t, qg, kg, bd)


def _attn_kernel(sink_ref, q_ref, *refs, n_q, band):
    if band:
        k_ref, v_ref, kc_ref, vc_ref, o_ref, bias_ref = refs
        seq_len = v_ref.shape[1]
        assert n_q >= 3
    else:
        kc_ref, vc_ref, o_ref = refs
    qb = ATT_BLOCK
    rows = ATT_GROUP * qb
    row_id = lax.broadcasted_iota(jnp.int32, (rows, 1), 0)
    nt = (((1,), (1,)), ((), ()))

    if band:
        @pl.when(pl.program_id(0) == 0)
        def _():
            rel0 = (lax.broadcasted_iota(jnp.int32, (rows, band), 1)
                    - lax.broadcasted_iota(jnp.int32, (rows, band), 0) % qb)
            for var in range(3):
                bias_ref[var] = jnp.where(jnp.abs(rel0 - var * WINDOW) <= WINDOW, 0.0, -jnp.inf)

    for j in range(ATT_KV_HEADS):
        kc = kc_ref[j]
        vc = vc_ref[j]
        snk = jnp.zeros((rows, 1), F32)
        for g in range(ATT_GROUP):
            snk = jnp.where(row_id // qb == g, sink_ref[ATT_GROUP * j + g], snk)

        def block(i, carry, j=j, kc=kc, vc=vc, snk=snk):
            q0 = pl.multiple_of(i * qb, qb)
            qh = jnp.concatenate([q_ref[ATT_GROUP * j + g, pl.ds(q0, qb), :] for g in range(ATT_GROUP)],
                                 axis=0)
            s_ctx = jnp.dot(qh, kc, preferred_element_type=F32)
            m = jnp.maximum(jnp.max(s_ctx, axis=-1, keepdims=True), snk)
            if band:
                k0 = pl.multiple_of(jnp.clip(q0 - WINDOW, 0, seq_len - band), qb)
                var = jnp.where(i == 0, 0, jnp.where(i == n_q - 1, 2, 1))
                s_loc = jnp.dot(qh, k_ref[j, :, pl.ds(k0, band)], preferred_element_type=F32) + bias_ref[var]
                m = jnp.maximum(m, jnp.max(s_loc, axis=-1, keepdims=True))
            p_ctx = jnp.exp(s_ctx - m)
            den = jnp.sum(p_ctx, axis=-1, keepdims=True) + jnp.exp(snk - m)
            o = jnp.dot(p_ctx.astype(BF16), vc, preferred_element_type=F32)
            if band:
                p_loc = jnp.exp(s_loc - m)
                den = den + jnp.sum(p_loc, axis=-1, keepdims=True)
                o = o + jnp.dot(p_loc.astype(BF16), v_ref[j, pl.ds(k0, band), :], preferred_element_type=F32)
            o = o / den
            for g in range(ATT_GROUP):
                c0 = (ATT_GROUP * j + g) * HEAD_DIM
                o_ref[pl.ds(q0, qb), c0:c0 + HEAD_DIM] = o[g * qb:(g + 1) * qb, :]
            return carry

        lax.fori_loop(0, n_q, block, 0, unroll=2)


def _attention(sinks, q, k, v, n_batch, seq_len, ctx_len, latent):
    ctx_blk0 = n_batch * seq_len // ctx_len
    kc_spec = pl.BlockSpec((ATT_KV_HEADS, HEAD_DIM, ctx_len), lambda b: (0, 0, ctx_blk0 + b))
    vc_spec = pl.BlockSpec((ATT_KV_HEADS, ctx_len, HEAD_DIM), lambda b: (0, ctx_blk0 + b, 0))
    smem = pl.BlockSpec(memory_space=pltpu.SMEM)
    scratch = []
    if latent:
        rows = seq_len
        band = ATT_BLOCK + 2 * WINDOW
        in_specs = [smem, pl.BlockSpec((ATT_HEADS, rows, HEAD_DIM), lambda b: (0, b, 0)),
                    pl.BlockSpec((ATT_KV_HEADS, HEAD_DIM, rows), lambda b: (0, 0, b)),
                    pl.BlockSpec((ATT_KV_HEADS, rows, HEAD_DIM), lambda b: (0, b, 0)), kc_spec, vc_spec]
        args = (sinks, q, k, v, k, v)
        scratch = [pltpu.VMEM((3, ATT_GROUP * ATT_BLOCK, band), F32)]
    else:
        rows = ctx_len
        band = 0
        in_specs = [smem, pl.BlockSpec((ATT_HEADS, rows, HEAD_DIM), lambda b: (0, ctx_blk0 + b, 0)),
                    kc_spec, vc_spec]
        args = (sinks, q, k, v)
    return pl.pallas_call(
        functools.partial(_attn_kernel, n_q=rows // ATT_BLOCK, band=band),
        grid=(n_batch,),
        in_specs=in_specs,
        out_specs=pl.BlockSpec((rows, ATT_WIDTH), lambda b: (b, 0)),
        out_shape=jax.ShapeDtypeStruct((n_batch * rows, ATT_WIDTH), F32),
        scratch_shapes=scratch,
        compiler_params=_cparams(1),
        name="attn_latent" if latent else "attn_ctx",
    )(*args)


def _merge_kernel(*refs, n_lat_tiles, with_ctx):
    if with_ctx:
        (attl_ref, attc_ref, syl_ref, syc_ref, hyl_ref, hyc_ref, z_ref, x_ref, mod_ref,
         ga_ref, gs_ref, gh_ref, bd_ref, w_ref, o_ref) = refs
        is_lat = pl.program_id(0) < n_lat_tiles
        att = jnp.where(is_lat, attl_ref[...], attc_ref[...])
        sy = jnp.where(is_lat, syl_ref[...], syc_ref[...])
        hy_t = jnp.where(is_lat, hyl_ref[...], hyc_ref[...])
    else:
        attl_ref, syl_ref, hyl_ref, z_ref, x_ref, mod_ref, ga_ref, gs_ref, gh_ref, bd_ref, w_ref, o_ref = refs
        att = attl_ref[...]
        sy = syl_ref[...]
        hy_t = hyl_ref[...]
    a = att * lax.rsqrt(jnp.mean(att * att, axis=-1, keepdims=True) + EPS) * ga_ref[...]
    s = sy * _silu(z_ref[...])
    s = s * lax.rsqrt(jnp.mean(s * s, axis=-1, keepdims=True) + EPS) * gs_ref[...]
    hy = hy_t.T
    hn = hy * lax.rsqrt(_segsum(hy * hy, bd_ref[...]) * (1.0 / (HYENA_WIDTH // HYENA_GROUPS)) + EPS) * gh_ref[...]
    y = jnp.dot(a.astype(BF16), w_ref[0:ATT_WIDTH, :], preferred_element_type=F32)
    y = y + jnp.dot(s.astype(BF16), w_ref[ATT_WIDTH:ATT_WIDTH + SSD_WIDTH, :], preferred_element_type=F32)
    y = y + jnp.dot(hn.astype(BF16), w_ref[ATT_WIDTH + SSD_WIDTH:, :], preferred_element_type=F32)
    o_ref[...] = x_ref[...] + mod_ref[0, 2:3, :] * y


def _merge(att, sy, hy, z, xa, mods, ga, gs, gh, bd, w_out, n_batch, seq_len):
    d = xa.shape[1]
    tm = TOKEN_TILE
    nl = att[0].shape[0] // tm
    with_ctx = att[1] is not None
    t = att[0].shape[0] + (att[1].shape[0] if with_ctx else 0)
    row = lambda w: pl.BlockSpec((tm, w), lambda i: (i, 0))
    lat_row = lambda w: pl.BlockSpec((tm, w), lambda i: (jnp.minimum(i, nl - 1), 0))
    ctx_row = lambda w: pl.BlockSpec((tm, w), lambda i: (jnp.maximum(i - nl, 0), 0))
    const = lambda a: pl.BlockSpec(a.shape, lambda i: (0,) * a.ndim)
    if with_ctx:
        streams = [att[0], att[1], sy[0], sy[1], hy[0], hy[1]]
        specs = [lat_row(ATT_WIDTH), ctx_row(ATT_WIDTH), lat_row(SSD_WIDTH), ctx_row(SSD_WIDTH),
                 pl.BlockSpec((HYENA_WIDTH, tm), lambda i: (0, jnp.minimum(i, nl - 1))),
                 pl.BlockSpec((HYENA_WIDTH, tm), lambda i: (0, jnp.maximum(i - nl, 0)))]
    else:
        streams = [att[0], sy[0], hy[0]]
        specs = [row(ATT_WIDTH), row(SSD_WIDTH), pl.BlockSpec((HYENA_WIDTH, tm), lambda i: (0, i))]
    return pl.pallas_call(
        functools.partial(_merge_kernel, n_lat_tiles=nl, with_ctx=with_ctx),
        grid=(t // tm,),
        in_specs=specs + [row(SSD_WIDTH), row(d),
                          pl.BlockSpec((1, 6, d), _mod_index(seq_len // tm, n_batch)),
                          const(ga), const(gs), const(gh), const(bd), const(w_out)],
        out_specs=row(d),
        out_shape=jax.ShapeDtypeStruct((t, d), F32),
        compiler_params=_cparams(1),
        name="merge_outproj",
    )(*streams, z, xa, mods, ga, gs, gh, bd, w_out)


def _swiglu_accumulate(h, wg_ref, wu_ref, wd_ref, acc_ref):
    for c in range(N_FFN_CHUNKS):
        cols = slice(c * FFN_CHUNK, (c + 1) * FFN_CHUNK)
        g = jnp.dot(h, wg_ref[:, cols], preferred_element_type=F32)
        u = jnp.dot(h, wu_ref[:, cols], preferred_element_type=F32)
        a = (_silu(g) * u).astype(BF16)
        part = jnp.dot(a, wd_ref[cols, :], preferred_element_type=F32)
        if c == 0:
            acc_ref[...] = part
        else:
            acc_ref[...] += part


def _ffn_kernel(x_ref, mod_ref, g2_ref, wg_ref, wu_ref, wd_ref, o_ref, acc_ref):
    x = x_ref[...]
    h = _modnorm(x, g2_ref[...], mod_ref[0, 4:5, :], mod_ref[0, 3:4, :]).astype(BF16)
    _swiglu_accumulate(h, wg_ref, wu_ref, wd_ref, acc_ref)
    o_ref[...] = x + mod_ref[0, 5:6, :] * acc_ref[...]


def _ffn(xa, mods, g2, wg, wu, wd, n_batch, seq_len):
    t, d = xa.shape
    tm = TOKEN_TILE
    row = pl.BlockSpec((tm, d), lambda i: (i, 0))
    resident = lambda a: pl.BlockSpec(a.shape, lambda i: (0,) * a.ndim, pipeline_mode=pl.Buffered(1))
    return pl.pallas_call(
        _ffn_kernel,
        grid=(t // tm,),
        in_specs=[row, pl.BlockSpec((1, 6, d), _mod_index(seq_len // tm, n_batch)),
                  pl.BlockSpec(g2.shape, lambda i: (0, 0)), resident(wg), resident(wu), resident(wd)],
        out_specs=row,
        out_shape=jax.ShapeDtypeStruct((t, d), F32),
        scratch_shapes=[pltpu.VMEM((tm, d), F32)],
        compiler_params=_cparams(1),
        name="ffn",
    )(xa, mods, g2, wg, wu, wd)


def _router_kernel(x_ref, mod_ref, g2_ref, r_ref, h_ref, idx_ref, wt_ref):
    h = _modnorm(x_ref[...], g2_ref[...], mod_ref[0, 4:5, :], mod_ref[0, 3:4, :])
    h_ref[...] = h
    logits = jnp.dot(h, r_ref[...], precision=lax.Precision.HIGHEST, preferred_element_type=F32)
    lane = lax.broadcasted_iota(jnp.int32, logits.shape, 1)
    neg = -jnp.inf
    l1 = jnp.where(lane < N_EXPERTS, logits, neg)
    m1 = jnp.max(l1, axis=-1, keepdims=True)
    i1 = jnp.min(jnp.where(l1 == m1, lane, LANES), axis=-1, keepdims=True)
    l2 = jnp.where(lane == i1, neg, l1)
    m2 = jnp.max(l2, axis=-1, keepdims=True)
    i2 = jnp.min(jnp.where(l2 == m2, lane, LANES), axis=-1, keepdims=True)
    e = jnp.exp(m2 - m1)
    w1 = 1.0 / (1.0 + e)
    w2 = e / (1.0 + e)
    idx_ref[...] = jnp.where(lane == 0, i1, jnp.where(lane == 1, i2, 0))
    wt_ref[...] = jnp.where(lane == 0, w1, jnp.where(lane == 1, w2, 0.0))


def _router(xa, mods, g2, r_pad, n_rows, n_batch, seq_len):
    d = xa.shape[1]
    tm = TOKEN_TILE
    row = lambda w: pl.BlockSpec((tm, w), lambda i: (i, 0))
    return pl.pallas_call(
        _router_kernel,
        grid=(n_rows // tm,),
        in_specs=[row(d), pl.BlockSpec((1, 6, d), _mod_index(seq_len // tm, n_batch)),
                  pl.BlockSpec(g2.shape, lambda i: (0, 0)), pl.BlockSpec(r_pad.shape, lambda i: (0, 0))],
        out_specs=[row(d), row(LANES), row(LANES)],
        out_shape=[jax.ShapeDtypeStruct((n_rows, d), F32),
                   jax.ShapeDtypeStruct((n_rows, LANES), jnp.int32),
                   jax.ShapeDtypeStruct((n_rows, LANES), F32)],
        compiler_params=_cparams(1),
        name="moe_router",
    )(xa, mods, g2, r_pad)


def _row_copy(src, src_row, dst, dst_row, sem):
    return pltpu.make_async_copy(src.at[pl.ds(src_row, 1), :], dst.at[pl.ds(dst_row, 1), :], sem)


DMA_ISSUE_UNROLL = 8


def _idx_copy(dest_hbm, dest_smem, sem_idx, tile, slot):
    n = dest_hbm.shape[1]
    half = dest_smem.at[pl.ds(pl.multiple_of(slot * n, n), n)]
    return pltpu.make_async_copy(dest_hbm.at[tile], half, sem_idx.at[slot])


def _dispatch_kernel(pad_tile_ref, dest_hbm, h_ref, xs_out, dest_smem, zeros, sem_idx, sem_rows, sem_zero):
    i = pl.program_id(0)
    n = pl.num_programs(0)
    tm = h_ref.shape[0]
    slot = i % 2

    @pl.when(i == 0)
    def _():
        zeros[...] = jnp.zeros_like(zeros)

        def zero_copy(e):
            return pltpu.make_async_copy(zeros, xs_out.at[pl.ds(pl.multiple_of(pad_tile_ref[e], tm), tm), :],
                                         sem_zero)

        for e in range(N_EXPERTS):
            @pl.when(pad_tile_ref[e] >= 0)
            def _(e=e):
                zero_copy(e).start()
        for e in range(N_EXPERTS):
            @pl.when(pad_tile_ref[e] >= 0)
            def _(e=e):
                zero_copy(e).wait()
        _idx_copy(dest_hbm, dest_smem, sem_idx, 0, 0).start()

    @pl.when(i + 1 < n)
    def _():
        _idx_copy(dest_hbm, dest_smem, sem_idx, i + 1, 1 - slot).start()

    _idx_copy(dest_hbm, dest_smem, sem_idx, i, slot).wait()

    base = slot * (2 * tm)

    def issue(r, carry):
        _row_copy(h_ref, r, xs_out, dest_smem[base + 2 * r], sem_rows).start()
        _row_copy(h_ref, r, xs_out, dest_smem[base + 2 * r + 1], sem_rows).start()
        return carry

    lax.fori_loop(0, tm, issue, 0, unroll=DMA_ISSUE_UNROLL)
    for _ in range(2):
        pltpu.make_async_copy(h_ref, xs_out.at[pl.ds(0, tm), :], sem_rows).wait()


def _dispatch(pad_tile, dest, h, n_slots):
    n_rows, d = h.shape
    tm = TOKEN_TILE
    assert tm == EXPERT_TILE
    return pl.pallas_call(
        _dispatch_kernel,
        grid_spec=pltpu.PrefetchScalarGridSpec(
            num_scalar_prefetch=1,
            grid=(n_rows // tm,),
            in_specs=[pl.BlockSpec(memory_space=pl.ANY), pl.BlockSpec((tm, d), lambda i, pt: (i, 0))],
            out_specs=pl.BlockSpec(memory_space=pl.ANY),
            scratch_shapes=[pltpu.SMEM((4 * tm,), jnp.int32), pltpu.VMEM((tm, d), F32),
                            pltpu.SemaphoreType.DMA((2,)), pltpu.SemaphoreType.DMA(()),
                            pltpu.SemaphoreType.DMA(())]),
        out_shape=jax.ShapeDtypeStruct((n_slots, d), F32),
        compiler_params=_cparams(1),
        name="moe_dispatch",
    )(pad_tile, dest.reshape(n_rows // tm, 2 * tm), h)


def _expert_kernel(te_ref, nused_ref, xs_ref, wg_ref, wu_ref, wd_ref, o_ref, acc_ref):
    del te_ref
    live = pl.program_id(0) < nused_ref[0]

    @pl.when(live)
    def _():
        _swiglu_accumulate(xs_ref[...].astype(BF16), wg_ref, wu_ref, wd_ref, acc_ref)
        o_ref[...] = acc_ref[...]

    @pl.when(jnp.logical_not(live))
    def _():
        o_ref[...] = jnp.zeros_like(o_ref)


def _experts(tile_expert, n_used, xs, wg, wu, wd):
    s, d = xs.shape
    tm = EXPERT_TILE
    row = pl.BlockSpec((tm, d), lambda i, te, nu: (i, 0))
    xs_row = pl.BlockSpec((tm, d), lambda i, te, nu: (jnp.minimum(i, nu[0] - 1), 0))
    wspec = lambda a: pl.BlockSpec((None,) + a.shape[1:], lambda i, te, nu: (te[i], 0, 0))
    return pl.pallas_call(
        _expert_kernel,
        grid_spec=pltpu.PrefetchScalarGridSpec(
            num_scalar_prefetch=2,
            grid=(s // tm,),
            in_specs=[xs_row, wspec(wg), wspec(wu), wspec(wd)],
            out_specs=row,
            scratch_shapes=[pltpu.VMEM((tm, d), F32)]),
        out_shape=jax.ShapeDtypeStruct((s, d), F32),
        compiler_params=_cparams(1),
        name="moe_experts",
    )(tile_expert, n_used, xs, wg, wu, wd)


def _combine_kernel(dest_hbm, eo_hbm, x_ref, wt_ref, mod_ref, o_ref, dest_smem, buf, sem_idx, sem_rows):
    i = pl.program_id(0)
    n = pl.num_programs(0)
    tm = x_ref.shape[0]
    slot = i % 2

    def gather(s):
        base = s * (2 * tm)

        def issue(r, carry):
            _row_copy(eo_hbm, dest_smem[base + 2 * r], buf.at[s, 0], r, sem_rows.at[s]).start()
            _row_copy(eo_hbm, dest_smem[base + 2 * r + 1], buf.at[s, 1], r, sem_rows.at[s]).start()
            return carry

        lax.fori_loop(0, tm, issue, 0, unroll=DMA_ISSUE_UNROLL)

    @pl.when(i == 0)
    def _():
        first = _idx_copy(dest_hbm, dest_smem, sem_idx, 0, 0)
        first.start()
        first.wait()
        gather(0)

        @pl.when(n > 1)
        def _():
            _idx_copy(dest_hbm, dest_smem, sem_idx, 1, 1).start()

    @pl.when(i + 1 < n)
    def _():
        _idx_copy(dest_hbm, dest_smem, sem_idx, i + 1, 1 - slot).wait()

        @pl.when(i + 2 < n)
        def _():
            _idx_copy(dest_hbm, dest_smem, sem_idx, i + 2, slot).start()

        gather(1 - slot)

    for k in range(2):
        pltpu.make_async_copy(eo_hbm.at[pl.ds(0, tm), :], buf.at[slot, k], sem_rows.at[slot]).wait()
    wt = wt_ref[...]
    y = wt[:, 0:1] * buf[slot, 0] + wt[:, 1:2] * buf[slot, 1]
    o_ref[...] = x_ref[...] + mod_ref[0, 5:6, :] * y


def _combine(dest, eo, xa, wts, mods, n_rows, n_batch, seq_len):
    d = xa.shape[1]
    tm = TOKEN_TILE
    row = lambda w: pl.BlockSpec((tm, w), lambda i: (i, 0))
    return pl.pallas_call(
        _combine_kernel,
        grid=(n_rows // tm,),
        in_specs=[pl.BlockSpec(memory_space=pl.ANY), pl.BlockSpec(memory_space=pl.ANY), row(d), row(LANES),
                  pl.BlockSpec((1, 6, d), _mod_index(seq_len // tm, n_batch))],
        out_specs=row(d),
        out_shape=jax.ShapeDtypeStruct((n_rows, d), F32),
        scratch_shapes=[pltpu.SMEM((4 * tm,), jnp.int32), pltpu.VMEM((2, 2, tm, d), F32),
                        pltpu.SemaphoreType.DMA((2,)), pltpu.SemaphoreType.DMA((2,))],
        compiler_params=_cparams(1),
        name="moe_combine",
    )(dest.reshape(n_rows // tm, 2 * tm), eo, xa, wts, mods)


def _moe(xa, mods, g2, r_pad, wg, wu, wd, n_rows, n_batch, seq_len):
    h, idx, wts = _router(xa, mods, g2, r_pad, n_rows, n_batch, seq_len)
    tm = EXPERT_TILE
    e_flat = idx[:, :2].reshape(-1)
    onehot = (e_flat[:, None] == jnp.arange(N_EXPERTS, dtype=jnp.int32)[None, :]).astype(jnp.int32)
    csum = jnp.cumsum(onehot, axis=0)
    counts = csum[-1]
    rank = jnp.sum(onehot * csum, axis=1) - 1
    padded = ((counts + tm - 1) // tm) * tm
    ends = jnp.cumsum(padded)
    starts = ends - padded
    dest = (jnp.sum(onehot * starts[None, :], axis=1) + rank).astype(jnp.int32)
    n_slots = 2 * n_rows + N_EXPERTS * tm
    tile_start = jnp.arange(n_slots // tm, dtype=jnp.int32) * tm
    tile_expert = jnp.minimum(jnp.sum((tile_start[:, None] >= ends[None, :]).astype(jnp.int32), axis=1),
                              N_EXPERTS - 1).astype(jnp.int32)
    n_used = (ends[-1:] // tm).astype(jnp.int32)
    pad_tile = jnp.where(padded > 0, ends - tm, -1).astype(jnp.int32)
    xs = _dispatch(pad_tile, dest, h, n_slots)
    eo = _experts(tile_expert, n_used, xs, wg, wu, wd)
    return _combine(dest, eo, xa, wts, mods, n_rows, n_batch, seq_len)


def _softplus(v):
    return jnp.maximum(v, 0.0) + jnp.log1p(jnp.exp(-jnp.abs(v)))


def _ssd_kernel(xl_ref, dl_ref, xc_ref, dc_ref, cw_ref, cb_ref, dtb_ref, alogc_ref, dsk_ref, *rest, want_ctx):
    if want_ctx:
        yl_ref, yc_ref, xs_l, dtt_l, xs_c, dtt_c, st_ref = rest
    else:
        yl_ref, xs_l, dtt_l, xs_c, dtt_c, st_ref = rest
        yc_ref = None
    ck = SSD_CHUNK
    hp = SSD_HEAD_DIM
    ns = SSD_STATE
    n_col = 2 * SSD_HEADS
    hi = lax.Precision.HIGHEST
    row = lax.broadcasted_iota(jnp.int32, (ck, 1), 0)
    li = lax.broadcasted_iota(jnp.int32, (ck, ck), 0)
    si = lax.broadcasted_iota(jnp.int32, (ck, ck), 1)
    masks = (si <= li, si >= li)
    tris = (masks[0].astype(F32), masks[1].astype(F32))
    a_col = -jnp.exp(alogc_ref[...])
    dskip = dsk_ref[...]

    def prep(raw_ref, dtraw_ref, xs_s, dtt_s, y_ref):
        n = raw_ref.shape[0]
        nk = n // ck

        def body(k, carry):
            r0 = pl.multiple_of(k * ck, ck)
            a = raw_ref[pl.ds(r0, ck), :]
            top = raw_ref[pl.ds(pl.multiple_of(jnp.maximum(r0 - 8, 0), 8), 8), :][7:8, :]
            bot = raw_ref[pl.ds(pl.multiple_of(jnp.minimum(r0 + ck, n - 8), 8), 8), :][0:1, :]
            top = jnp.where(k > 0, top, 0.0)
            bot = jnp.where(k < nk - 1, bot, 0.0)
            prev = jnp.where(row == 0, top, pltpu.roll(a, 1, 0))
            nxt = jnp.where(row == ck - 1, bot, pltpu.roll(a, ck - 1, 0))
            xs = _silu(prev * cw_ref[0:1, :] + a * cw_ref[1:2, :] + nxt * cw_ref[2:3, :] + cb_ref[...])
            xs_s[pl.ds(r0, ck), :] = xs
            dt = _softplus(dtraw_ref[pl.ds(r0, ck), :] + dtb_ref[...])
            dtt_s[:, pl.ds(r0, ck)] = dt.T[0:n_col, :]
            if y_ref is not None:
                y_ref[pl.ds(r0, ck), :] = xs[:, 0:SSD_WIDTH] * dskip
            return carry

        lax.fori_loop(0, nk, body, 0)

    def run(xs_s, dtt_s, y_ref):
        nk = xs_s.shape[0] // ck

        def one(kk, dr):
            r0 = pl.multiple_of(kk * ck, ck)
            xc = xs_s[pl.ds(r0, ck), :]
            dtr = dtt_s[:, pl.ds(r0, ck)]
            dta_r = dtr * a_col
            la_r = jnp.dot(dta_r, tris[1 - dr], precision=hi, preferred_element_type=F32)
            la_end = la_r[:, ck - 1:ck] if dr == 0 else la_r[:, 0:1]
            if y_ref is not None:
                la_c = jnp.concatenate([la_r, jnp.zeros((ck - n_col, ck), F32)], axis=0).T
            ys = []
            for g in range(SSD_GROUPS):
                b_t = xc[:, SSD_WIDTH + g * ns:SSD_WIDTH + (g + 1) * ns].T
                c0 = SSD_WIDTH + SSD_GROUPS * ns + g * ns
                c_g = xc[:, c0:c0 + ns].astype(BF16)
                if y_ref is not None:
                    scores = jnp.dot(c_g, b_t.astype(BF16), preferred_element_type=F32)
                for hh in range(SSD_HEADS // SSD_GROUPS):
                    h = g * (SSD_HEADS // SSD_GROUPS) + hh
                    col = dr * SSD_HEADS + h
                    xh = xc[:, h * hp:(h + 1) * hp].astype(BF16)
                    dt_row = dtr[col:col + 1, :]
                    le = la_end[col:col + 1, :]
                    st = st_ref[col]
                    if y_ref is not None:
                        la_col = la_c[:, col:col + 1]
                        decay = jnp.exp(jnp.where(masks[dr], la_col - la_r[col:col + 1, :], -jnp.inf))
                        y = jnp.dot((scores * decay * dt_row).astype(BF16), xh, preferred_element_type=F32)
                        y = y + jnp.dot(c_g, st.astype(BF16), preferred_element_type=F32) * jnp.exp(la_col)
                        ys.append(y)
                    bw = (b_t * (dt_row * jnp.exp(le - la_r[col:col + 1, :]))).astype(BF16)
                    st_ref[col] = st * jnp.exp(le) + jnp.dot(bw, xh, preferred_element_type=F32)
            if y_ref is not None:
                y_ref[pl.ds(r0, ck), :] += jnp.concatenate(ys, axis=1)

        def body(k, carry):
            one(k, 0)
            one(nk - 1 - k, 1)
            return carry

        lax.fori_loop(0, nk, body, 0)

    prep(xl_ref, dl_ref, xs_l, dtt_l, yl_ref)
    prep(xc_ref, dc_ref, xs_c, dtt_c, yc_ref)
    st_ref[...] = jnp.zeros_like(st_ref)
    run(xs_c, dtt_c, yc_ref)
    run(xs_l, dtt_l, yl_ref)


def _ssd(xbc, dtp, conv_w, conv_b, dt_bias, a_log, d_skip, n_batch, seq_len, ctx_len, want_ctx):
    ctx0 = n_batch * seq_len // ctx_len
    pad = lambda v: jnp.pad(v.reshape(1, -1), ((0, 0), (0, LANES - v.size)))
    lat = lambda w: pl.BlockSpec((seq_len, w), lambda b: (b, 0))
    ctx = lambda w: pl.BlockSpec((ctx_len, w), lambda b: (ctx0 + b, 0))
    const = lambda a: pl.BlockSpec(a.shape, lambda b: (0,) * a.ndim)
    consts = (conv_w, conv_b.reshape(1, -1), pad(dt_bias), a_log.reshape(-1, 1),
              jnp.repeat(d_skip, SSD_HEAD_DIM).reshape(1, -1))
    out_specs = [lat(SSD_WIDTH)]
    out_shape = [jax.ShapeDtypeStruct((n_batch * seq_len, SSD_WIDTH), F32)]
    if want_ctx:
        out_specs.append(pl.BlockSpec((ctx_len, SSD_WIDTH), lambda b: (b, 0)))
        out_shape.append(jax.ShapeDtypeStruct((n_batch * ctx_len, SSD_WIDTH), F32))
    return pl.pallas_call(
        functools.partial(_ssd_kernel, want_ctx=want_ctx),
        grid=(n_batch,),
        in_specs=[lat(SSD_XBC_COLS), lat(LANES), ctx(SSD_XBC_COLS), ctx(LANES)] + [const(a) for a in consts],
        out_specs=out_specs,
        out_shape=out_shape,
        scratch_shapes=[pltpu.VMEM((seq_len, SSD_XBC_COLS), F32), pltpu.VMEM((2 * SSD_HEADS, seq_len), F32),
                        pltpu.VMEM((ctx_len, SSD_XBC_COLS), F32), pltpu.VMEM((2 * SSD_HEADS, ctx_len), F32),
                        pltpu.VMEM((2 * SSD_HEADS, SSD_STATE, SSD_HEAD_DIM), F32)],
        compiler_params=_cparams(1),
        name="ssd",
    )(xbc, dtp, xbc, dtp, *consts)


HY_BLOCK = 256
HY_CH_STEP = 8


def _hyena_tables(seq_len):
    nj = 2 * seq_len
    lag = jnp.arange(nj, dtype=jnp.int32) - seq_len
    dist = jnp.abs(lag)
    pos = jnp.minimum(dist, seq_len - 1)
    t = jnp.linspace(0.0, 1.0, seq_len, dtype=F32)[pos]
    w = ((2.0 * math.pi / seq_len) * jnp.arange(seq_len, dtype=F32))[pos]
    bands = (HYENA_POS_DIM - 1) // 2
    freqs = jnp.linspace(1e-4, bands - 1, bands, dtype=F32)[None, :]
    z = jnp.concatenate([t[:, None], jnp.cos(freqs * w[:, None]), -jnp.sin(freqs * w[:, None])], axis=-1)
    zt = jnp.pad(z.T, ((0, (-HYENA_POS_DIM) % 8), (0, 0)))
    deltas = jnp.abs(jnp.linspace(math.log(HYENA_DECAY_TARGET) / HYENA_SLOW_DECAY,
                                  math.log(HYENA_DECAY_TARGET) / HYENA_FAST_DECAY, HYENA_WIDTH, dtype=F32))
    dec = jnp.exp(-t[None, :] * deltas[:, None]) * (dist < seq_len).astype(F32)[None, :]
    fwd = (lag >= 0).astype(F32)[None, :]
    return zt, dec, fwd


def _hyfilt_kernel(zt_ref, dec_ref, fwd_ref, w1_ref, b1_ref, f1_ref, w2_ref, b2_ref, f2_ref, w3_ref, o_ref):
    hi = lax.Precision.HIGHEST
    h = jnp.sin(f1_ref[...] * (jnp.dot(w1_ref[...], zt_ref[...], precision=hi, preferred_element_type=F32)
                               + b1_ref[...]))
    h = jnp.sin(f2_ref[...] * (jnp.dot(w2_ref[...], h, precision=hi, preferred_element_type=F32) + b2_ref[...]))
    hw = jnp.dot(w3_ref[...], h, precision=hi, preferred_element_type=F32)
    fwd = fwd_ref[...] > 0.5
    dec = dec_ref[...]
    nw = HYENA_WIDTH
    for o in range(HYENA_ORDER):
        o_ref[o] = jnp.where(fwd, hw[o * nw:(o + 1) * nw], hw[(HYENA_ORDER + o) * nw:(HYENA_ORDER + o + 1) * nw]) * dec


def _hyena_filters(tables, w1, b1, f1, w2, b2, f2, w3):
    zt, dec, fwd = tables
    nj = zt.shape[1]
    tj = 512
    col = lambda v: v.reshape(-1, 1)
    w1t = jnp.pad(w1.T, ((0, 0), (0, zt.shape[0] - w1.shape[0])))
    consts = (w1t, col(b1), col(f1), w2.T, col(b2), col(f2), w3.T)
    lanes = lambda a: pl.BlockSpec((a.shape[0], tj), lambda j: (0, j))
    const = lambda a: pl.BlockSpec(a.shape, lambda j: (0, 0))
    return pl.pallas_call(
        _hyfilt_kernel,
        grid=(nj // tj,),
        in_specs=[lanes(zt), lanes(dec), lanes(fwd)] + [const(a) for a in consts],
        out_specs=pl.BlockSpec((HYENA_ORDER, HYENA_WIDTH, tj), lambda j: (0, 0, j)),
        out_shape=jax.ShapeDtypeStruct((HYENA_ORDER, HYENA_WIDTH, nj), F32),
        compiler_params=_cparams(1),
        name="hyena_filters",
    )(zt, dec, fwd, *consts)


def _hyconv_kernel(cw_ref, cb_ref, hb_ref, v_ref, x1_ref, x2_ref, kf_ref, o_ref):
    n_ch, n_b, seq_len = v_ref.shape
    nb = seq_len // HY_BLOCK
    blk = HY_BLOCK
    c_base = pl.program_id(0) * n_ch
    lane = lax.broadcasted_iota(jnp.int32, (1, seq_len), 1)

    def sconv(x, ch):
        prev = jnp.where(lane == 0, 0.0, pltpu.roll(x, 1, 1))
        nxt = jnp.where(lane == seq_len - 1, 0.0, pltpu.roll(x, seq_len - 1, 1))
        return prev * cw_ref[0, ch] + x * cw_ref[1, ch] + nxt * cw_ref[2, ch] + cb_ref[ch]

    def long_conv(vals, kf_row):
        skew = pltpu.roll(jnp.broadcast_to(kf_row, (blk, 2 * seq_len)), 0, 1, stride=1, stride_axis=0)
        vb = vals.astype(BF16)
        acc = [None] * nb
        for d in range(-(nb - 1), nb):
            tt = skew[:, seq_len + d * blk:seq_len + (d + 1) * blk].astype(BF16)
            sis = list(range(max(0, -d), min(nb, nb - d)))
            lhs = [vb[:, s * blk:(s + 1) * blk] for s in sis]
            lhs = lhs[0] if len(lhs) == 1 else jnp.concatenate(lhs, axis=0)
            out = jnp.dot(lhs, tt, preferred_element_type=F32)
            for idx, s in enumerate(sis):
                piece = out[idx * n_b:(idx + 1) * n_b]
                acc[s + d] = piece if acc[s + d] is None else acc[s + d] + piece
        return acc[0] if nb == 1 else jnp.concatenate(acc, axis=1)

    def channel(cc, carry):
        ch = c_base + cc
        v = sconv(v_ref[cc], ch)
        x1 = sconv(x1_ref[cc], HYENA_WIDTH + ch)
        x2 = sconv(x2_ref[cc], 2 * HYENA_WIDTH + ch)
        z = x1 * (long_conv(v, kf_ref[0, cc]) + v * hb_ref[0, ch])
        o_ref[cc] = x2 * (long_conv(z, kf_ref[1, cc]) + z * hb_ref[1, ch])
        return carry

    lax.fori_loop(0, n_ch, channel, 0)


def _hyena_conv(hyt, kf, conv_w, conv_b, hy_bias, n_batch, seq_len, row0):
    t = hyt.shape[1]
    cs = HY_CH_STEP
    nw = HYENA_WIDTH
    view = hyt.reshape(3 * nw, t // seq_len, seq_len)
    blk0 = row0 // n_batch
    stream = lambda k: pl.BlockSpec((cs, n_batch, seq_len), lambda c: (k * (nw // cs) + c, blk0, 0))
    smem = pl.BlockSpec(memory_space=pltpu.SMEM)
    return pl.pallas_call(
        _hyconv_kernel,
        grid=(nw // cs,),
        in_specs=[smem, smem, smem, stream(0), stream(1), stream(2),
                  pl.BlockSpec((HYENA_ORDER, cs, 1, 2 * seq_len), lambda c: (0, c, 0, 0))],
        out_specs=pl.BlockSpec((cs, n_batch, seq_len), lambda c: (c, 0, 0)),
        out_shape=jax.ShapeDtypeStruct((nw, n_batch, seq_len), F32),
        compiler_params=_cparams(1),
        name="hyena_conv",
    )(conv_w, conv_b, hy_bias, view, view, view, kf.reshape(HYENA_ORDER, nw, 1, 2 * seq_len))


def _rope_tables(seq_len, extra):
    rows = seq_len // GRID_W
    row = jnp.repeat(jnp.arange(rows, dtype=F32), GRID_W)
    col = jnp.tile(jnp.arange(GRID_W, dtype=F32), rows)
    inv = ROPE_THETA ** (-jnp.arange(0, ROPE_AXIS_DIM, 2, dtype=F32) / ROPE_AXIS_DIM)
    ang = jnp.stack([row[:, None] * inv, col[:, None] * inv], axis=1)
    cos = jnp.cos(ang)
    sin = jnp.sin(ang)
    cos_h = jnp.concatenate([cos, cos], axis=-1).reshape(seq_len, HEAD_DIM)
    sin_h = jnp.concatenate([-sin, sin], axis=-1).reshape(seq_len, HEAD_DIM)
    cos_t = jnp.concatenate([jnp.tile(cos_h, (1, LANES // HEAD_DIM)), jnp.ones((extra, LANES), F32)], axis=0)
    sin_t = jnp.concatenate([jnp.tile(sin_h, (1, LANES // HEAD_DIM)), jnp.zeros((extra, LANES), F32)], axis=0)
    return cos_t, sin_t


def _block_diag_ones(n, seg):
    i = jnp.arange(n) // seg
    return (i[:, None] == i[None, :]).astype(BF16)


def kernel(x, c, ctx, c_ctx, w_ada, b_ada, norm1, norm2, w_in, w_out, q_norm, k_norm, att_sinks, att_out_norm, ssd_conv_w, ssd_conv_b, ssd_dt_bias, ssd_a_log, ssd_d, ssd_norm, hy_conv_w, hy_conv_b, hy_w1, hy_b1, hy_f1, hy_w2, hy_b2, hy_f2, hy_w3, hy_bias, hy_out_norm, ffn_w_gate, ffn_w_up, ffn_w_down, moe_router, moe_w_gate, moe_w_up, moe_w_down):
    n_batch, seq_len, d = x.shape
    ctx_len = ctx.shape[1]
    n_lat = n_batch * seq_len
    n_ctx = n_batch * ctx_len
    depth = w_in.shape[0]
    xa = jnp.concatenate([x.reshape(n_lat, d), ctx.reshape(n_ctx, d)], axis=0)

    cc = jnp.concatenate([c, c_ctx[None, :]], axis=0)
    pad_rows = (-cc.shape[0]) % 8
    cc = jnp.pad(cc, ((0, pad_rows), (0, 0)))
    mods_all = _adaln(cc, w_ada, b_ada)[:, :n_batch + 1].reshape(depth, n_batch + 1, 6, d)

    cos_t, sin_t = _rope_tables(seq_len, TOKEN_TILE)
    bd_q = _block_diag_ones(Q_COLS, HEAD_DIM)
    bd_h = _block_diag_ones(HYENA_WIDTH, HYENA_WIDTH // HYENA_GROUPS)
    hy_tab_l = _hyena_tables(seq_len)
    hy_tab_c = _hyena_tables(ctx_len)

    for i in range(depth):
        last = i == depth - 1
        j = i // 2
        mods = mods_all[i]
        wi = w_in[i]
        c_dt = QKV_W + ZX_W
        w_cat = jnp.concatenate([wi[:, :c_dt], wi[:, c_dt + SSD_DT_COLS:], wi[:, c_dt:c_dt + SSD_DT_COLS],
                                 jnp.zeros((d, LANES - SSD_DT_COLS), F32)], axis=1).astype(BF16)
        qg = jnp.tile(q_norm[i], Q_COLS // HEAD_DIM)[None, :]
        kg = jnp.tile(k_norm[i], KV_COLS // HEAD_DIM)[None, :]
        q, k, v, z, xbc, hyt, dtp = _inproj(xa, mods, norm1[i][None, :], w_cat, cos_t, sin_t, qg, kg, bd_q,
                                            n_batch, seq_len)

        att_l = _attention(att_sinks[i], q, k, v, n_batch, seq_len, ctx_len, True)
        ssd_out = _ssd(xbc, dtp, ssd_conv_w[i], ssd_conv_b[i], ssd_dt_bias[i], ssd_a_log[i], ssd_d[i],
                       n_batch, seq_len, ctx_len, not last)
        filt = (hy_w1[i], hy_b1[i], hy_f1[i], hy_w2[i], hy_b2[i], hy_f2[i], hy_w3[i])
        hyo_l = _hyena_conv(hyt, _hyena_filters(hy_tab_l, *filt), hy_conv_w[i], hy_conv_b[i], hy_bias[i],
                            n_batch, seq_len, 0).reshape(HYENA_WIDTH, n_lat)
        if last:
            att = (att_l, None)
            sy = (ssd_out[0], None)
            hyo = (hyo_l, None)
            n_rows = n_lat
        else:
            att = (att_l, _attention(att_sinks[i], q, k, v, n_batch, seq_len, ctx_len, False))
            sy = tuple(ssd_out)
            hyo_c = _hyena_conv(hyt, _hyena_filters(hy_tab_c, *filt), hy_conv_w[i], hy_conv_b[i], hy_bias[i],
                                n_batch, ctx_len, n_lat // ctx_len).reshape(HYENA_WIDTH, n_ctx)
            hyo = (hyo_l, hyo_c)
            n_rows = n_lat + n_ctx
        xa = _merge(att, sy, hyo, z, xa, mods, att_out_norm[i][None, :], ssd_norm[i][None, :],
                    hy_out_norm[i][None, :], bd_h, w_out[i].astype(BF16), n_batch, seq_len)

        g2 = norm2[i][None, :]
        if i % 2 == 0:
            xa = _ffn(xa, mods, g2, ffn_w_gate[j].astype(BF16), ffn_w_up[j].astype(BF16),
                      ffn_w_down[j].astype(BF16), n_batch, seq_len)
        else:
            r_pad = jnp.pad(moe_router[j], ((0, 0), (0, LANES - N_EXPERTS)))
            xa = _moe(xa, mods, g2, r_pad, moe_w_gate[j].astype(BF16), moe_w_up[j].astype(BF16),
                      moe_w_down[j].astype(BF16), n_rows, n_batch, seq_len)
    return xa[:n_lat].reshape(n_batch, seq_len, d)
```

```python
import functools
import math

import jax
import jax.numpy as jnp
from jax import lax
from jax.experimental import pallas as pl
from jax.experimental.pallas import tpu as pltpu

F32 = jnp.float32
BF16 = jnp.bfloat16

D_MODEL = 1024
DEPTH = 4
GRID_W = 64
EPS = 1e-6
HEAD_DIM = 64
ATT_WIDTH = 512
ATT_HEADS = 8
ATT_KV_HEADS = 2
ATT_GROUP = 4
WINDOW = 128
ATT_BLOCK = 128
ROPE_THETA = 10000.0
ROPE_AXIS_DIM = 32
SSD_WIDTH = 256
SSD_HEAD_DIM = 64
SSD_HEADS = 4
SSD_STATE = 64
SSD_GROUPS = 2
SSD_CHUNK = 128
HYENA_WIDTH = 256
HYENA_GROUPS = 4
HYENA_ORDER = 2
HYENA_POS_DIM = 33
HYENA_FAST_DECAY = 0.3
HYENA_SLOW_DECAY = 1.5
HYENA_DECAY_TARGET = 1e-2
Q_COLS = 512
KV_COLS = 128
SSD_XBC_COLS = 512
SSD_DT_COLS = 8
HY_COLS = 768
FFN_DIM = 2816
N_EXPERTS = 8
FFN_CHUNK = 256
N_FFN_CHUNKS = FFN_DIM // FFN_CHUNK
LANES = 128
QKV_W = Q_COLS + 2 * KV_COLS
ZX_W = SSD_WIDTH + SSD_XBC_COLS
PROJ_PAD = QKV_W + ZX_W + HY_COLS + LANES
VMEM_LIMIT = 56 * 1024 * 1024
TOKEN_TILE = 512
EXPERT_TILE = 512


def _cparams(n_axes):
    return pltpu.CompilerParams(dimension_semantics=("arbitrary",) * n_axes,
                                vmem_limit_bytes=VMEM_LIMIT)


def _silu(v):
    return v / (1.0 + jnp.exp(-v))


def _modnorm(x, g, scale, shift):
    ms = jnp.mean(x * x, axis=-1, keepdims=True)
    return x * lax.rsqrt(ms + EPS) * g * (1.0 + scale) + shift


def _segsum(t, bd):
    hi = t.astype(BF16)
    lo = (t - hi.astype(F32)).astype(BF16)
    return (jnp.dot(hi, bd, preferred_element_type=F32)
            + jnp.dot(lo, bd, preferred_element_type=F32))


def _mod_index(tiles_per_batch, n_batch):
    return lambda i: (jnp.minimum(i // tiles_per_batch, n_batch), 0, 0)


def _adaln_kernel(c_ref, w_ref, b_ref, o_ref):
    s = _silu(c_ref[...]).astype(BF16)
    o_ref[...] = jnp.dot(s, w_ref[...].astype(BF16), preferred_element_type=F32) + b_ref[...]


def _adaln(cc, w_ada, b_ada):
    depth, d, n = w_ada.shape
    r = cc.shape[0]
    tn = 512
    return pl.pallas_call(
        _adaln_kernel,
        grid=(depth, n // tn),
        in_specs=[pl.BlockSpec((r, d), lambda l, j: (0, 0)),
                  pl.BlockSpec((None, d, tn), lambda l, j: (l, 0, j)),
                  pl.BlockSpec((None, 1, tn), lambda l, j: (l, 0, j))],
        out_specs=pl.BlockSpec((None, r, tn), lambda l, j: (l, 0, j)),
        out_shape=jax.ShapeDtypeStruct((depth, r, n), F32),
        compiler_params=_cparams(2),
        name="adaln",
    )(cc, w_ada, b_ada.reshape(depth, 1, n))


def _inproj_kernel(x_ref, mod_ref, g1_ref, w_ref, cos_ref, sin_ref, qg_ref, kg_ref, bd_ref,
                   q_ref, k_ref, v_ref, z_ref, xbc_ref, hyl_ref, hyc_ref, dt_ref, *, n_lat_tiles):
    x = x_ref[...]
    h = _modnorm(x, g1_ref[...], mod_ref[0, 1:2, :], mod_ref[0, 0:1, :]).astype(BF16)
    pq = jnp.dot(h, w_ref[:, 0:QKV_W], preferred_element_type=F32)
    cos = cos_ref[...]
    sin = sin_ref[...]
    lane = lax.broadcasted_iota(jnp.int32, (1, LANES), 1)
    first_half = (lane % 32) < 16

    def rope(t):
        partner = jnp.where(first_half, pltpu.roll(t, LANES - 16, 1), pltpu.roll(t, 16, 1))
        return t * cos + partner * sin

    q = pq[:, 0:Q_COLS]
    qn = q * lax.rsqrt(_segsum(q * q, bd_ref[...]) * (1.0 / HEAD_DIM) + EPS) * qg_ref[...]
    scale = HEAD_DIM ** -0.5
    for j in range(Q_COLS // LANES):
        pair = (rope(qn[:, LANES * j:LANES * (j + 1)]) * scale).astype(BF16)
        q_ref[2 * j] = pair[:, 0:HEAD_DIM]
        q_ref[2 * j + 1] = pair[:, HEAD_DIM:LANES]
    k = pq[:, Q_COLS:Q_COLS + KV_COLS]
    kn = k * lax.rsqrt(_segsum(k * k, bd_ref[0:KV_COLS, 0:KV_COLS]) * (1.0 / HEAD_DIM) + EPS) * kg_ref[...]
    kt = rope(kn).T.astype(BF16)
    vv = pq[:, Q_COLS + KV_COLS:QKV_W].astype(BF16)
    for j in range(ATT_KV_HEADS):
        k_ref[j] = kt[j * HEAD_DIM:(j + 1) * HEAD_DIM, :]
        v_ref[j] = vv[:, j * HEAD_DIM:(j + 1) * HEAD_DIM]
    zx = jnp.dot(h, w_ref[:, QKV_W:QKV_W + ZX_W], preferred_element_type=F32)
    z_ref[...] = zx[:, 0:SSD_WIDTH]
    xbc_ref[...] = zx[:, SSD_WIDTH:ZX_W]
    hy_t = jnp.dot(h, w_ref[:, QKV_W + ZX_W:QKV_W + ZX_W + HY_COLS], preferred_element_type=F32).T
    is_lat = pl.program_id(0) < n_lat_tiles

    @pl.when(is_lat)
    def _():
        hyl_ref[...] = hy_t

    @pl.when(jnp.logical_not(is_lat))
    def _():
        ctx_len = hyc_ref.shape[2]
        for k in range(hyc_ref.shape[0]):
            hyc_ref[k] = hy_t[:, k * ctx_len:(k + 1) * ctx_len]
    dt_ref[...] = jnp.dot(h, w_ref[:, QKV_W + ZX_W + HY_COLS:PROJ_PAD], preferred_element_type=F32)


def _inproj(xa, mods, g1, w_cat, cos_t, sin_t, qg, kg, bd, n_batch, seq_len, ctx_len):
    t, d = xa.shape
    tm = TOKEN_TILE
    tpb = seq_len // tm
    n_lat = n_batch * tpb
    bpt = tm // ctx_len
    rope_idx = lambda i: (jnp.where(i < n_lat, i % tpb, tpb), 0)
    row = lambda w: pl.BlockSpec((tm, w), lambda i: (i, 0))
    heads = lambda nh: pl.BlockSpec((nh, tm, HEAD_DIM), lambda i: (0, i, 0))
    const = lambda a: pl.BlockSpec(a.shape, lambda i: (0,) * a.ndim)
    lat_tile = lambda i: jnp.minimum(i, n_lat - 1)
    hy_lat = pl.BlockSpec((None, HY_COLS, tm), lambda i: (lat_tile(i) // tpb, 0, lat_tile(i) % tpb))
    hy_ctx = pl.BlockSpec((bpt, HY_COLS, ctx_len), lambda i: (jnp.maximum(i - n_lat, 0), 0, 0))
    return pl.pallas_call(
        functools.partial(_inproj_kernel, n_lat_tiles=n_lat),
        grid=(t // tm,),
        in_specs=[row(d),
                  pl.BlockSpec((1, 6, d), _mod_index(tpb, n_batch)),
                  const(g1), const(w_cat),
                  pl.BlockSpec((tm, LANES), rope_idx), pl.BlockSpec((tm, LANES), rope_idx),
                  const(qg), const(kg), const(bd)],
        out_specs=[heads(ATT_HEADS), pl.BlockSpec((ATT_KV_HEADS, HEAD_DIM, tm), lambda i: (0, 0, i)),
                   heads(ATT_KV_HEADS), row(SSD_WIDTH), row(SSD_XBC_COLS),
                   hy_lat, hy_ctx, row(LANES)],
        out_shape=[jax.ShapeDtypeStruct((ATT_HEADS, t, HEAD_DIM), BF16),
                   jax.ShapeDtypeStruct((ATT_KV_HEADS, HEAD_DIM, t), BF16),
                   jax.ShapeDtypeStruct((ATT_KV_HEADS, t, HEAD_DIM), BF16),
                   jax.ShapeDtypeStruct((t, SSD_WIDTH), F32),
                   jax.ShapeDtypeStruct((t, SSD_XBC_COLS), F32),
                   jax.ShapeDtypeStruct((n_batch, HY_COLS, seq_len), F32),
                   jax.ShapeDtypeStruct((n_batch, HY_COLS, ctx_len), F32),
                   jax.ShapeDtypeStruct((t, LANES), F32)],
        compiler_params=_cparams(1),
        name="inproj",
    )(xa, mods, g1, w_cat, cos_t, sin_t, qg, kg, bd)


def _attn_kernel(sink_ref, q_ref, *refs, n_q, band):
    if band:
        k_ref, v_ref, kc_ref, vc_ref, o_ref, bias_ref = refs
        seq_len = v_ref.shape[1]
        assert n_q >= 3
    else:
        kc_ref, vc_ref, o_ref = refs
    qb = ATT_BLOCK
    rows = ATT_GROUP * qb
    row_id = lax.broadcasted_iota(jnp.int32, (rows, 1), 0)
    nt = (((1,), (1,)), ((), ()))

    if band:
        @pl.when(pl.program_id(0) == 0)
        def _():
            rel0 = (lax.broadcasted_iota(jnp.int32, (rows, band), 1)
                    - lax.broadcasted_iota(jnp.int32, (rows, band), 0) % qb)
            for var in range(3):
                bias_ref[var] = jnp.where(jnp.abs(rel0 - var * WINDOW) <= WINDOW, 0.0, -jnp.inf)

    for j in range(ATT_KV_HEADS):
        kc = kc_ref[j]
        vc = vc_ref[j]
        snk = jnp.zeros((rows, 1), F32)
        for g in range(ATT_GROUP):
            snk = jnp.where(row_id // qb == g, sink_ref[ATT_GROUP * j + g], snk)

        def block(i, carry, j=j, kc=kc, vc=vc, snk=snk):
            q0 = pl.multiple_of(i * qb, qb)
            qh = jnp.concatenate([q_ref[ATT_GROUP * j + g, pl.ds(q0, qb), :] for g in range(ATT_GROUP)],
                                 axis=0)
            s_ctx = jnp.dot(qh, kc, preferred_element_type=F32)
            m = jnp.maximum(jnp.max(s_ctx, axis=-1, keepdims=True), snk)
            if band:
                k0 = pl.multiple_of(jnp.clip(q0 - WINDOW, 0, seq_len - band), qb)
                var = jnp.where(i == 0, 0, jnp.where(i == n_q - 1, 2, 1))
                s_loc = jnp.dot(qh, k_ref[j, :, pl.ds(k0, band)], preferred_element_type=F32) + bias_ref[var]
                m = jnp.maximum(m, jnp.max(s_loc, axis=-1, keepdims=True))
            p_ctx = jnp.exp(s_ctx - m)
            den = jnp.sum(p_ctx, axis=-1, keepdims=True) + jnp.exp(snk - m)
            o = jnp.dot(p_ctx.astype(BF16), vc, preferred_element_type=F32)
            if band:
                p_loc = jnp.exp(s_loc - m)
                den = den + jnp.sum(p_loc, axis=-1, keepdims=True)
                o = o + jnp.dot(p_loc.astype(BF16), v_ref[j, pl.ds(k0, band), :], preferred_element_type=F32)
            o = o / den
            for g in range(ATT_GROUP):
                c0 = (ATT_GROUP * j + g) * HEAD_DIM
                o_ref[pl.ds(q0, qb), c0:c0 + HEAD_DIM] = o[g * qb:(g + 1) * qb, :]
            return carry

        lax.fori_loop(0, n_q, block, 0, unroll=2)


def _attention(sinks, q, k, v, n_batch, seq_len, ctx_len, latent):
    ctx_blk0 = n_batch * seq_len // ctx_len
    kc_spec = pl.BlockSpec((ATT_KV_HEADS, HEAD_DIM, ctx_len), lambda b: (0, 0, ctx_blk0 + b))
    vc_spec = pl.BlockSpec((ATT_KV_HEADS, ctx_len, HEAD_DIM), lambda b: (0, ctx_blk0 + b, 0))
    smem = pl.BlockSpec(memory_space=pltpu.SMEM)
    scratch = []
    if latent:
        rows = seq_len
        band = ATT_BLOCK + 2 * WINDOW
        in_specs = [smem, pl.BlockSpec((ATT_HEADS, rows, HEAD_DIM), lambda b: (0, b, 0)),
                    pl.BlockSpec((ATT_KV_HEADS, HEAD_DIM, rows), lambda b: (0, 0, b)),
                    pl.BlockSpec((ATT_KV_HEADS, rows, HEAD_DIM), lambda b: (0, b, 0)), kc_spec, vc_spec]
        args = (sinks, q, k, v, k, v)
        scratch = [pltpu.VMEM((3, ATT_GROUP * ATT_BLOCK, band), F32)]
    else:
        rows = ctx_len
        band = 0
        in_specs = [smem, pl.BlockSpec((ATT_HEADS, rows, HEAD_DIM), lambda b: (0, ctx_blk0 + b, 0)),
                    kc_spec, vc_spec]
        args = (sinks, q, k, v)
    return pl.pallas_call(
        functools.partial(_attn_kernel, n_q=rows // ATT_BLOCK, band=band),
        grid=(n_batch,),
        in_specs=in_specs,
        out_specs=pl.BlockSpec((rows, ATT_WIDTH), lambda b: (b, 0)),
        out_shape=jax.ShapeDtypeStruct((n_batch * rows, ATT_WIDTH), F32),
        scratch_shapes=scratch,
        compiler_params=_cparams(1),
        name="attn_latent" if latent else "attn_ctx",
    )(*args)


def _merge_kernel(*refs, n_lat_tiles, with_ctx):
    if with_ctx:
        (attl_ref, attc_ref, syl_ref, syc_ref, hyl_ref, hyc_ref, z_ref, x_ref, mod_ref,
         ga_ref, gs_ref, gh_ref, bd_ref, w_ref, o_ref) = refs
        is_lat = pl.program_id(0) < n_lat_tiles
        att = jnp.where(is_lat, attl_ref[...], attc_ref[...])
        sy = jnp.where(is_lat, syl_ref[...], syc_ref[...])
        hy_ctx = jnp.concatenate([hyc_ref[k] for k in range(hyc_ref.shape[0])], axis=1)
        hy_t = jnp.where(is_lat, hyl_ref[...], hy_ctx)
    else:
        attl_ref, syl_ref, hyl_ref, z_ref, x_ref, mod_ref, ga_ref, gs_ref, gh_ref, bd_ref, w_ref, o_ref = refs
        att = attl_ref[...]
        sy = syl_ref[...]
        hy_t = hyl_ref[...]
    a = att * lax.rsqrt(jnp.mean(att * att, axis=-1, keepdims=True) + EPS) * ga_ref[...]
    s = sy * _silu(z_ref[...])
    s = s * lax.rsqrt(jnp.mean(s * s, axis=-1, keepdims=True) + EPS) * gs_ref[...]
    hy = hy_t.T
    hn = hy * lax.rsqrt(_segsum(hy * hy, bd_ref[...]) * (1.0 / (HYENA_WIDTH // HYENA_GROUPS)) + EPS) * gh_ref[...]
    y = jnp.dot(a.astype(BF16), w_ref[0:ATT_WIDTH, :], preferred_element_type=F32)
    y = y + jnp.dot(s.astype(BF16), w_ref[ATT_WIDTH:ATT_WIDTH + SSD_WIDTH, :], preferred_element_type=F32)
    y = y + jnp.dot(hn.astype(BF16), w_ref[ATT_WIDTH + SSD_WIDTH:, :], preferred_element_type=F32)
    o_ref[...] = x_ref[...] + mod_ref[0, 2:3, :] * y


def _merge(att, sy, hy, z, xa, mods, ga, gs, gh, bd, w_out, n_batch, seq_len):
    d = xa.shape[1]
    tm = TOKEN_TILE
    nl = att[0].shape[0] // tm
    with_ctx = att[1] is not None
    t = att[0].shape[0] + (att[1].shape[0] if with_ctx else 0)
    row = lambda w: pl.BlockSpec((tm, w), lambda i: (i, 0))
    lat_row = lambda w: pl.BlockSpec((tm, w), lambda i: (jnp.minimum(i, nl - 1), 0))
    ctx_row = lambda w: pl.BlockSpec((tm, w), lambda i: (jnp.maximum(i - nl, 0), 0))
    const = lambda a: pl.BlockSpec(a.shape, lambda i: (0,) * a.ndim)
    tpb = seq_len // tm
    lat_tile = lambda i: jnp.minimum(i, nl - 1)
    hy_lat = pl.BlockSpec((None, HYENA_WIDTH, tm), lambda i: (lat_tile(i) // tpb, 0, lat_tile(i) % tpb))
    if with_ctx:
        ctx_len = hy[1].shape[2]
        streams = [att[0], att[1], sy[0], sy[1], hy[0], hy[1]]
        specs = [lat_row(ATT_WIDTH), ctx_row(ATT_WIDTH), lat_row(SSD_WIDTH), ctx_row(SSD_WIDTH), hy_lat,
                 pl.BlockSpec((tm // ctx_len, HYENA_WIDTH, ctx_len), lambda i: (jnp.maximum(i - nl, 0), 0, 0))]
    else:
        streams = [att[0], sy[0], hy[0]]
        specs = [row(ATT_WIDTH), row(SSD_WIDTH), hy_lat]
    return pl.pallas_call(
        functools.partial(_merge_kernel, n_lat_tiles=nl, with_ctx=with_ctx),
        grid=(t // tm,),
        in_specs=specs + [row(SSD_WIDTH), row(d),
                          pl.BlockSpec((1, 6, d), _mod_index(seq_len // tm, n_batch)),
                          const(ga), const(gs), const(gh), const(bd), const(w_out)],
        out_specs=row(d),
        out_shape=jax.ShapeDtypeStruct((t, d), F32),
        compiler_params=_cparams(1),
        name="merge_outproj",
    )(*streams, z, xa, mods, ga, gs, gh, bd, w_out)


def _swiglu_accumulate(h, wg_ref, wu_ref, wd_ref, acc_ref):
    for c in range(N_FFN_CHUNKS):
        cols = slice(c * FFN_CHUNK, (c + 1) * FFN_CHUNK)
        g = jnp.dot(h, wg_ref[:, cols], preferred_element_type=F32)
        u = jnp.dot(h, wu_ref[:, cols], preferred_element_type=F32)
        a = (_silu(g) * u).astype(BF16)
        part = jnp.dot(a, wd_ref[cols, :], preferred_element_type=F32)
        if c == 0:
            acc_ref[...] = part
        else:
            acc_ref[...] += part


def _ffn_kernel(x_ref, mod_ref, g2_ref, wg_ref, wu_ref, wd_ref, o_ref, acc_ref):
    x = x_ref[...]
    h = _modnorm(x, g2_ref[...], mod_ref[0, 4:5, :], mod_ref[0, 3:4, :]).astype(BF16)
    _swiglu_accumulate(h, wg_ref, wu_ref, wd_ref, acc_ref)
    o_ref[...] = x + mod_ref[0, 5:6, :] * acc_ref[...]


def _ffn(xa, mods, g2, wg, wu, wd, n_batch, seq_len):
    t, d = xa.shape
    tm = TOKEN_TILE
    row = pl.BlockSpec((tm, d), lambda i: (i, 0))
    resident = lambda a: pl.BlockSpec(a.shape, lambda i: (0,) * a.ndim, pipeline_mode=pl.Buffered(1))
    return pl.pallas_call(
        _ffn_kernel,
        grid=(t // tm,),
        in_specs=[row, pl.BlockSpec((1, 6, d), _mod_index(seq_len // tm, n_batch)),
                  pl.BlockSpec(g2.shape, lambda i: (0, 0)), resident(wg), resident(wu), resident(wd)],
        out_specs=row,
        out_shape=jax.ShapeDtypeStruct((t, d), F32),
        scratch_shapes=[pltpu.VMEM((tm, d), F32)],
        compiler_params=_cparams(1),
        name="ffn",
    )(xa, mods, g2, wg, wu, wd)


def _router_kernel(x_ref, mod_ref, g2_ref, r_ref, h_ref, idx_ref, wt_ref):
    h = _modnorm(x_ref[...], g2_ref[...], mod_ref[0, 4:5, :], mod_ref[0, 3:4, :])
    h_ref[...] = h
    logits = jnp.dot(h, r_ref[...], precision=lax.Precision.HIGHEST, preferred_element_type=F32)
    lane = lax.broadcasted_iota(jnp.int32, logits.shape, 1)
    neg = -jnp.inf
    l1 = jnp.where(lane < N_EXPERTS, logits, neg)
    m1 = jnp.max(l1, axis=-1, keepdims=True)
    i1 = jnp.min(jnp.where(l1 == m1, lane, LANES), axis=-1, keepdims=True)
    l2 = jnp.where(lane == i1, neg, l1)
    m2 = jnp.max(l2, axis=-1, keepdims=True)
    i2 = jnp.min(jnp.where(l2 == m2, lane, LANES), axis=-1, keepdims=True)
    e = jnp.exp(m2 - m1)
    w1 = 1.0 / (1.0 + e)
    w2 = e / (1.0 + e)
    idx_ref[...] = jnp.where(lane == 0, i1, jnp.where(lane == 1, i2, 0))
    wt_ref[...] = jnp.where(lane == 0, w1, jnp.where(lane == 1, w2, 0.0))


def _router(xa, mods, g2, r_pad, n_rows, n_batch, seq_len):
    d = xa.shape[1]
    tm = TOKEN_TILE
    row = lambda w: pl.BlockSpec((tm, w), lambda i: (i, 0))
    return pl.pallas_call(
        _router_kernel,
        grid=(n_rows // tm,),
        in_specs=[row(d), pl.BlockSpec((1, 6, d), _mod_index(seq_len // tm, n_batch)),
                  pl.BlockSpec(g2.shape, lambda i: (0, 0)), pl.BlockSpec(r_pad.shape, lambda i: (0, 0))],
        out_specs=[row(d), row(LANES), row(LANES)],
        out_shape=[jax.ShapeDtypeStruct((n_rows, d), F32),
                   jax.ShapeDtypeStruct((n_rows, LANES), jnp.int32),
                   jax.ShapeDtypeStruct((n_rows, LANES), F32)],
        compiler_params=_cparams(1),
        name="moe_router",
    )(xa, mods, g2, r_pad)


def _row_copy(src, src_row, dst, dst_row, sem):
    return pltpu.make_async_copy(src.at[pl.ds(src_row, 1), :], dst.at[pl.ds(dst_row, 1), :], sem)


DMA_ISSUE_UNROLL = 8


def _idx_copy(dest_hbm, dest_smem, sem_idx, tile, slot):
    n = dest_hbm.shape[1]
    half = dest_smem.at[pl.ds(pl.multiple_of(slot * n, n), n)]
    return pltpu.make_async_copy(dest_hbm.at[tile], half, sem_idx.at[slot])


def _dispatch_kernel(pad_tile_ref, dest_hbm, h_ref, xs_out, dest_smem, zeros, sem_idx, sem_rows, sem_zero):
    i = pl.program_id(0)
    n = pl.num_programs(0)
    tm = h_ref.shape[0]
    slot = i % 2

    @pl.when(i == 0)
    def _():
        zeros[...] = jnp.zeros_like(zeros)

        def zero_copy(e):
            return pltpu.make_async_copy(zeros, xs_out.at[pl.ds(pl.multiple_of(pad_tile_ref[e], tm), tm), :],
                                         sem_zero)

        for e in range(2 * N_EXPERTS):
            @pl.when(pad_tile_ref[e] >= 0)
            def _(e=e):
                zero_copy(e).start()
        for e in range(2 * N_EXPERTS):
            @pl.when(pad_tile_ref[e] >= 0)
            def _(e=e):
                zero_copy(e).wait()
        _idx_copy(dest_hbm, dest_smem, sem_idx, 0, 0).start()

    @pl.when(i + 1 < n)
    def _():
        _idx_copy(dest_hbm, dest_smem, sem_idx, i + 1, 1 - slot).start()

    _idx_copy(dest_hbm, dest_smem, sem_idx, i, slot).wait()

    base = slot * (2 * tm)

    def issue(r, carry):
        _row_copy(h_ref, r, xs_out, dest_smem[base + 2 * r], sem_rows).start()
        _row_copy(h_ref, r, xs_out, dest_smem[base + 2 * r + 1], sem_rows).start()
        return carry

    lax.fori_loop(0, tm, issue, 0, unroll=DMA_ISSUE_UNROLL)
    for _ in range(2):
        pltpu.make_async_copy(h_ref, xs_out.at[pl.ds(0, tm), :], sem_rows).wait()


def _dispatch(pad_tile, dest, h, n_slots):
    n_rows, d = h.shape
    tm = TOKEN_TILE
    assert tm == EXPERT_TILE
    return pl.pallas_call(
        _dispatch_kernel,
        grid_spec=pltpu.PrefetchScalarGridSpec(
            num_scalar_prefetch=1,
            grid=(n_rows // tm,),
            in_specs=[pl.BlockSpec(memory_space=pl.ANY), pl.BlockSpec((tm, d), lambda i, pt: (i, 0))],
            out_specs=pl.BlockSpec(memory_space=pl.ANY),
            scratch_shapes=[pltpu.SMEM((4 * tm,), jnp.int32), pltpu.VMEM((tm, d), F32),
                            pltpu.SemaphoreType.DMA((2,)), pltpu.SemaphoreType.DMA(()),
                            pltpu.SemaphoreType.DMA(())]),
        out_shape=jax.ShapeDtypeStruct((n_slots, d), F32),
        compiler_params=_cparams(1),
        name="moe_dispatch",
    )(pad_tile, dest.reshape(n_rows // tm, 2 * tm), h)


def _expert_kernel(te_ref, nused_ref, xs_ref, wg_ref, wu_ref, wd_ref, o_ref, acc_ref):
    del te_ref
    live = pl.program_id(0) < nused_ref[0]

    @pl.when(live)
    def _():
        _swiglu_accumulate(xs_ref[...].astype(BF16), wg_ref, wu_ref, wd_ref, acc_ref)
        o_ref[...] = acc_ref[...]

    @pl.when(jnp.logical_not(live))
    def _():
        o_ref[...] = jnp.zeros_like(o_ref)


def _experts(tile_expert, n_used, xs, wg, wu, wd):
    s, d = xs.shape
    tm = EXPERT_TILE
    row = pl.BlockSpec((tm, d), lambda i, te, nu: (i, 0))
    xs_row = pl.BlockSpec((tm, d), lambda i, te, nu: (jnp.minimum(i, nu[0] - 1), 0))
    wspec = lambda a: pl.BlockSpec((None,) + a.shape[1:], lambda i, te, nu: (te[i], 0, 0))
    return pl.pallas_call(
        _expert_kernel,
        grid_spec=pltpu.PrefetchScalarGridSpec(
            num_scalar_prefetch=2,
            grid=(s // tm,),
            in_specs=[xs_row, wspec(wg), wspec(wu), wspec(wd)],
            out_specs=row,
            scratch_shapes=[pltpu.VMEM((tm, d), F32)]),
        out_shape=jax.ShapeDtypeStruct((s, d), F32),
        compiler_params=_cparams(1),
        name="moe_experts",
    )(tile_expert, n_used, xs, wg, wu, wd)


def _combine_kernel(dest_hbm, eo_hbm, x_ref, wt_ref, mod_ref, o_ref, dest_smem, buf, sem_idx, sem_rows):
    i = pl.program_id(0)
    n = pl.num_programs(0)
    tm = x_ref.shape[0]
    slot = i % 2

    def gather(s):
        base = s * (2 * tm)

        def issue(r, carry):
            _row_copy(eo_hbm, dest_smem[base + 2 * r], buf.at[s, 0], r, sem_rows.at[s]).start()
            _row_copy(eo_hbm, dest_smem[base + 2 * r + 1], buf.at[s, 1], r, sem_rows.at[s]).start()
            return carry

        lax.fori_loop(0, tm, issue, 0, unroll=DMA_ISSUE_UNROLL)

    @pl.when(i == 0)
    def _():
        first = _idx_copy(dest_hbm, dest_smem, sem_idx, 0, 0)
        first.start()
        first.wait()
        gather(0)

        @pl.when(n > 1)
        def _():
            _idx_copy(dest_hbm, dest_smem, sem_idx, 1, 1).start()

    @pl.when(i + 1 < n)
    def _():
        _idx_copy(dest_hbm, dest_smem, sem_idx, i + 1, 1 - slot).wait()

        @pl.when(i + 2 < n)
        def _():
            _idx_copy(dest_hbm, dest_smem, sem_idx, i + 2, slot).start()

        gather(1 - slot)

    for k in range(2):
        pltpu.make_async_copy(eo_hbm.at[pl.ds(0, tm), :], buf.at[slot, k], sem_rows.at[slot]).wait()
    wt = wt_ref[...]
    y = wt[:, 0:1] * buf[slot, 0] + wt[:, 1:2] * buf[slot, 1]
    o_ref[...] = x_ref[...] + mod_ref[0, 5:6, :] * y


def _combine(dest, eo, xa, wts, mods, n_rows, n_batch, seq_len):
    d = xa.shape[1]
    tm = TOKEN_TILE
    row = lambda w: pl.BlockSpec((tm, w), lambda i: (i, 0))
    return pl.pallas_call(
        _combine_kernel,
        grid=(n_rows // tm,),
        in_specs=[pl.BlockSpec(memory_space=pl.ANY), pl.BlockSpec(memory_space=pl.ANY), row(d), row(LANES),
                  pl.BlockSpec((1, 6, d), _mod_index(seq_len // tm, n_batch))],
        out_specs=row(d),
        out_shape=jax.ShapeDtypeStruct((n_rows, d), F32),
        scratch_shapes=[pltpu.SMEM((4 * tm,), jnp.int32), pltpu.VMEM((2, 2, tm, d), F32),
                        pltpu.SemaphoreType.DMA((2,)), pltpu.SemaphoreType.DMA((2,))],
        compiler_params=_cparams(1),
        name="moe_combine",
    )(dest.reshape(n_rows // tm, 2 * tm), eo, xa, wts, mods)


def _moe(xa, mods, g2, r_pad, wg, wu, wd, n_rows, n_batch, seq_len):
    h, idx, wts = _router(xa, mods, g2, r_pad, n_rows, n_batch, seq_len)
    tm = EXPERT_TILE
    e_flat = idx[:, :2].reshape(-1)
    onehot = (e_flat[:, None] == jnp.arange(N_EXPERTS, dtype=jnp.int32)[None, :]).astype(jnp.int32)
    csum = jnp.cumsum(onehot, axis=0)
    counts = csum[-1]
    rank = jnp.sum(onehot * csum, axis=1) - 1
    padded = ((counts + tm - 1) // tm) * tm
    ends = jnp.cumsum(padded)
    starts = ends - padded
    dest = (jnp.sum(onehot * starts[None, :], axis=1) + rank).astype(jnp.int32)
    n_slots = 2 * n_rows + N_EXPERTS * tm
    tile_start = jnp.arange(n_slots // tm, dtype=jnp.int32) * tm
    tile_expert = jnp.minimum(jnp.sum((tile_start[:, None] >= ends[None, :]).astype(jnp.int32), axis=1),
                              N_EXPERTS - 1).astype(jnp.int32)
    n_used = (ends[-1:] // tm).astype(jnp.int32)
    tail = ends[-1] + jnp.arange(N_EXPERTS, dtype=ends.dtype) * tm
    pad_tile = jnp.concatenate([jnp.where(padded > 0, ends - tm, -1),
                                jnp.where(tail < n_slots, tail, -1)]).astype(jnp.int32)
    xs = _dispatch(pad_tile, dest, h, n_slots)
    eo = _experts(tile_expert, n_used, xs, wg, wu, wd)
    return _combine(dest, eo, xa, wts, mods, n_rows, n_batch, seq_len)


def _softplus(v):
    return jnp.maximum(v, 0.0) + jnp.log1p(jnp.exp(-jnp.abs(v)))


def _ssd_kernel(xl_ref, dl_ref, xc_ref, dc_ref, cw_ref, cb_ref, dtb_ref, alogc_ref, dsk_ref, *rest, want_ctx):
    if want_ctx:
        yl_ref, yc_ref, xs_l, dtt_l, xs_c, dtt_c, st_ref = rest
    else:
        yl_ref, xs_l, dtt_l, xs_c, dtt_c, st_ref = rest
        yc_ref = None
    ck = SSD_CHUNK
    hp = SSD_HEAD_DIM
    ns = SSD_STATE
    n_col = 2 * SSD_HEADS
    hi = lax.Precision.HIGHEST
    row = lax.broadcasted_iota(jnp.int32, (ck, 1), 0)
    li = lax.broadcasted_iota(jnp.int32, (ck, ck), 0)
    si = lax.broadcasted_iota(jnp.int32, (ck, ck), 1)
    masks = (si <= li, si >= li)
    tris = (masks[0].astype(F32), masks[1].astype(F32))
    a_col = -jnp.exp(alogc_ref[...])
    dskip = dsk_ref[...]

    def prep(raw_ref, dtraw_ref, xs_s, dtt_s, y_ref):
        n = raw_ref.shape[0]
        nk = n // ck

        def body(k, carry):
            r0 = pl.multiple_of(k * ck, ck)
            a = raw_ref[pl.ds(r0, ck), :]
            top = raw_ref[pl.ds(pl.multiple_of(jnp.maximum(r0 - 8, 0), 8), 8), :][7:8, :]
            bot = raw_ref[pl.ds(pl.multiple_of(jnp.minimum(r0 + ck, n - 8), 8), 8), :][0:1, :]
            top = jnp.where(k > 0, top, 0.0)
            bot = jnp.where(k < nk - 1, bot, 0.0)
            prev = jnp.where(row == 0, top, pltpu.roll(a, 1, 0))
            nxt = jnp.where(row == ck - 1, bot, pltpu.roll(a, ck - 1, 0))
            xs = _silu(prev * cw_ref[0:1, :] + a * cw_ref[1:2, :] + nxt * cw_ref[2:3, :] + cb_ref[...])
            xs_s[pl.ds(r0, ck), :] = xs
            dt = _softplus(dtraw_ref[pl.ds(r0, ck), :] + dtb_ref[...])
            dtt_s[:, pl.ds(r0, ck)] = dt.T[0:n_col, :]
            if y_ref is not None:
                y_ref[pl.ds(r0, ck), :] = xs[:, 0:SSD_WIDTH] * dskip
            return carry

        lax.fori_loop(0, nk, body, 0)

    def run(xs_s, dtt_s, y_ref):
        nk = xs_s.shape[0] // ck

        def one(kk, dr):
            r0 = pl.multiple_of(kk * ck, ck)
            xc = xs_s[pl.ds(r0, ck), :]
            dtr = dtt_s[:, pl.ds(r0, ck)]
            dta_r = dtr * a_col
            la_r = jnp.dot(dta_r, tris[1 - dr], precision=hi, preferred_element_type=F32)
            la_end = la_r[:, ck - 1:ck] if dr == 0 else la_r[:, 0:1]
            if y_ref is not None:
                la_c = jnp.concatenate([la_r, jnp.zeros((ck - n_col, ck), F32)], axis=0).T
            ys = []
            for g in range(SSD_GROUPS):
                b_t = xc[:, SSD_WIDTH + g * ns:SSD_WIDTH + (g + 1) * ns].T
                c0 = SSD_WIDTH + SSD_GROUPS * ns + g * ns
                c_g = xc[:, c0:c0 + ns].astype(BF16)
                if y_ref is not None:
                    scores = jnp.dot(c_g, b_t.astype(BF16), preferred_element_type=F32)
                for hh in range(SSD_HEADS // SSD_GROUPS):
                    h = g * (SSD_HEADS // SSD_GROUPS) + hh
                    col = dr * SSD_HEADS + h
                    xh = xc[:, h * hp:(h + 1) * hp].astype(BF16)
                    dt_row = dtr[col:col + 1, :]
                    le = la_end[col:col + 1, :]
                    st = st_ref[col]
                    if y_ref is not None:
                        la_col = la_c[:, col:col + 1]
                        decay = jnp.exp(jnp.where(masks[dr], la_col - la_r[col:col + 1, :], -jnp.inf))
                        y = jnp.dot((scores * decay * dt_row).astype(BF16), xh, preferred_element_type=F32)
                        y = y + jnp.dot(c_g, st.astype(BF16), preferred_element_type=F32) * jnp.exp(la_col)
                        ys.append(y)
                    bw = (b_t * (dt_row * jnp.exp(le - la_r[col:col + 1, :]))).astype(BF16)
                    st_ref[col] = st * jnp.exp(le) + jnp.dot(bw, xh, preferred_element_type=F32)
            if y_ref is not None:
                y_ref[pl.ds(r0, ck), :] += jnp.concatenate(ys, axis=1)

        def body(k, carry):
            one(k, 0)
            one(nk - 1 - k, 1)
            return carry

        lax.fori_loop(0, nk, body, 0)

    prep(xl_ref, dl_ref, xs_l, dtt_l, yl_ref)
    prep(xc_ref, dc_ref, xs_c, dtt_c, yc_ref)
    st_ref[...] = jnp.zeros_like(st_ref)
    run(xs_c, dtt_c, yc_ref)
    run(xs_l, dtt_l, yl_ref)


def _ssd(xbc, dtp, conv_w, conv_b, dt_bias, a_log, d_skip, n_batch, seq_len, ctx_len, want_ctx):
    ctx0 = n_batch * seq_len // ctx_len
    pad = lambda v: jnp.pad(v.reshape(1, -1), ((0, 0), (0, LANES - v.size)))
    lat = lambda w: pl.BlockSpec((seq_len, w), lambda b: (b, 0))
    ctx = lambda w: pl.BlockSpec((ctx_len, w), lambda b: (ctx0 + b, 0))
    const = lambda a: pl.BlockSpec(a.shape, lambda b: (0,) * a.ndim)
    consts = (conv_w, conv_b.reshape(1, -1), pad(dt_bias), a_log.reshape(-1, 1),
              jnp.repeat(d_skip, SSD_HEAD_DIM).reshape(1, -1))
    out_specs = [lat(SSD_WIDTH)]
    out_shape = [jax.ShapeDtypeStruct((n_batch * seq_len, SSD_WIDTH), F32)]
    if want_ctx:
        out_specs.append(pl.BlockSpec((ctx_len, SSD_WIDTH), lambda b: (b, 0)))
        out_shape.append(jax.ShapeDtypeStruct((n_batch * ctx_len, SSD_WIDTH), F32))
    return pl.pallas_call(
        functools.partial(_ssd_kernel, want_ctx=want_ctx),
        grid=(n_batch,),
        in_specs=[lat(SSD_XBC_COLS), lat(LANES), ctx(SSD_XBC_COLS), ctx(LANES)] + [const(a) for a in consts],
        out_specs=out_specs,
        out_shape=out_shape,
        scratch_shapes=[pltpu.VMEM((seq_len, SSD_XBC_COLS), F32), pltpu.VMEM((2 * SSD_HEADS, seq_len), F32),
                        pltpu.VMEM((ctx_len, SSD_XBC_COLS), F32), pltpu.VMEM((2 * SSD_HEADS, ctx_len), F32),
                        pltpu.VMEM((2 * SSD_HEADS, SSD_STATE, SSD_HEAD_DIM), F32)],
        compiler_params=_cparams(1),
        name="ssd",
    )(xbc, dtp, xbc, dtp, *consts)


HY_BLOCK = 256
HY_CH_STEP = 8


def _hyena_tables(seq_len):
    nj = 2 * seq_len
    lag = jnp.arange(nj, dtype=jnp.int32) - seq_len
    dist = jnp.abs(lag)
    pos = jnp.minimum(dist, seq_len - 1)
    t = jnp.linspace(0.0, 1.0, seq_len, dtype=F32)[pos]
    w = ((2.0 * math.pi / seq_len) * jnp.arange(seq_len, dtype=F32))[pos]
    bands = (HYENA_POS_DIM - 1) // 2
    freqs = jnp.linspace(1e-4, bands - 1, bands, dtype=F32)[None, :]
    z = jnp.concatenate([t[:, None], jnp.cos(freqs * w[:, None]), -jnp.sin(freqs * w[:, None])], axis=-1)
    zt = jnp.pad(z.T, ((0, (-HYENA_POS_DIM) % 8), (0, 0)))
    deltas = jnp.abs(jnp.linspace(math.log(HYENA_DECAY_TARGET) / HYENA_SLOW_DECAY,
                                  math.log(HYENA_DECAY_TARGET) / HYENA_FAST_DECAY, HYENA_WIDTH, dtype=F32))
    dec = jnp.exp(-t[None, :] * deltas[:, None]) * (dist < seq_len).astype(F32)[None, :]
    fwd = (lag >= 0).astype(F32)[None, :]
    return zt, dec, fwd


def _hyfilt_kernel(zt_ref, dec_ref, fwd_ref, w1_ref, b1_ref, f1_ref, w2_ref, b2_ref, f2_ref, w3_ref, o_ref):
    hi = lax.Precision.HIGHEST
    h = jnp.sin(f1_ref[...] * (jnp.dot(w1_ref[...], zt_ref[...], precision=hi, preferred_element_type=F32)
                               + b1_ref[...]))
    h = jnp.sin(f2_ref[...] * (jnp.dot(w2_ref[...], h, precision=hi, preferred_element_type=F32) + b2_ref[...]))
    hw = jnp.dot(w3_ref[...], h, precision=hi, preferred_element_type=F32)
    fwd = fwd_ref[...] > 0.5
    dec = dec_ref[...]
    nw = HYENA_WIDTH
    for o in range(HYENA_ORDER):
        o_ref[o] = jnp.where(fwd, hw[o * nw:(o + 1) * nw], hw[(HYENA_ORDER + o) * nw:(HYENA_ORDER + o + 1) * nw]) * dec


def _hyena_filters(tables, w1, b1, f1, w2, b2, f2, w3):
    zt, dec, fwd = tables
    nj = zt.shape[1]
    tj = 512
    col = lambda v: v.reshape(-1, 1)
    w1t = jnp.pad(w1.T, ((0, 0), (0, zt.shape[0] - w1.shape[0])))
    consts = (w1t, col(b1), col(f1), w2.T, col(b2), col(f2), w3.T)
    lanes = lambda a: pl.BlockSpec((a.shape[0], tj), lambda j: (0, j))
    const = lambda a: pl.BlockSpec(a.shape, lambda j: (0, 0))
    return pl.pallas_call(
        _hyfilt_kernel,
        grid=(nj // tj,),
        in_specs=[lanes(zt), lanes(dec), lanes(fwd)] + [const(a) for a in consts],
        out_specs=pl.BlockSpec((HYENA_ORDER, HYENA_WIDTH, tj), lambda j: (0, 0, j)),
        out_shape=jax.ShapeDtypeStruct((HYENA_ORDER, HYENA_WIDTH, nj), F32),
        compiler_params=_cparams(1),
        name="hyena_filters",
    )(zt, dec, fwd, *consts)


def _hyconv_kernel(cw_ref, cb_ref, hb_ref, v_ref, x1_ref, x2_ref, kf_ref, o_ref):
    n_b, n_ch, seq_len = v_ref.shape
    nb = seq_len // HY_BLOCK
    blk = HY_BLOCK
    c_base = pl.program_id(0) * n_ch
    lane = lax.broadcasted_iota(jnp.int32, (1, seq_len), 1)

    def sconv(x, ch):
        prev = jnp.where(lane == 0, 0.0, pltpu.roll(x, 1, 1))
        nxt = jnp.where(lane == seq_len - 1, 0.0, pltpu.roll(x, seq_len - 1, 1))
        return prev * cw_ref[0, ch] + x * cw_ref[1, ch] + nxt * cw_ref[2, ch] + cb_ref[ch]

    def long_conv(vals, kf_row):
        skew = pltpu.roll(jnp.broadcast_to(kf_row, (blk, 2 * seq_len)), 0, 1, stride=1, stride_axis=0)
        vb = vals.astype(BF16)
        acc = [None] * nb
        for d in range(-(nb - 1), nb):
            tt = skew[:, seq_len + d * blk:seq_len + (d + 1) * blk].astype(BF16)
            sis = list(range(max(0, -d), min(nb, nb - d)))
            lhs = [vb[:, s * blk:(s + 1) * blk] for s in sis]
            lhs = lhs[0] if len(lhs) == 1 else jnp.concatenate(lhs, axis=0)
            out = jnp.dot(lhs, tt, preferred_element_type=F32)
            for idx, s in enumerate(sis):
                piece = out[idx * n_b:(idx + 1) * n_b]
                acc[s + d] = piece if acc[s + d] is None else acc[s + d] + piece
        return acc[0] if nb == 1 else jnp.concatenate(acc, axis=1)

    def channel(cc, carry):
        ch = c_base + cc
        v = sconv(v_ref[:, cc, :], ch)
        x1 = sconv(x1_ref[:, cc, :], HYENA_WIDTH + ch)
        x2 = sconv(x2_ref[:, cc, :], 2 * HYENA_WIDTH + ch)
        z = x1 * (long_conv(v, kf_ref[0, pl.ds(cc, 1), :]) + v * hb_ref[0, ch])
        o_ref[:, cc, :] = x2 * (long_conv(z, kf_ref[1, pl.ds(cc, 1), :]) + z * hb_ref[1, ch])
        return carry

    lax.fori_loop(0, n_ch, channel, 0)


def _hyena_conv(hy, kf, conv_w, conv_b, hy_bias):
    n_batch, _, seq_len = hy.shape
    cs = HY_CH_STEP
    nw = HYENA_WIDTH
    stream = lambda k: pl.BlockSpec((n_batch, cs, seq_len), lambda c: (0, k * (nw // cs) + c, 0))
    smem = pl.BlockSpec(memory_space=pltpu.SMEM)
    return pl.pallas_call(
        _hyconv_kernel,
        grid=(nw // cs,),
        in_specs=[smem, smem, smem, stream(0), stream(1), stream(2),
                  pl.BlockSpec((HYENA_ORDER, cs, 2 * seq_len), lambda c: (0, c, 0))],
        out_specs=pl.BlockSpec((n_batch, cs, seq_len), lambda c: (0, c, 0)),
        out_shape=jax.ShapeDtypeStruct((n_batch, nw, seq_len), F32),
        compiler_params=_cparams(1),
        name="hyena_conv",
    )(conv_w, conv_b, hy_bias, hy, hy, hy, kf)


def _rope_tables(seq_len, extra):
    rows = seq_len // GRID_W
    row = jnp.repeat(jnp.arange(rows, dtype=F32), GRID_W)
    col = jnp.tile(jnp.arange(GRID_W, dtype=F32), rows)
    inv = ROPE_THETA ** (-jnp.arange(0, ROPE_AXIS_DIM, 2, dtype=F32) / ROPE_AXIS_DIM)
    ang = jnp.stack([row[:, None] * inv, col[:, None] * inv], axis=1)
    cos = jnp.cos(ang)
    sin = jnp.sin(ang)
    cos_h = jnp.concatenate([cos, cos], axis=-1).reshape(seq_len, HEAD_DIM)
    sin_h = jnp.concatenate([-sin, sin], axis=-1).reshape(seq_len, HEAD_DIM)
    cos_t = jnp.concatenate([jnp.tile(cos_h, (1, LANES // HEAD_DIM)), jnp.ones((extra, LANES), F32)], axis=0)
    sin_t = jnp.concatenate([jnp.tile(sin_h, (1, LANES // HEAD_DIM)), jnp.zeros((extra, LANES), F32)], axis=0)
    return cos_t, sin_t


def _block_diag_ones(n, seg):
    i = jnp.arange(n) // seg
    return (i[:, None] == i[None, :]).astype(BF16)


def kernel(x, c, ctx, c_ctx, w_ada, b_ada, norm1, norm2, w_in, w_out, q_norm, k_norm, att_sinks, att_out_norm, ssd_conv_w, ssd_conv_b, ssd_dt_bias, ssd_a_log, ssd_d, ssd_norm, hy_conv_w, hy_conv_b, hy_w1, hy_b1, hy_f1, hy_w2, hy_b2, hy_f2, hy_w3, hy_bias, hy_out_norm, ffn_w_gate, ffn_w_up, ffn_w_down, moe_router, moe_w_gate, moe_w_up, moe_w_down):
    n_batch, seq_len, d = x.shape
    ctx_len = ctx.shape[1]
    n_lat = n_batch * seq_len
    n_ctx = n_batch * ctx_len
    depth = w_in.shape[0]
    xa = jnp.concatenate([x.reshape(n_lat, d), ctx.reshape(n_ctx, d)], axis=0)

    cc = jnp.concatenate([c, c_ctx[None, :]], axis=0)
    pad_rows = (-cc.shape[0]) % 8
    cc = jnp.pad(cc, ((0, pad_rows), (0, 0)))
    mods_all = _adaln(cc, w_ada, b_ada)[:, :n_batch + 1].reshape(depth, n_batch + 1, 6, d)

    cos_t, sin_t = _rope_tables(seq_len, TOKEN_TILE)
    bd_q = _block_diag_ones(Q_COLS, HEAD_DIM)
    bd_h = _block_diag_ones(HYENA_WIDTH, HYENA_WIDTH // HYENA_GROUPS)
    hy_tab_l = _hyena_tables(seq_len)
    hy_tab_c = _hyena_tables(ctx_len)

    for i in range(depth):
        last = i == depth - 1
        j = i // 2
        mods = mods_all[i]
        wi = w_in[i]
        c_dt = QKV_W + ZX_W
        w_cat = jnp.concatenate([wi[:, :c_dt], wi[:, c_dt + SSD_DT_COLS:], wi[:, c_dt:c_dt + SSD_DT_COLS],
                                 jnp.zeros((d, LANES - SSD_DT_COLS), F32)], axis=1).astype(BF16)
        qg = jnp.tile(q_norm[i], Q_COLS // HEAD_DIM)[None, :]
        kg = jnp.tile(k_norm[i], KV_COLS // HEAD_DIM)[None, :]
        q, k, v, z, xbc, hy_l, hy_c, dtp = _inproj(xa, mods, norm1[i][None, :], w_cat, cos_t, sin_t, qg, kg,
                                                   bd_q, n_batch, seq_len, ctx_len)

        att_l = _attention(att_sinks[i], q, k, v, n_batch, seq_len, ctx_len, True)
        ssd_out = _ssd(xbc, dtp, ssd_conv_w[i], ssd_conv_b[i], ssd_dt_bias[i], ssd_a_log[i], ssd_d[i],
                       n_batch, seq_len, ctx_len, not last)
        filt = (hy_w1[i], hy_b1[i], hy_f1[i], hy_w2[i], hy_b2[i], hy_f2[i], hy_w3[i])
        hyo_l = _hyena_conv(hy_l, _hyena_filters(hy_tab_l, *filt), hy_conv_w[i], hy_conv_b[i], hy_bias[i])
        if last:
            att = (att_l, None)
            sy = (ssd_out[0], None)
            hyo = (hyo_l, None)
            n_rows = n_lat
        else:
            att = (att_l, _attention(att_sinks[i], q, k, v, n_batch, seq_len, ctx_len, False))
            sy = tuple(ssd_out)
            hyo_c = _hyena_conv(hy_c, _hyena_filters(hy_tab_c, *filt), hy_conv_w[i], hy_conv_b[i], hy_bias[i])
            hyo = (hyo_l, hyo_c)
            n_rows = n_lat + n_ctx
        xa = _merge(att, sy, hyo, z, xa, mods, att_out_norm[i][None, :], ssd_norm[i][None, :],
                    hy_out_norm[i][None, :], bd_h, w_out[i].astype(BF16), n_batch, seq_len)

        g2 = norm2[i][None, :]
        if i % 2 == 0:
            xa = _ffn(xa, mods, g2, ffn_w_gate[j].astype(BF16), ffn_w_up[j].astype(BF16),
                      ffn_w_down[j].astype(BF16), n_batch, seq_len)
        else:
            r_pad = jnp.pad(moe_router[j], ((0, 0), (0, LANES - N_EXPERTS)))
            xa = _moe(xa, mods, g2, r_pad, moe_w_gate[j].astype(BF16), moe_w_up[j].astype(BF16),
                      moe_w_down[j].astype(BF16), n_rows, n_batch, seq_len)
    return xa[:n_lat].reshape(n_batch, seq_len, d)
```

```python
import functools
import math

import jax
import jax.numpy as jnp
import numpy as np
from jax import lax
from jax.experimental import pallas as pl
from jax.experimental.pallas import tpu as pltpu

F32 = jnp.float32
BF16 = jnp.bfloat16

D_MODEL = 1024
DEPTH = 4
GRID_W = 64
EPS = 1e-6
HEAD_DIM = 64
ATT_WIDTH = 512
ATT_HEADS = 8
ATT_KV_HEADS = 2
ATT_GROUP = 4
WINDOW = 128
ATT_BLOCK = 128
ROPE_THETA = 10000.0
ROPE_AXIS_DIM = 32
SSD_WIDTH = 256
SSD_HEAD_DIM = 64
SSD_HEADS = 4
SSD_STATE = 64
SSD_GROUPS = 2
SSD_CHUNK = 128
HYENA_WIDTH = 256
HYENA_GROUPS = 4
HYENA_ORDER = 2
HYENA_POS_DIM = 33
HYENA_FAST_DECAY = 0.3
HYENA_SLOW_DECAY = 1.5
HYENA_DECAY_TARGET = 1e-2
Q_COLS = 512
KV_COLS = 128
SSD_XBC_COLS = 512
SSD_DT_COLS = 8
HY_COLS = 768
FFN_DIM = 2816
N_EXPERTS = 8
FFN_CHUNK = 256
N_FFN_CHUNKS = FFN_DIM // FFN_CHUNK
LANES = 128
QKV_W = Q_COLS + 2 * KV_COLS
ZX_W = SSD_WIDTH + SSD_XBC_COLS
PROJ_PAD = QKV_W + ZX_W + HY_COLS + LANES
VMEM_LIMIT = 56 * 1024 * 1024
TOKEN_TILE = 512
EXPERT_TILE = 512


def _cparams(n_axes):
    return pltpu.CompilerParams(dimension_semantics=("arbitrary",) * n_axes,
                                vmem_limit_bytes=VMEM_LIMIT)


def _silu(v):
    return v / (1.0 + jnp.exp(-v))


def _modnorm(x, g, scale, shift):
    ms = jnp.mean(x * x, axis=-1, keepdims=True)
    return x * lax.rsqrt(ms + EPS) * g * (1.0 + scale) + shift


def _segsum(t, bd):
    hi = t.astype(BF16)
    lo = (t - hi.astype(F32)).astype(BF16)
    return (jnp.dot(hi, bd, preferred_element_type=F32)
            + jnp.dot(lo, bd, preferred_element_type=F32))


def _mod_index(tiles_per_batch, n_batch):
    return lambda i: (jnp.minimum(i // tiles_per_batch, n_batch), 0, 0)


def _adaln_kernel(c_ref, w_ref, b_ref, o_ref):
    s = _silu(c_ref[...]).astype(BF16)
    o_ref[...] = jnp.dot(s, w_ref[...].astype(BF16), preferred_element_type=F32) + b_ref[...]


def _adaln(cc, w_ada, b_ada):
    depth, d, n = w_ada.shape
    r = cc.shape[0]
    tn = 512
    return pl.pallas_call(
        _adaln_kernel,
        grid=(depth, n // tn),
        in_specs=[pl.BlockSpec((r, d), lambda l, j: (0, 0)),
                  pl.BlockSpec((None, d, tn), lambda l, j: (l, 0, j)),
                  pl.BlockSpec((None, 1, tn), lambda l, j: (l, 0, j))],
        out_specs=pl.BlockSpec((None, r, tn), lambda l, j: (l, 0, j)),
        out_shape=jax.ShapeDtypeStruct((depth, r, n), F32),
        compiler_params=_cparams(2),
        name="adaln",
    )(cc, w_ada, b_ada.reshape(depth, 1, n))


def _inproj_kernel(x_ref, mod_ref, g1_ref, w_ref, cos_ref, sin_ref, qg_ref, kg_ref, bd_ref,
                   q_ref, k_ref, v_ref, z_ref, xbc_ref, hyl_ref, hyc_ref, dt_ref, *, n_lat_tiles):
    x = x_ref[...]
    h = _modnorm(x, g1_ref[...], mod_ref[0, 1:2, :], mod_ref[0, 0:1, :]).astype(BF16)
    pq = jnp.dot(h, w_ref[:, 0:QKV_W], preferred_element_type=F32)
    cos = cos_ref[...]
    sin = sin_ref[...]
    lane = lax.broadcasted_iota(jnp.int32, (1, LANES), 1)
    first_half = (lane % 32) < 16

    def rope(t):
        partner = jnp.where(first_half, pltpu.roll(t, LANES - 16, 1), pltpu.roll(t, 16, 1))
        return t * cos + partner * sin

    q = pq[:, 0:Q_COLS]
    qn = q * lax.rsqrt(_segsum(q * q, bd_ref[...]) * (1.0 / HEAD_DIM) + EPS) * qg_ref[...]
    scale = HEAD_DIM ** -0.5
    for j in range(Q_COLS // LANES):
        pair = (rope(qn[:, LANES * j:LANES * (j + 1)]) * scale).astype(BF16)
        q_ref[2 * j] = pair[:, 0:HEAD_DIM]
        q_ref[2 * j + 1] = pair[:, HEAD_DIM:LANES]
    k = pq[:, Q_COLS:Q_COLS + KV_COLS]
    kn = k * lax.rsqrt(_segsum(k * k, bd_ref[0:KV_COLS, 0:KV_COLS]) * (1.0 / HEAD_DIM) + EPS) * kg_ref[...]
    kt = rope(kn).T.astype(BF16)
    vv = pq[:, Q_COLS + KV_COLS:QKV_W].astype(BF16)
    for j in range(ATT_KV_HEADS):
        k_ref[j] = kt[j * HEAD_DIM:(j + 1) * HEAD_DIM, :]
        v_ref[j] = vv[:, j * HEAD_DIM:(j + 1) * HEAD_DIM]
    zx = jnp.dot(h, w_ref[:, QKV_W:QKV_W + ZX_W], preferred_element_type=F32)
    z_ref[...] = zx[:, 0:SSD_WIDTH]
    xbc_ref[...] = zx[:, SSD_WIDTH:ZX_W]
    hy_t = jnp.dot(h, w_ref[:, QKV_W + ZX_W:QKV_W + ZX_W + HY_COLS], preferred_element_type=F32).T
    is_lat = pl.program_id(0) < n_lat_tiles

    @pl.when(is_lat)
    def _():
        hyl_ref[...] = hy_t

    @pl.when(jnp.logical_not(is_lat))
    def _():
        ctx_len = hyc_ref.shape[2]
        for k in range(hyc_ref.shape[0]):
            hyc_ref[k] = hy_t[:, k * ctx_len:(k + 1) * ctx_len]
    dt_ref[...] = jnp.dot(h, w_ref[:, QKV_W + ZX_W + HY_COLS:PROJ_PAD], preferred_element_type=F32)


def _inproj(xa, mods, g1, w_cat, cos_t, sin_t, qg, kg, bd, n_batch, seq_len, ctx_len):
    t, d = xa.shape
    tm = TOKEN_TILE
    tpb = seq_len // tm
    n_lat = n_batch * tpb
    bpt = tm // ctx_len
    rope_idx = lambda i: (jnp.where(i < n_lat, i % tpb, tpb), 0)
    row = lambda w: pl.BlockSpec((tm, w), lambda i: (i, 0))
    heads = lambda nh: pl.BlockSpec((nh, tm, HEAD_DIM), lambda i: (0, i, 0))
    const = lambda a: pl.BlockSpec(a.shape, lambda i: (0,) * a.ndim)
    lat_tile = lambda i: jnp.minimum(i, n_lat - 1)
    hy_lat = pl.BlockSpec((None, HY_COLS, tm), lambda i: (lat_tile(i) // tpb, 0, lat_tile(i) % tpb))
    hy_ctx = pl.BlockSpec((bpt, HY_COLS, ctx_len), lambda i: (jnp.maximum(i - n_lat, 0), 0, 0))
    return pl.pallas_call(
        functools.partial(_inproj_kernel, n_lat_tiles=n_lat),
        grid=(t // tm,),
        in_specs=[row(d),
                  pl.BlockSpec((1, 6, d), _mod_index(tpb, n_batch)),
                  const(g1), const(w_cat),
                  pl.BlockSpec((tm, LANES), rope_idx), pl.BlockSpec((tm, LANES), rope_idx),
                  const(qg), const(kg), const(bd)],
        out_specs=[heads(ATT_HEADS), pl.BlockSpec((ATT_KV_HEADS, HEAD_DIM, tm), lambda i: (0, 0, i)),
                   heads(ATT_KV_HEADS), row(SSD_WIDTH), row(SSD_XBC_COLS),
                   hy_lat, hy_ctx, row(LANES)],
        out_shape=[jax.ShapeDtypeStruct((ATT_HEADS, t, HEAD_DIM), BF16),
                   jax.ShapeDtypeStruct((ATT_KV_HEADS, HEAD_DIM, t), BF16),
                   jax.ShapeDtypeStruct((ATT_KV_HEADS, t, HEAD_DIM), BF16),
                   jax.ShapeDtypeStruct((t, SSD_WIDTH), F32),
                   jax.ShapeDtypeStruct((t, SSD_XBC_COLS), F32),
                   jax.ShapeDtypeStruct((n_batch, HY_COLS, seq_len), F32),
                   jax.ShapeDtypeStruct((n_batch, HY_COLS, ctx_len), F32),
                   jax.ShapeDtypeStruct((t, LANES), F32)],
        compiler_params=_cparams(1),
        name="inproj",
    )(xa, mods, g1, w_cat, cos_t, sin_t, qg, kg, bd)


def _attn_kernel(sink_ref, q_ref, *refs, n_q, band):
    if band:
        k_ref, v_ref, kc_ref, vc_ref, o_ref, bias_ref = refs
        seq_len = v_ref.shape[1]
        assert n_q >= 3
    else:
        kc_ref, vc_ref, o_ref = refs
    qb = ATT_BLOCK
    rows = ATT_GROUP * qb
    row_id = lax.broadcasted_iota(jnp.int32, (rows, 1), 0)
    nt = (((1,), (1,)), ((), ()))

    if band:
        @pl.when(pl.program_id(0) == 0)
        def _():
            rel0 = (lax.broadcasted_iota(jnp.int32, (rows, band), 1)
                    - lax.broadcasted_iota(jnp.int32, (rows, band), 0) % qb)
            for var in range(3):
                bias_ref[var] = jnp.where(jnp.abs(rel0 - var * WINDOW) <= WINDOW, 0.0, -jnp.inf)

    for j in range(ATT_KV_HEADS):
        kc = kc_ref[j]
        vc = vc_ref[j]
        snk = jnp.zeros((rows, 1), F32)
        for g in range(ATT_GROUP):
            snk = jnp.where(row_id // qb == g, sink_ref[ATT_GROUP * j + g], snk)

        def block(i, carry, j=j, kc=kc, vc=vc, snk=snk):
            q0 = pl.multiple_of(i * qb, qb)
            qh = jnp.concatenate([q_ref[ATT_GROUP * j + g, pl.ds(q0, qb), :] for g in range(ATT_GROUP)],
                                 axis=0)
            s_ctx = jnp.dot(qh, kc, preferred_element_type=F32)
            m = jnp.maximum(jnp.max(s_ctx, axis=-1, keepdims=True), snk)
            if band:
                k0 = pl.multiple_of(jnp.clip(q0 - WINDOW, 0, seq_len - band), qb)
                var = jnp.where(i == 0, 0, jnp.where(i == n_q - 1, 2, 1))
                s_loc = jnp.dot(qh, k_ref[j, :, pl.ds(k0, band)], preferred_element_type=F32) + bias_ref[var]
                m = jnp.maximum(m, jnp.max(s_loc, axis=-1, keepdims=True))
            p_ctx = jnp.exp(s_ctx - m)
            den = jnp.sum(p_ctx, axis=-1, keepdims=True) + jnp.exp(snk - m)
            o = jnp.dot(p_ctx.astype(BF16), vc, preferred_element_type=F32)
            if band:
                p_loc = jnp.exp(s_loc - m)
                den = den + jnp.sum(p_loc, axis=-1, keepdims=True)
                o = o + jnp.dot(p_loc.astype(BF16), v_ref[j, pl.ds(k0, band), :], preferred_element_type=F32)
            o = o / den
            for g in range(ATT_GROUP):
                c0 = (ATT_GROUP * j + g) * HEAD_DIM
                o_ref[pl.ds(q0, qb), c0:c0 + HEAD_DIM] = o[g * qb:(g + 1) * qb, :]
            return carry

        lax.fori_loop(0, n_q, block, 0, unroll=4)


def _attention(sinks, q, k, v, n_batch, seq_len, ctx_len, latent):
    ctx_blk0 = n_batch * seq_len // ctx_len
    kc_spec = pl.BlockSpec((ATT_KV_HEADS, HEAD_DIM, ctx_len), lambda b: (0, 0, ctx_blk0 + b))
    vc_spec = pl.BlockSpec((ATT_KV_HEADS, ctx_len, HEAD_DIM), lambda b: (0, ctx_blk0 + b, 0))
    smem = pl.BlockSpec(memory_space=pltpu.SMEM)
    scratch = []
    if latent:
        rows = seq_len
        band = ATT_BLOCK + 2 * WINDOW
        in_specs = [smem, pl.BlockSpec((ATT_HEADS, rows, HEAD_DIM), lambda b: (0, b, 0)),
                    pl.BlockSpec((ATT_KV_HEADS, HEAD_DIM, rows), lambda b: (0, 0, b)),
                    pl.BlockSpec((ATT_KV_HEADS, rows, HEAD_DIM), lambda b: (0, b, 0)), kc_spec, vc_spec]
        args = (sinks, q, k, v, k, v)
        scratch = [pltpu.VMEM((3, ATT_GROUP * ATT_BLOCK, band), F32)]
    else:
        rows = ctx_len
        band = 0
        in_specs = [smem, pl.BlockSpec((ATT_HEADS, rows, HEAD_DIM), lambda b: (0, ctx_blk0 + b, 0)),
                    kc_spec, vc_spec]
        args = (sinks, q, k, v)
    return pl.pallas_call(
        functools.partial(_attn_kernel, n_q=rows // ATT_BLOCK, band=band),
        grid=(n_batch,),
        in_specs=in_specs,
        out_specs=pl.BlockSpec((rows, ATT_WIDTH), lambda b: (b, 0)),
        out_shape=jax.ShapeDtypeStruct((n_batch * rows, ATT_WIDTH), F32),
        scratch_shapes=scratch,
        compiler_params=_cparams(1),
        name="attn_latent" if latent else "attn_ctx",
    )(*args)


def _merge_kernel(*refs, n_lat_tiles, with_ctx):
    if with_ctx:
        (attl_ref, attc_ref, syl_ref, syc_ref, hyl_ref, hyc_ref, z_ref, x_ref, mod_ref,
         ga_ref, gs_ref, gh_ref, bd_ref, w_ref, o_ref) = refs
        is_lat = pl.program_id(0) < n_lat_tiles
        att = jnp.where(is_lat, attl_ref[...], attc_ref[...])
        sy = jnp.where(is_lat, syl_ref[...], syc_ref[...])
        hy_ctx = jnp.concatenate([hyc_ref[k] for k in range(hyc_ref.shape[0])], axis=1)
        hy_t = jnp.where(is_lat, hyl_ref[...], hy_ctx)
    else:
        attl_ref, syl_ref, hyl_ref, z_ref, x_ref, mod_ref, ga_ref, gs_ref, gh_ref, bd_ref, w_ref, o_ref = refs
        att = attl_ref[...]
        sy = syl_ref[...]
        hy_t = hyl_ref[...]
    a = att * lax.rsqrt(jnp.mean(att * att, axis=-1, keepdims=True) + EPS) * ga_ref[...]
    s = sy * _silu(z_ref[...])
    s = s * lax.rsqrt(jnp.mean(s * s, axis=-1, keepdims=True) + EPS) * gs_ref[...]
    hy = hy_t.T
    hn = hy * lax.rsqrt(_segsum(hy * hy, bd_ref[...]) * (1.0 / (HYENA_WIDTH // HYENA_GROUPS)) + EPS) * gh_ref[...]
    y = jnp.dot(a.astype(BF16), w_ref[0:ATT_WIDTH, :], preferred_element_type=F32)
    y = y + jnp.dot(s.astype(BF16), w_ref[ATT_WIDTH:ATT_WIDTH + SSD_WIDTH, :], preferred_element_type=F32)
    y = y + jnp.dot(hn.astype(BF16), w_ref[ATT_WIDTH + SSD_WIDTH:, :], preferred_element_type=F32)
    o_ref[...] = x_ref[...] + mod_ref[0, 2:3, :] * y


def _merge(att, sy, hy, z, xa, mods, ga, gs, gh, bd, w_out, n_batch, seq_len):
    d = xa.shape[1]
    tm = TOKEN_TILE
    nl = att[0].shape[0] // tm
    with_ctx = att[1] is not None
    t = att[0].shape[0] + (att[1].shape[0] if with_ctx else 0)
    row = lambda w: pl.BlockSpec((tm, w), lambda i: (i, 0))
    lat_row = lambda w: pl.BlockSpec((tm, w), lambda i: (jnp.minimum(i, nl - 1), 0))
    ctx_row = lambda w: pl.BlockSpec((tm, w), lambda i: (jnp.maximum(i - nl, 0), 0))
    const = lambda a: pl.BlockSpec(a.shape, lambda i: (0,) * a.ndim)
    tpb = seq_len // tm
    lat_tile = lambda i: jnp.minimum(i, nl - 1)
    hy_lat = pl.BlockSpec((None, HYENA_WIDTH, tm), lambda i: (lat_tile(i) // tpb, 0, lat_tile(i) % tpb))
    if with_ctx:
        ctx_len = hy[1].shape[2]
        streams = [att[0], att[1], sy[0], sy[1], hy[0], hy[1]]
        specs = [lat_row(ATT_WIDTH), ctx_row(ATT_WIDTH), lat_row(SSD_WIDTH), ctx_row(SSD_WIDTH), hy_lat,
                 pl.BlockSpec((tm // ctx_len, HYENA_WIDTH, ctx_len), lambda i: (jnp.maximum(i - nl, 0), 0, 0))]
    else:
        streams = [att[0], sy[0], hy[0]]
        specs = [row(ATT_WIDTH), row(SSD_WIDTH), hy_lat]
    return pl.pallas_call(
        functools.partial(_merge_kernel, n_lat_tiles=nl, with_ctx=with_ctx),
        grid=(t // tm,),
        in_specs=specs + [row(SSD_WIDTH), row(d),
                          pl.BlockSpec((1, 6, d), _mod_index(seq_len // tm, n_batch)),
                          const(ga), const(gs), const(gh), const(bd), const(w_out)],
        out_specs=row(d),
        out_shape=jax.ShapeDtypeStruct((t, d), F32),
        compiler_params=_cparams(1),
        name="merge_outproj",
    )(*streams, z, xa, mods, ga, gs, gh, bd, w_out)


def _swiglu_accumulate(h, wg_ref, wu_ref, wd_ref, acc_ref):
    for c in range(N_FFN_CHUNKS):
        cols = slice(c * FFN_CHUNK, (c + 1) * FFN_CHUNK)
        g = jnp.dot(h, wg_ref[:, cols], preferred_element_type=F32)
        u = jnp.dot(h, wu_ref[:, cols], preferred_element_type=F32)
        a = (_silu(g) * u).astype(BF16)
        part = jnp.dot(a, wd_ref[cols, :], preferred_element_type=F32)
        if c == 0:
            acc_ref[...] = part
        else:
            acc_ref[...] += part


def _ffn_kernel(x_ref, mod_ref, g2_ref, wg_ref, wu_ref, wd_ref, o_ref, acc_ref):
    x = x_ref[...]
    h = _modnorm(x, g2_ref[...], mod_ref[0, 4:5, :], mod_ref[0, 3:4, :]).astype(BF16)
    _swiglu_accumulate(h, wg_ref, wu_ref, wd_ref, acc_ref)
    o_ref[...] = x + mod_ref[0, 5:6, :] * acc_ref[...]


def _ffn(xa, mods, g2, wg, wu, wd, n_batch, seq_len):
    t, d = xa.shape
    tm = TOKEN_TILE
    row = pl.BlockSpec((tm, d), lambda i: (i, 0))
    resident = lambda a: pl.BlockSpec(a.shape, lambda i: (0,) * a.ndim, pipeline_mode=pl.Buffered(1))
    return pl.pallas_call(
        _ffn_kernel,
        grid=(t // tm,),
        in_specs=[row, pl.BlockSpec((1, 6, d), _mod_index(seq_len // tm, n_batch)),
                  pl.BlockSpec(g2.shape, lambda i: (0, 0)), resident(wg), resident(wu), resident(wd)],
        out_specs=row,
        out_shape=jax.ShapeDtypeStruct((t, d), F32),
        scratch_shapes=[pltpu.VMEM((tm, d), F32)],
        compiler_params=_cparams(1),
        name="ffn",
    )(xa, mods, g2, wg, wu, wd)


def _router_kernel(x_ref, mod_ref, g2_ref, r_ref, h_ref, idx_ref, wt_ref):
    h = _modnorm(x_ref[...], g2_ref[...], mod_ref[0, 4:5, :], mod_ref[0, 3:4, :])
    h_ref[...] = h
    h_hi = h.astype(BF16)
    h_lo = (h - h_hi.astype(F32)).astype(BF16)
    both = jnp.dot(h_hi, r_ref[...], preferred_element_type=F32)
    logits = (both[:, 0:LANES] + both[:, LANES:2 * LANES]
              + jnp.dot(h_lo, r_ref[:, 0:LANES], preferred_element_type=F32))
    lane = lax.broadcasted_iota(jnp.int32, logits.shape, 1)
    neg = -jnp.inf
    l1 = jnp.where(lane < N_EXPERTS, logits, neg)
    m1 = jnp.max(l1, axis=-1, keepdims=True)
    i1 = jnp.min(jnp.where(l1 == m1, lane, LANES), axis=-1, keepdims=True)
    l2 = jnp.where(lane == i1, neg, l1)
    m2 = jnp.max(l2, axis=-1, keepdims=True)
    i2 = jnp.min(jnp.where(l2 == m2, lane, LANES), axis=-1, keepdims=True)
    e = jnp.exp(m2 - m1)
    w1 = 1.0 / (1.0 + e)
    w2 = e / (1.0 + e)
    idx_ref[...] = jnp.where(lane == 0, i1, jnp.where(lane == 1, i2, 0))
    wt_ref[...] = jnp.where(lane == 0, w1, jnp.where(lane == 1, w2, 0.0))


def _router(xa, mods, g2, r_pad, n_rows, n_batch, seq_len):
    d = xa.shape[1]
    tm = TOKEN_TILE
    row = lambda w: pl.BlockSpec((tm, w), lambda i: (i, 0))
    return pl.pallas_call(
        _router_kernel,
        grid=(n_rows // tm,),
        in_specs=[row(d), pl.BlockSpec((1, 6, d), _mod_index(seq_len // tm, n_batch)),
                  pl.BlockSpec(g2.shape, lambda i: (0, 0)), pl.BlockSpec(r_pad.shape, lambda i: (0, 0))],
        out_specs=[row(d), row(LANES), row(LANES)],
        out_shape=[jax.ShapeDtypeStruct((n_rows, d), F32),
                   jax.ShapeDtypeStruct((n_rows, LANES), jnp.int32),
                   jax.ShapeDtypeStruct((n_rows, LANES), F32)],
        compiler_params=_cparams(1),
        name="moe_router",
    )(xa, mods, g2, r_pad)


def _row_copy(src, src_row, dst, dst_row, sem):
    return pltpu.make_async_copy(src.at[pl.ds(src_row, 1), :], dst.at[pl.ds(dst_row, 1), :], sem)


DMA_ISSUE_UNROLL = 8


def _idx_copy(dest_hbm, dest_smem, sem_idx, tile, slot):
    n = dest_hbm.shape[1]
    half = dest_smem.at[pl.ds(pl.multiple_of(slot * n, n), n)]
    return pltpu.make_async_copy(dest_hbm.at[tile], half, sem_idx.at[slot])


def _dispatch_kernel(pad_tile_ref, dest_hbm, h_ref, xs_out, dest_smem, zeros, sem_idx, sem_rows, sem_zero):
    i = pl.program_id(0)
    n = pl.num_programs(0)
    tm = h_ref.shape[0]
    slot = i % 2

    @pl.when(i == 0)
    def _():
        zeros[...] = jnp.zeros_like(zeros)

        def zero_copy(e):
            return pltpu.make_async_copy(zeros, xs_out.at[pl.ds(pl.multiple_of(pad_tile_ref[e], tm), tm), :],
                                         sem_zero)

        for e in range(2 * N_EXPERTS):
            @pl.when(pad_tile_ref[e] >= 0)
            def _(e=e):
                zero_copy(e).start()
        for e in range(2 * N_EXPERTS):
            @pl.when(pad_tile_ref[e] >= 0)
            def _(e=e):
                zero_copy(e).wait()
        _idx_copy(dest_hbm, dest_smem, sem_idx, 0, 0).start()

    @pl.when(i + 1 < n)
    def _():
        _idx_copy(dest_hbm, dest_smem, sem_idx, i + 1, 1 - slot).start()

    _idx_copy(dest_hbm, dest_smem, sem_idx, i, slot).wait()

    base = slot * (2 * tm)

    def issue(r, carry):
        _row_copy(h_ref, r, xs_out, dest_smem[base + 2 * r], sem_rows).start()
        _row_copy(h_ref, r, xs_out, dest_smem[base + 2 * r + 1], sem_rows).start()
        return carry

    lax.fori_loop(0, tm, issue, 0, unroll=DMA_ISSUE_UNROLL)
    for _ in range(2):
        pltpu.make_async_copy(h_ref, xs_out.at[pl.ds(0, tm), :], sem_rows).wait()


def _dispatch(pad_tile, dest, h, n_slots):
    n_rows, d = h.shape
    tm = TOKEN_TILE
    assert tm == EXPERT_TILE
    return pl.pallas_call(
        _dispatch_kernel,
        grid_spec=pltpu.PrefetchScalarGridSpec(
            num_scalar_prefetch=1,
            grid=(n_rows // tm,),
            in_specs=[pl.BlockSpec(memory_space=pl.ANY), pl.BlockSpec((tm, d), lambda i, pt: (i, 0))],
            out_specs=pl.BlockSpec(memory_space=pl.ANY),
            scratch_shapes=[pltpu.SMEM((4 * tm,), jnp.int32), pltpu.VMEM((tm, d), F32),
                            pltpu.SemaphoreType.DMA((2,)), pltpu.SemaphoreType.DMA(()),
                            pltpu.SemaphoreType.DMA(())]),
        out_shape=jax.ShapeDtypeStruct((n_slots, d), F32),
        compiler_params=_cparams(1),
        name="moe_dispatch",
    )(pad_tile, dest.reshape(n_rows // tm, 2 * tm), h)


def _expert_kernel(te_ref, nused_ref, xs_ref, wg_ref, wu_ref, wd_ref, o_ref, acc_ref):
    del te_ref
    live = pl.program_id(0) < nused_ref[0]

    @pl.when(live)
    def _():
        _swiglu_accumulate(xs_ref[...].astype(BF16), wg_ref, wu_ref, wd_ref, acc_ref)
        o_ref[...] = acc_ref[...]

    @pl.when(jnp.logical_not(live))
    def _():
        o_ref[...] = jnp.zeros_like(o_ref)


def _experts(tile_expert, n_used, xs, wg, wu, wd):
    s, d = xs.shape
    tm = EXPERT_TILE
    row = pl.BlockSpec((tm, d), lambda i, te, nu: (i, 0))
    xs_row = pl.BlockSpec((tm, d), lambda i, te, nu: (jnp.minimum(i, nu[0] - 1), 0))
    wspec = lambda a: pl.BlockSpec((None,) + a.shape[1:], lambda i, te, nu: (te[i], 0, 0))
    return pl.pallas_call(
        _expert_kernel,
        grid_spec=pltpu.PrefetchScalarGridSpec(
            num_scalar_prefetch=2,
            grid=(s // tm,),
            in_specs=[xs_row, wspec(wg), wspec(wu), wspec(wd)],
            out_specs=row,
            scratch_shapes=[pltpu.VMEM((tm, d), F32)]),
        out_shape=jax.ShapeDtypeStruct((s, d), F32),
        compiler_params=_cparams(1),
        name="moe_experts",
    )(tile_expert, n_used, xs, wg, wu, wd)


def _combine_kernel(dest_hbm, eo_hbm, x_ref, wt_ref, mod_ref, o_ref, dest_smem, buf, sem_idx, sem_rows):
    i = pl.program_id(0)
    n = pl.num_programs(0)
    tm = x_ref.shape[0]
    slot = i % 2

    def gather(s):
        base = s * (2 * tm)

        def issue(r, carry):
            _row_copy(eo_hbm, dest_smem[base + 2 * r], buf.at[s, 0], r, sem_rows.at[s]).start()
            _row_copy(eo_hbm, dest_smem[base + 2 * r + 1], buf.at[s, 1], r, sem_rows.at[s]).start()
            return carry

        lax.fori_loop(0, tm, issue, 0, unroll=DMA_ISSUE_UNROLL)

    @pl.when(i == 0)
    def _():
        first = _idx_copy(dest_hbm, dest_smem, sem_idx, 0, 0)
        first.start()
        first.wait()
        gather(0)

        @pl.when(n > 1)
        def _():
            _idx_copy(dest_hbm, dest_smem, sem_idx, 1, 1).start()

    @pl.when(i + 1 < n)
    def _():
        _idx_copy(dest_hbm, dest_smem, sem_idx, i + 1, 1 - slot).wait()

        @pl.when(i + 2 < n)
        def _():
            _idx_copy(dest_hbm, dest_smem, sem_idx, i + 2, slot).start()

        gather(1 - slot)

    for k in range(2):
        pltpu.make_async_copy(eo_hbm.at[pl.ds(0, tm), :], buf.at[slot, k], sem_rows.at[slot]).wait()
    wt = wt_ref[...]
    y = wt[:, 0:1] * buf[slot, 0] + wt[:, 1:2] * buf[slot, 1]
    o_ref[...] = x_ref[...] + mod_ref[0, 5:6, :] * y


def _combine(dest, eo, xa, wts, mods, n_rows, n_batch, seq_len):
    d = xa.shape[1]
    tm = TOKEN_TILE
    row = lambda w: pl.BlockSpec((tm, w), lambda i: (i, 0))
    return pl.pallas_call(
        _combine_kernel,
        grid=(n_rows // tm,),
        in_specs=[pl.BlockSpec(memory_space=pl.ANY), pl.BlockSpec(memory_space=pl.ANY), row(d), row(LANES),
                  pl.BlockSpec((1, 6, d), _mod_index(seq_len // tm, n_batch))],
        out_specs=row(d),
        out_shape=jax.ShapeDtypeStruct((n_rows, d), F32),
        scratch_shapes=[pltpu.SMEM((4 * tm,), jnp.int32), pltpu.VMEM((2, 2, tm, d), F32),
                        pltpu.SemaphoreType.DMA((2,)), pltpu.SemaphoreType.DMA((2,))],
        compiler_params=_cparams(1),
        name="moe_combine",
    )(dest.reshape(n_rows // tm, 2 * tm), eo, xa, wts, mods)


def _moe(xa, mods, g2, r_pad, wg, wu, wd, n_rows, n_batch, seq_len):
    h, idx, wts = _router(xa, mods, g2, r_pad, n_rows, n_batch, seq_len)
    tm = EXPERT_TILE
    e_flat = idx[:, :2].reshape(-1)
    onehot = (e_flat[:, None] == jnp.arange(N_EXPERTS, dtype=jnp.int32)[None, :]).astype(jnp.int32)
    csum = jnp.cumsum(onehot, axis=0)
    counts = csum[-1]
    rank = jnp.sum(onehot * csum, axis=1) - 1
    padded = ((counts + tm - 1) // tm) * tm
    ends = jnp.cumsum(padded)
    starts = ends - padded
    dest = (jnp.sum(onehot * starts[None, :], axis=1) + rank).astype(jnp.int32)
    n_slots = 2 * n_rows + N_EXPERTS * tm
    tile_start = jnp.arange(n_slots // tm, dtype=jnp.int32) * tm
    tile_expert = jnp.minimum(jnp.sum((tile_start[:, None] >= ends[None, :]).astype(jnp.int32), axis=1),
                              N_EXPERTS - 1).astype(jnp.int32)
    n_used = (ends[-1:] // tm).astype(jnp.int32)
    tail = ends[-1] + jnp.arange(N_EXPERTS, dtype=ends.dtype) * tm
    pad_tile = jnp.concatenate([jnp.where(padded > 0, ends - tm, -1),
                                jnp.where(tail < n_slots, tail, -1)]).astype(jnp.int32)
    xs = _dispatch(pad_tile, dest, h, n_slots)
    eo = _experts(tile_expert, n_used, xs, wg, wu, wd)
    return _combine(dest, eo, xa, wts, mods, n_rows, n_batch, seq_len)


def _softplus(v):
    return jnp.maximum(v, 0.0) + jnp.log1p(jnp.exp(-jnp.abs(v)))


def _ssd_kernel(xl_ref, dl_ref, xc_ref, dc_ref, cw_ref, cb_ref, dtb_ref, alogc_ref, dsk_ref, *rest, want_ctx):
    if want_ctx:
        yl_ref, yc_ref, xs_l, dtt_l, xs_c, dtt_c, st_ref = rest
    else:
        yl_ref, xs_l, dtt_l, xs_c, dtt_c, st_ref = rest
        yc_ref = None
    ck = SSD_CHUNK
    hp = SSD_HEAD_DIM
    ns = SSD_STATE
    n_col = 2 * SSD_HEADS
    hi = lax.Precision.HIGHEST
    row = lax.broadcasted_iota(jnp.int32, (ck, 1), 0)
    li = lax.broadcasted_iota(jnp.int32, (ck, ck), 0)
    si = lax.broadcasted_iota(jnp.int32, (ck, ck), 1)
    masks = (si <= li, si >= li)
    tris = (masks[0].astype(F32), masks[1].astype(F32))
    a_col = -jnp.exp(alogc_ref[...])
    dskip = dsk_ref[...]

    def prep(raw_ref, dtraw_ref, xs_s, dtt_s, y_ref):
        n = raw_ref.shape[0]
        nk = n // ck

        def body(k, carry):
            r0 = pl.multiple_of(k * ck, ck)
            a = raw_ref[pl.ds(r0, ck), :]
            top = raw_ref[pl.ds(pl.multiple_of(jnp.maximum(r0 - 8, 0), 8), 8), :][7:8, :]
            bot = raw_ref[pl.ds(pl.multiple_of(jnp.minimum(r0 + ck, n - 8), 8), 8), :][0:1, :]
            top = jnp.where(k > 0, top, 0.0)
            bot = jnp.where(k < nk - 1, bot, 0.0)
            prev = jnp.where(row == 0, top, pltpu.roll(a, 1, 0))
            nxt = jnp.where(row == ck - 1, bot, pltpu.roll(a, ck - 1, 0))
            xs = _silu(prev * cw_ref[0:1, :] + a * cw_ref[1:2, :] + nxt * cw_ref[2:3, :] + cb_ref[...])
            xs_s[pl.ds(r0, ck), :] = xs
            dt = _softplus(dtraw_ref[pl.ds(r0, ck), :] + dtb_ref[...])
            dtt_s[:, pl.ds(r0, ck)] = dt.T[0:n_col, :]
            if y_ref is not None:
                y_ref[pl.ds(r0, ck), :] = xs[:, 0:SSD_WIDTH] * dskip
            return carry

        lax.fori_loop(0, nk, body, 0)

    def run(xs_s, dtt_s, y_ref):
        nk = xs_s.shape[0] // ck

        def one(kk, dr):
            r0 = pl.multiple_of(kk * ck, ck)
            xc = xs_s[pl.ds(r0, ck), :]
            dtr = dtt_s[:, pl.ds(r0, ck)]
            dta_r = dtr * a_col
            la_r = jnp.dot(dta_r, tris[1 - dr], precision=hi, preferred_element_type=F32)
            la_end = la_r[:, ck - 1:ck] if dr == 0 else la_r[:, 0:1]
            if y_ref is not None:
                la_c = jnp.concatenate([la_r, jnp.zeros((ck - n_col, ck), F32)], axis=0).T
            ys = []
            for g in range(SSD_GROUPS):
                b_t = xc[:, SSD_WIDTH + g * ns:SSD_WIDTH + (g + 1) * ns].T
                c0 = SSD_WIDTH + SSD_GROUPS * ns + g * ns
                c_g = xc[:, c0:c0 + ns].astype(BF16)
                if y_ref is not None:
                    scores = jnp.dot(c_g, b_t.astype(BF16), preferred_element_type=F32)
                for hh in range(SSD_HEADS // SSD_GROUPS):
                    h = g * (SSD_HEADS // SSD_GROUPS) + hh
                    col = dr * SSD_HEADS + h
                    xh = xc[:, h * hp:(h + 1) * hp].astype(BF16)
                    dt_row = dtr[col:col + 1, :]
                    le = la_end[col:col + 1, :]
                    st = st_ref[col]
                    if y_ref is not None:
                        la_col = la_c[:, col:col + 1]
                        decay = jnp.exp(jnp.where(masks[dr], la_col - la_r[col:col + 1, :], -jnp.inf))
                        y = jnp.dot((scores * decay * dt_row).astype(BF16), xh, preferred_element_type=F32)
                        y = y + jnp.dot(c_g, st.astype(BF16), preferred_element_type=F32) * jnp.exp(la_col)
                        ys.append(y)
                    bw = (b_t * (dt_row * jnp.exp(le - la_r[col:col + 1, :]))).astype(BF16)
                    st_ref[col] = st * jnp.exp(le) + jnp.dot(bw, xh, preferred_element_type=F32)
            if y_ref is not None:
                y_ref[pl.ds(r0, ck), :] += jnp.concatenate(ys, axis=1)

        def body(k, carry):
            one(k, 0)
            one(nk - 1 - k, 1)
            return carry

        lax.fori_loop(0, nk, body, 0, unroll=2)

    prep(xl_ref, dl_ref, xs_l, dtt_l, yl_ref)
    prep(xc_ref, dc_ref, xs_c, dtt_c, yc_ref)
    st_ref[...] = jnp.zeros_like(st_ref)
    run(xs_c, dtt_c, yc_ref)
    run(xs_l, dtt_l, yl_ref)


def _ssd(xbc, dtp, conv_w, conv_b, dt_bias, a_log, d_skip, n_batch, seq_len, ctx_len, want_ctx):
    ctx0 = n_batch * seq_len // ctx_len
    pad = lambda v: jnp.pad(v.reshape(1, -1), ((0, 0), (0, LANES - v.size)))
    lat = lambda w: pl.BlockSpec((seq_len, w), lambda b: (b, 0))
    ctx = lambda w: pl.BlockSpec((ctx_len, w), lambda b: (ctx0 + b, 0))
    const = lambda a: pl.BlockSpec(a.shape, lambda b: (0,) * a.ndim)
    consts = (conv_w, conv_b.reshape(1, -1), pad(dt_bias), a_log.reshape(-1, 1),
              jnp.repeat(d_skip, SSD_HEAD_DIM).reshape(1, -1))
    out_specs = [lat(SSD_WIDTH)]
    out_shape = [jax.ShapeDtypeStruct((n_batch * seq_len, SSD_WIDTH), F32)]
    if want_ctx:
        out_specs.append(pl.BlockSpec((ctx_len, SSD_WIDTH), lambda b: (b, 0)))
        out_shape.append(jax.ShapeDtypeStruct((n_batch * ctx_len, SSD_WIDTH), F32))
    return pl.pallas_call(
        functools.partial(_ssd_kernel, want_ctx=want_ctx),
        grid=(n_batch,),
        in_specs=[lat(SSD_XBC_COLS), lat(LANES), ctx(SSD_XBC_COLS), ctx(LANES)] + [const(a) for a in consts],
        out_specs=out_specs,
        out_shape=out_shape,
        scratch_shapes=[pltpu.VMEM((seq_len, SSD_XBC_COLS), F32), pltpu.VMEM((2 * SSD_HEADS, seq_len), F32),
                        pltpu.VMEM((ctx_len, SSD_XBC_COLS), F32), pltpu.VMEM((2 * SSD_HEADS, ctx_len), F32),
                        pltpu.VMEM((2 * SSD_HEADS, SSD_STATE, SSD_HEAD_DIM), F32)],
        compiler_params=_cparams(1),
        name="ssd",
    )(xbc, dtp, xbc, dtp, *consts)


HY_BLOCK = 256
HY_CH_STEP = 8


def _hyena_tables(seq_len):
    f32 = np.float32
    nj = 2 * seq_len
    lag = np.arange(nj, dtype=np.int32) - seq_len
    dist = np.abs(lag)
    pos = np.minimum(dist, seq_len - 1)
    t = np.linspace(0.0, 1.0, seq_len, dtype=f32)[pos]
    w = (f32(2.0 * math.pi / seq_len) * np.arange(seq_len, dtype=f32))[pos]
    bands = (HYENA_POS_DIM - 1) // 2
    freqs = np.linspace(1e-4, bands - 1, bands, dtype=f32)[None, :]
    ang = (freqs * w[:, None]).astype(f32)
    z = np.concatenate([t[:, None], np.cos(ang), -np.sin(ang)], axis=-1).astype(f32)
    zt = np.pad(z.T, ((0, (-HYENA_POS_DIM) % 8), (0, 0)))
    deltas = np.abs(np.linspace(math.log(HYENA_DECAY_TARGET) / HYENA_SLOW_DECAY,
                                math.log(HYENA_DECAY_TARGET) / HYENA_FAST_DECAY, HYENA_WIDTH, dtype=f32))
    dec = (np.exp(-t[None, :] * deltas[:, None]) * (dist < seq_len)[None, :]).astype(f32)
    fwd = (lag >= 0).astype(f32)[None, :]
    return jnp.asarray(zt), jnp.asarray(dec), jnp.asarray(fwd)


def _hyfilt_kernel(zt_ref, dec_ref, fwd_ref, w1_ref, b1_ref, f1_ref, w2_ref, b2_ref, f2_ref, w3_ref, o_ref):
    hi = lax.Precision.HIGHEST
    h = jnp.sin(f1_ref[...] * (jnp.dot(w1_ref[...], zt_ref[...], precision=hi, preferred_element_type=F32)
                               + b1_ref[...]))
    h = jnp.sin(f2_ref[...] * (jnp.dot(w2_ref[...], h, precision=hi, preferred_element_type=F32) + b2_ref[...]))
    hw = jnp.dot(w3_ref[...], h, precision=hi, preferred_element_type=F32)
    fwd = fwd_ref[...] > 0.5
    dec = dec_ref[...]
    nw = HYENA_WIDTH
    for o in range(HYENA_ORDER):
        o_ref[o] = jnp.where(fwd, hw[o * nw:(o + 1) * nw], hw[(HYENA_ORDER + o) * nw:(HYENA_ORDER + o + 1) * nw]) * dec


def _hyena_filters(tables, w1, b1, f1, w2, b2, f2, w3):
    zt, dec, fwd = tables
    nj = zt.shape[1]
    tj = 512
    col = lambda v: v.reshape(-1, 1)
    w1t = jnp.pad(w1.T, ((0, 0), (0, zt.shape[0] - w1.shape[0])))
    consts = (w1t, col(b1), col(f1), w2.T, col(b2), col(f2), w3.T)
    lanes = lambda a: pl.BlockSpec((a.shape[0], tj), lambda j: (0, j))
    const = lambda a: pl.BlockSpec(a.shape, lambda j: (0, 0))
    return pl.pallas_call(
        _hyfilt_kernel,
        grid=(nj // tj,),
        in_specs=[lanes(zt), lanes(dec), lanes(fwd)] + [const(a) for a in consts],
        out_specs=pl.BlockSpec((HYENA_ORDER, HYENA_WIDTH, tj), lambda j: (0, 0, j)),
        out_shape=jax.ShapeDtypeStruct((HYENA_ORDER, HYENA_WIDTH, nj), F32),
        compiler_params=_cparams(1),
        name="hyena_filters",
    )(zt, dec, fwd, *consts)


def _hyconv_kernel(cw_ref, cb_ref, hb_ref, v_ref, x1_ref, x2_ref, kf_ref, o_ref):
    n_b, n_ch, seq_len = v_ref.shape
    nb = seq_len // HY_BLOCK
    blk = HY_BLOCK
    c_base = pl.program_id(0) * n_ch
    lane = lax.broadcasted_iota(jnp.int32, (1, seq_len), 1)

    def sconv(x, ch):
        prev = jnp.where(lane == 0, 0.0, pltpu.roll(x, 1, 1))
        nxt = jnp.where(lane == seq_len - 1, 0.0, pltpu.roll(x, seq_len - 1, 1))
        return prev * cw_ref[0, ch] + x * cw_ref[1, ch] + nxt * cw_ref[2, ch] + cb_ref[ch]

    def long_conv(vals, kf_row):
        skew = pltpu.roll(jnp.broadcast_to(kf_row, (blk, 2 * seq_len)), 0, 1, stride=1, stride_axis=0)
        vb = vals.astype(BF16)
        acc = [None] * nb
        for d in range(-(nb - 1), nb):
            tt = skew[:, seq_len + d * blk:seq_len + (d + 1) * blk].astype(BF16)
            sis = list(range(max(0, -d), min(nb, nb - d)))
            lhs = [vb[:, s * blk:(s + 1) * blk] for s in sis]
            lhs = lhs[0] if len(lhs) == 1 else jnp.concatenate(lhs, axis=0)
            out = jnp.dot(lhs, tt, preferred_element_type=F32)
            for idx, s in enumerate(sis):
                piece = out[idx * n_b:(idx + 1) * n_b]
                acc[s + d] = piece if acc[s + d] is None else acc[s + d] + piece
        return acc[0] if nb == 1 else jnp.concatenate(acc, axis=1)

    def channel(cc, carry):
        ch = c_base + cc
        v = sconv(v_ref[:, cc, :], ch)
        x1 = sconv(x1_ref[:, cc, :], HYENA_WIDTH + ch)
        x2 = sconv(x2_ref[:, cc, :], 2 * HYENA_WIDTH + ch)
        z = x1 * (long_conv(v, kf_ref[0, pl.ds(cc, 1), :]) + v * hb_ref[0, ch])
        o_ref[:, cc, :] = x2 * (long_conv(z, kf_ref[1, pl.ds(cc, 1), :]) + z * hb_ref[1, ch])
        return carry

    lax.fori_loop(0, n_ch, channel, 0)


def _hyena_conv(hy, kf, conv_w, conv_b, hy_bias):
    n_batch, _, seq_len = hy.shape
    cs = HY_CH_STEP
    nw = HYENA_WIDTH
    stream = lambda k: pl.BlockSpec((n_batch, cs, seq_len), lambda c: (0, k * (nw // cs) + c, 0))
    smem = pl.BlockSpec(memory_space=pltpu.SMEM)
    return pl.pallas_call(
        _hyconv_kernel,
        grid=(nw // cs,),
        in_specs=[smem, smem, smem, stream(0), stream(1), stream(2),
                  pl.BlockSpec((HYENA_ORDER, cs, 2 * seq_len), lambda c: (0, c, 0))],
        out_specs=pl.BlockSpec((n_batch, cs, seq_len), lambda c: (0, c, 0)),
        out_shape=jax.ShapeDtypeStruct((n_batch, nw, seq_len), F32),
        compiler_params=_cparams(1),
        name="hyena_conv",
    )(conv_w, conv_b, hy_bias, hy, hy, hy, kf)


def _rope_tables(seq_len, extra):
    rows = seq_len // GRID_W
    row = jnp.repeat(jnp.arange(rows, dtype=F32), GRID_W)
    col = jnp.tile(jnp.arange(GRID_W, dtype=F32), rows)
    inv = ROPE_THETA ** (-jnp.arange(0, ROPE_AXIS_DIM, 2, dtype=F32) / ROPE_AXIS_DIM)
    ang = jnp.stack([row[:, None] * inv, col[:, None] * inv], axis=1)
    cos = jnp.cos(ang)
    sin = jnp.sin(ang)
    cos_h = jnp.concatenate([cos, cos], axis=-1).reshape(seq_len, HEAD_DIM)
    sin_h = jnp.concatenate([-sin, sin], axis=-1).reshape(seq_len, HEAD_DIM)
    cos_t = jnp.concatenate([jnp.tile(cos_h, (1, LANES // HEAD_DIM)), jnp.ones((extra, LANES), F32)], axis=0)
    sin_t = jnp.concatenate([jnp.tile(sin_h, (1, LANES // HEAD_DIM)), jnp.zeros((extra, LANES), F32)], axis=0)
    return cos_t, sin_t


def _block_diag_ones(n, seg):
    i = jnp.arange(n) // seg
    return (i[:, None] == i[None, :]).astype(BF16)


def kernel(x, c, ctx, c_ctx, w_ada, b_ada, norm1, norm2, w_in, w_out, q_norm, k_norm, att_sinks, att_out_norm, ssd_conv_w, ssd_conv_b, ssd_dt_bias, ssd_a_log, ssd_d, ssd_norm, hy_conv_w, hy_conv_b, hy_w1, hy_b1, hy_f1, hy_w2, hy_b2, hy_f2, hy_w3, hy_bias, hy_out_norm, ffn_w_gate, ffn_w_up, ffn_w_down, moe_router, moe_w_gate, moe_w_up, moe_w_down):
    n_batch, seq_len, d = x.shape
    ctx_len = ctx.shape[1]
    n_lat = n_batch * seq_len
    n_ctx = n_batch * ctx_len
    depth = w_in.shape[0]
    xa = jnp.concatenate([x.reshape(n_lat, d), ctx.reshape(n_ctx, d)], axis=0)

    cc = jnp.concatenate([c, c_ctx[None, :]], axis=0)
    pad_rows = (-cc.shape[0]) % 8
    cc = jnp.pad(cc, ((0, pad_rows), (0, 0)))
    mods_all = _adaln(cc, w_ada, b_ada)[:, :n_batch + 1].reshape(depth, n_batch + 1, 6, d)

    cos_t, sin_t = _rope_tables(seq_len, TOKEN_TILE)
    bd_q = _block_diag_ones(Q_COLS, HEAD_DIM)
    bd_h = _block_diag_ones(HYENA_WIDTH, HYENA_WIDTH // HYENA_GROUPS)
    hy_tab_l = _hyena_tables(seq_len)
    hy_tab_c = _hyena_tables(ctx_len)

    for i in range(depth):
        last = i == depth - 1
        j = i // 2
        mods = mods_all[i]
        wi = w_in[i]
        c_dt = QKV_W + ZX_W
        w_cat = jnp.concatenate([wi[:, :c_dt], wi[:, c_dt + SSD_DT_COLS:], wi[:, c_dt:c_dt + SSD_DT_COLS],
                                 jnp.zeros((d, LANES - SSD_DT_COLS), F32)], axis=1).astype(BF16)
        qg = jnp.tile(q_norm[i], Q_COLS // HEAD_DIM)[None, :]
        kg = jnp.tile(k_norm[i], KV_COLS // HEAD_DIM)[None, :]
        q, k, v, z, xbc, hy_l, hy_c, dtp = _inproj(xa, mods, norm1[i][None, :], w_cat, cos_t, sin_t, qg, kg,
                                                   bd_q, n_batch, seq_len, ctx_len)

        att_l = _attention(att_sinks[i], q, k, v, n_batch, seq_len, ctx_len, True)
        ssd_out = _ssd(xbc, dtp, ssd_conv_w[i], ssd_conv_b[i], ssd_dt_bias[i], ssd_a_log[i], ssd_d[i],
                       n_batch, seq_len, ctx_len, not last)
        filt = (hy_w1[i], hy_b1[i], hy_f1[i], hy_w2[i], hy_b2[i], hy_f2[i], hy_w3[i])
        hyo_l = _hyena_conv(hy_l, _hyena_filters(hy_tab_l, *filt), hy_conv_w[i], hy_conv_b[i], hy_bias[i])
        if last:
            att = (att_l, None)
            sy = (ssd_out[0], None)
            hyo = (hyo_l, None)
            n_rows = n_lat
        else:
            att = (att_l, _attention(att_sinks[i], q, k, v, n_batch, seq_len, ctx_len, False))
            sy = tuple(ssd_out)
            hyo_c = _hyena_conv(hy_c, _hyena_filters(hy_tab_c, *filt), hy_conv_w[i], hy_conv_b[i], hy_bias[i])
            hyo = (hyo_l, hyo_c)
            n_rows = n_lat + n_ctx
        xa = _merge(att, sy, hyo, z, xa, mods, att_out_norm[i][None, :], ssd_norm[i][None, :],
                    hy_out_norm[i][None, :], bd_h, w_out[i].astype(BF16), n_batch, seq_len)

        g2 = norm2[i][None, :]
        if i % 2 == 0:
            xa = _ffn(xa, mods, g2, ffn_w_gate[j].astype(BF16), ffn_w_up[j].astype(BF16),
                      ffn_w_down[j].astype(BF16), n_batch, seq_len)
        else:
            r_full = jnp.pad(moe_router[j], ((0, 0), (0, LANES - N_EXPERTS)))
            r_hi = r_full.astype(BF16)
            r_pad = jnp.concatenate([r_hi, (r_full - r_hi.astype(F32)).astype(BF16)], axis=1)
            xa = _moe(xa, mods, g2, r_pad, moe_w_gate[j].astype(BF16), moe_w_up[j].astype(BF16),
                      moe_w_down[j].astype(BF16), n_rows, n_batch, seq_len)
    return xa[:n_lat].reshape(n_batch, seq_len, d)
```

```python
import functools
import math

import jax
import jax.numpy as jnp
import numpy as np
from jax import lax
from jax.experimental import pallas as pl
from jax.experimental.pallas import tpu as pltpu

F32 = jnp.float32
BF16 = jnp.bfloat16

D_MODEL = 1024
DEPTH = 4
GRID_W = 64
EPS = 1e-6
HEAD_DIM = 64
ATT_WIDTH = 512
ATT_HEADS = 8
ATT_KV_HEADS = 2
ATT_GROUP = 4
WINDOW = 128
ATT_BLOCK = 128
ROPE_THETA = 10000.0
ROPE_AXIS_DIM = 32
SSD_WIDTH = 256
SSD_HEAD_DIM = 64
SSD_HEADS = 4
SSD_STATE = 64
SSD_GROUPS = 2
SSD_CHUNK = 128
HYENA_WIDTH = 256
HYENA_GROUPS = 4
HYENA_ORDER = 2
HYENA_POS_DIM = 33
HYENA_FAST_DECAY = 0.3
HYENA_SLOW_DECAY = 1.5
HYENA_DECAY_TARGET = 1e-2
Q_COLS = 512
KV_COLS = 128
SSD_XBC_COLS = 512
SSD_DT_COLS = 8
HY_COLS = 768
FFN_DIM = 2816
N_EXPERTS = 8
FFN_CHUNK = 256
N_FFN_CHUNKS = FFN_DIM // FFN_CHUNK
LANES = 128
QKV_W = Q_COLS + 2 * KV_COLS
ZX_W = SSD_WIDTH + SSD_XBC_COLS
PROJ_PAD = QKV_W + ZX_W + HY_COLS + LANES
VMEM_LIMIT = 56 * 1024 * 1024
TOKEN_TILE = 512
EXPERT_TILE = 512


def _cparams(n_axes):
    return pltpu.CompilerParams(dimension_semantics=("arbitrary",) * n_axes,
                                vmem_limit_bytes=VMEM_LIMIT)


def _silu(v):
    return v / (1.0 + jnp.exp(-v))


def _modnorm(x, g, scale, shift):
    ms = jnp.mean(x * x, axis=-1, keepdims=True)
    return x * lax.rsqrt(ms + EPS) * g * (1.0 + scale) + shift


def _segsum(t, bd):
    hi = t.astype(BF16)
    lo = (t - hi.astype(F32)).astype(BF16)
    return (jnp.dot(hi, bd, preferred_element_type=F32)
            + jnp.dot(lo, bd, preferred_element_type=F32))


def _mod_index(tiles_per_batch, n_batch):
    return lambda i: (jnp.minimum(i // tiles_per_batch, n_batch), 0, 0)


def _adaln_kernel(c_ref, w_ref, b_ref, o_ref):
    s = _silu(c_ref[...]).astype(BF16)
    o_ref[...] = jnp.dot(s, w_ref[...].astype(BF16), preferred_element_type=F32) + b_ref[...]


def _adaln(cc, w_ada, b_ada):
    depth, d, n = w_ada.shape
    r = cc.shape[0]
    tn = 512
    return pl.pallas_call(
        _adaln_kernel,
        grid=(depth, n // tn),
        in_specs=[pl.BlockSpec((r, d), lambda l, j: (0, 0)),
                  pl.BlockSpec((None, d, tn), lambda l, j: (l, 0, j)),
                  pl.BlockSpec((None, 1, tn), lambda l, j: (l, 0, j))],
        out_specs=pl.BlockSpec((None, r, tn), lambda l, j: (l, 0, j)),
        out_shape=jax.ShapeDtypeStruct((depth, r, n), F32),
        compiler_params=_cparams(2),
        name="adaln",
    )(cc, w_ada, b_ada.reshape(depth, 1, n))


def _inproj_kernel(x_ref, mod_ref, g1_ref, w_ref, cos_ref, sin_ref, qg_ref, kg_ref, bd_ref,
                   q_ref, k_ref, v_ref, z_ref, xbc_ref, hyl_ref, hyc_ref, dt_ref, *, n_lat_tiles):
    x = x_ref[...]
    h = _modnorm(x, g1_ref[...], mod_ref[0, 1:2, :], mod_ref[0, 0:1, :]).astype(BF16)
    pq = jnp.dot(h, w_ref[:, 0:QKV_W], preferred_element_type=F32)
    cos = cos_ref[...]
    sin = sin_ref[...]
    lane = lax.broadcasted_iota(jnp.int32, (1, LANES), 1)
    first_half = (lane % 32) < 16

    def rope(t):
        partner = jnp.where(first_half, pltpu.roll(t, LANES - 16, 1), pltpu.roll(t, 16, 1))
        return t * cos + partner * sin

    q = pq[:, 0:Q_COLS]
    qn = q * lax.rsqrt(_segsum(q * q, bd_ref[...]) * (1.0 / HEAD_DIM) + EPS) * qg_ref[...]
    scale = HEAD_DIM ** -0.5
    for j in range(Q_COLS // LANES):
        pair = (rope(qn[:, LANES * j:LANES * (j + 1)]) * scale).astype(BF16)
        q_ref[2 * j] = pair[:, 0:HEAD_DIM]
        q_ref[2 * j + 1] = pair[:, HEAD_DIM:LANES]
    k = pq[:, Q_COLS:Q_COLS + KV_COLS]
    kn = k * lax.rsqrt(_segsum(k * k, bd_ref[0:KV_COLS, 0:KV_COLS]) * (1.0 / HEAD_DIM) + EPS) * kg_ref[...]
    kt = rope(kn).T.astype(BF16)
    vv = pq[:, Q_COLS + KV_COLS:QKV_W].astype(BF16)
    for j in range(ATT_KV_HEADS):
        k_ref[j] = kt[j * HEAD_DIM:(j + 1) * HEAD_DIM, :]
        v_ref[j] = vv[:, j * HEAD_DIM:(j + 1) * HEAD_DIM]
    zx = jnp.dot(h, w_ref[:, QKV_W:QKV_W + ZX_W], preferred_element_type=F32)
    z_ref[...] = zx[:, 0:SSD_WIDTH]
    xbc_ref[...] = zx[:, SSD_WIDTH:ZX_W]
    hy_t = jnp.dot(h, w_ref[:, QKV_W + ZX_W:QKV_W + ZX_W + HY_COLS], preferred_element_type=F32).T
    is_lat = pl.program_id(0) < n_lat_tiles

    @pl.when(is_lat)
    def _():
        hyl_ref[...] = hy_t

    @pl.when(jnp.logical_not(is_lat))
    def _():
        ctx_len = hyc_ref.shape[2]
        for k in range(hyc_ref.shape[0]):
            hyc_ref[k] = hy_t[:, k * ctx_len:(k + 1) * ctx_len]
    dt_ref[...] = jnp.dot(h, w_ref[:, QKV_W + ZX_W + HY_COLS:PROJ_PAD], preferred_element_type=F32)


def _inproj(xa, mods, g1, w_cat, cos_t, sin_t, qg, kg, bd, n_batch, seq_len, ctx_len):
    t, d = xa.shape
    tm = TOKEN_TILE
    tpb = seq_len // tm
    n_lat = n_batch * tpb
    bpt = tm // ctx_len
    rope_idx = lambda i: (jnp.where(i < n_lat, i % tpb, tpb), 0)
    row = lambda w: pl.BlockSpec((tm, w), lambda i: (i, 0))
    heads = lambda nh: pl.BlockSpec((nh, tm, HEAD_DIM), lambda i: (0, i, 0))
    const = lambda a: pl.BlockSpec(a.shape, lambda i: (0,) * a.ndim)
    lat_tile = lambda i: jnp.minimum(i, n_lat - 1)
    hy_lat = pl.BlockSpec((None, HY_COLS, tm), lambda i: (lat_tile(i) // tpb, 0, lat_tile(i) % tpb))
    hy_ctx = pl.BlockSpec((bpt, HY_COLS, ctx_len), lambda i: (jnp.maximum(i - n_lat, 0), 0, 0))
    return pl.pallas_call(
        functools.partial(_inproj_kernel, n_lat_tiles=n_lat),
        grid=(t // tm,),
        in_specs=[row(d),
                  pl.BlockSpec((1, 6, d), _mod_index(tpb, n_batch)),
                  const(g1), const(w_cat),
                  pl.BlockSpec((tm, LANES), rope_idx), pl.BlockSpec((tm, LANES), rope_idx),
                  const(qg), const(kg), const(bd)],
        out_specs=[heads(ATT_HEADS), pl.BlockSpec((ATT_KV_HEADS, HEAD_DIM, tm), lambda i: (0, 0, i)),
                   heads(ATT_KV_HEADS), row(SSD_WIDTH), row(SSD_XBC_COLS),
                   hy_lat, hy_ctx, row(LANES)],
        out_shape=[jax.ShapeDtypeStruct((ATT_HEADS, t, HEAD_DIM), BF16),
                   jax.ShapeDtypeStruct((ATT_KV_HEADS, HEAD_DIM, t), BF16),
                   jax.ShapeDtypeStruct((ATT_KV_HEADS, t, HEAD_DIM), BF16),
                   jax.ShapeDtypeStruct((t, SSD_WIDTH), F32),
                   jax.ShapeDtypeStruct((t, SSD_XBC_COLS), F32),
                   jax.ShapeDtypeStruct((n_batch, HY_COLS, seq_len), F32),
                   jax.ShapeDtypeStruct((n_batch, HY_COLS, ctx_len), F32),
                   jax.ShapeDtypeStruct((t, LANES), F32)],
        compiler_params=_cparams(1),
        name="inproj",
    )(xa, mods, g1, w_cat, cos_t, sin_t, qg, kg, bd)


def _attn_kernel(sink_ref, q_ref, *refs, n_q, band):
    if band:
        k_ref, v_ref, kc_ref, vc_ref, o_ref, bias_ref = refs
        seq_len = v_ref.shape[1]
        assert n_q >= 3
    else:
        kc_ref, vc_ref, o_ref = refs
    qb = ATT_BLOCK
    rows = ATT_GROUP * qb
    row_id = lax.broadcasted_iota(jnp.int32, (rows, 1), 0)
    nt = (((1,), (1,)), ((), ()))

    if band:
        @pl.when(pl.program_id(0) == 0)
        def _():
            rel0 = (lax.broadcasted_iota(jnp.int32, (rows, band), 1)
                    - lax.broadcasted_iota(jnp.int32, (rows, band), 0) % qb)
            for var in range(3):
                bias_ref[var] = jnp.where(jnp.abs(rel0 - var * WINDOW) <= WINDOW, 0.0, -jnp.inf)

    for j in range(ATT_KV_HEADS):
        kc = kc_ref[j]
        vc = vc_ref[j]
        snk = jnp.zeros((rows, 1), F32)
        for g in range(ATT_GROUP):
            snk = jnp.where(row_id // qb == g, sink_ref[ATT_GROUP * j + g], snk)

        def block(i, carry, j=j, kc=kc, vc=vc, snk=snk):
            q0 = pl.multiple_of(i * qb, qb)
            qh = jnp.concatenate([q_ref[ATT_GROUP * j + g, pl.ds(q0, qb), :] for g in range(ATT_GROUP)],
                                 axis=0)
            s_ctx = jnp.dot(qh, kc, preferred_element_type=F32)
            m = jnp.maximum(jnp.max(s_ctx, axis=-1, keepdims=True), snk)
            if band:
                k0 = pl.multiple_of(jnp.clip(q0 - WINDOW, 0, seq_len - band), qb)
                var = jnp.where(i == 0, 0, jnp.where(i == n_q - 1, 2, 1))
                s_loc = jnp.dot(qh, k_ref[j, :, pl.ds(k0, band)], preferred_element_type=F32) + bias_ref[var]
                m = jnp.maximum(m, jnp.max(s_loc, axis=-1, keepdims=True))
            p_ctx = jnp.exp(s_ctx - m)
            den = jnp.sum(p_ctx, axis=-1, keepdims=True) + jnp.exp(snk - m)
            o = jnp.dot(p_ctx.astype(BF16), vc, preferred_element_type=F32)
            if band:
                p_loc = jnp.exp(s_loc - m)
                den = den + jnp.sum(p_loc, axis=-1, keepdims=True)
                o = o + jnp.dot(p_loc.astype(BF16), v_ref[j, pl.ds(k0, band), :], preferred_element_type=F32)
            o = o / den
            for g in range(ATT_GROUP):
                c0 = (ATT_GROUP * j + g) * HEAD_DIM
                o_ref[pl.ds(q0, qb), c0:c0 + HEAD_DIM] = o[g * qb:(g + 1) * qb, :]
            return carry

        lax.fori_loop(0, n_q, block, 0, unroll=4)


def _attention(sinks, q, k, v, n_batch, seq_len, ctx_len, latent):
    ctx_blk0 = n_batch * seq_len // ctx_len
    kc_spec = pl.BlockSpec((ATT_KV_HEADS, HEAD_DIM, ctx_len), lambda b: (0, 0, ctx_blk0 + b))
    vc_spec = pl.BlockSpec((ATT_KV_HEADS, ctx_len, HEAD_DIM), lambda b: (0, ctx_blk0 + b, 0))
    smem = pl.BlockSpec(memory_space=pltpu.SMEM)
    scratch = []
    if latent:
        rows = seq_len
        band = ATT_BLOCK + 2 * WINDOW
        in_specs = [smem, pl.BlockSpec((ATT_HEADS, rows, HEAD_DIM), lambda b: (0, b, 0)),
                    pl.BlockSpec((ATT_KV_HEADS, HEAD_DIM, rows), lambda b: (0, 0, b)),
                    pl.BlockSpec((ATT_KV_HEADS, rows, HEAD_DIM), lambda b: (0, b, 0)), kc_spec, vc_spec]
        args = (sinks, q, k, v, k, v)
        scratch = [pltpu.VMEM((3, ATT_GROUP * ATT_BLOCK, band), F32)]
    else:
        rows = ctx_len
        band = 0
        in_specs = [smem, pl.BlockSpec((ATT_HEADS, rows, HEAD_DIM), lambda b: (0, ctx_blk0 + b, 0)),
                    kc_spec, vc_spec]
        args = (sinks, q, k, v)
    return pl.pallas_call(
        functools.partial(_attn_kernel, n_q=rows // ATT_BLOCK, band=band),
        grid=(n_batch,),
        in_specs=in_specs,
        out_specs=pl.BlockSpec((rows, ATT_WIDTH), lambda b: (b, 0)),
        out_shape=jax.ShapeDtypeStruct((n_batch * rows, ATT_WIDTH), F32),
        scratch_shapes=scratch,
        compiler_params=_cparams(1),
        name="attn_latent" if latent else "attn_ctx",
    )(*args)


def _merge_kernel(*refs, n_lat_tiles, with_ctx):
    if with_ctx:
        (attl_ref, attc_ref, syl_ref, syc_ref, hyl_ref, hyc_ref, z_ref, x_ref, mod_ref,
         ga_ref, gs_ref, gh_ref, bd_ref, w_ref, o_ref) = refs
        is_lat = pl.program_id(0) < n_lat_tiles
        att = jnp.where(is_lat, attl_ref[...], attc_ref[...])
        sy = jnp.where(is_lat, syl_ref[...], syc_ref[...])
        hy_ctx = jnp.concatenate([hyc_ref[k] for k in range(hyc_ref.shape[0])], axis=1)
        hy_t = jnp.where(is_lat, hyl_ref[...], hy_ctx)
    else:
        attl_ref, syl_ref, hyl_ref, z_ref, x_ref, mod_ref, ga_ref, gs_ref, gh_ref, bd_ref, w_ref, o_ref = refs
        att = attl_ref[...]
        sy = syl_ref[...]
        hy_t = hyl_ref[...]
    a = att * lax.rsqrt(jnp.mean(att * att, axis=-1, keepdims=True) + EPS) * ga_ref[...]
    s = sy * _silu(z_ref[...])
    s = s * lax.rsqrt(jnp.mean(s * s, axis=-1, keepdims=True) + EPS) * gs_ref[...]
    hy = hy_t.T
    hn = hy * lax.rsqrt(_segsum(hy * hy, bd_ref[...]) * (1.0 / (HYENA_WIDTH // HYENA_GROUPS)) + EPS) * gh_ref[...]
    y = jnp.dot(a.astype(BF16), w_ref[0:ATT_WIDTH, :], preferred_element_type=F32)
    y = y + jnp.dot(s.astype(BF16), w_ref[ATT_WIDTH:ATT_WIDTH + SSD_WIDTH, :], preferred_element_type=F32)
    y = y + jnp.dot(hn.astype(BF16), w_ref[ATT_WIDTH + SSD_WIDTH:, :], preferred_element_type=F32)
    o_ref[...] = x_ref[...] + mod_ref[0, 2:3, :] * y


def _merge(att, sy, hy, z, xa, mods, ga, gs, gh, bd, w_out, n_batch, seq_len):
    d = xa.shape[1]
    tm = TOKEN_TILE
    nl = att[0].shape[0] // tm
    with_ctx = att[1] is not None
    t = att[0].shape[0] + (att[1].shape[0] if with_ctx else 0)
    row = lambda w: pl.BlockSpec((tm, w), lambda i: (i, 0))
    lat_row = lambda w: pl.BlockSpec((tm, w), lambda i: (jnp.minimum(i, nl - 1), 0))
    ctx_row = lambda w: pl.BlockSpec((tm, w), lambda i: (jnp.maximum(i - nl, 0), 0))
    const = lambda a: pl.BlockSpec(a.shape, lambda i: (0,) * a.ndim)
    tpb = seq_len // tm
    lat_tile = lambda i: jnp.minimum(i, nl - 1)
    hy_lat = pl.BlockSpec((None, HYENA_WIDTH, tm), lambda i: (lat_tile(i) // tpb, 0, lat_tile(i) % tpb))
    if with_ctx:
        ctx_len = hy[1].shape[2]
        streams = [att[0], att[1], sy[0], sy[1], hy[0], hy[1]]
        specs = [lat_row(ATT_WIDTH), ctx_row(ATT_WIDTH), lat_row(SSD_WIDTH), ctx_row(SSD_WIDTH), hy_lat,
                 pl.BlockSpec((tm // ctx_len, HYENA_WIDTH, ctx_len), lambda i: (jnp.maximum(i - nl, 0), 0, 0))]
    else:
        streams = [att[0], sy[0], hy[0]]
        specs = [row(ATT_WIDTH), row(SSD_WIDTH), hy_lat]
    return pl.pallas_call(
        functools.partial(_merge_kernel, n_lat_tiles=nl, with_ctx=with_ctx),
        grid=(t // tm,),
        in_specs=specs + [row(SSD_WIDTH), row(d),
                          pl.BlockSpec((1, 6, d), _mod_index(seq_len // tm, n_batch)),
                          const(ga), const(gs), const(gh), const(bd), const(w_out)],
        out_specs=row(d),
        out_shape=jax.ShapeDtypeStruct((t, d), F32),
        compiler_params=_cparams(1),
        name="merge_outproj",
    )(*streams, z, xa, mods, ga, gs, gh, bd, w_out)


def _swiglu_accumulate(h, wg_ref, wu_ref, wd_ref, acc_ref, between_chunks=None):
    for c in range(N_FFN_CHUNKS):
        cols = slice(c * FFN_CHUNK, (c + 1) * FFN_CHUNK)
        g = jnp.dot(h, wg_ref[:, cols], preferred_element_type=F32)
        u = jnp.dot(h, wu_ref[:, cols], preferred_element_type=F32)
        a = (_silu(g) * u).astype(BF16)
        part = jnp.dot(a, wd_ref[cols, :], preferred_element_type=F32)
        if c == 0:
            acc_ref[...] = part
        else:
            acc_ref[...] += part
        if between_chunks is not None:
            between_chunks(c)


def _ffn_kernel(x_ref, mod_ref, g2_ref, wg_ref, wu_ref, wd_ref, o_ref, acc_ref):
    x = x_ref[...]
    h = _modnorm(x, g2_ref[...], mod_ref[0, 4:5, :], mod_ref[0, 3:4, :]).astype(BF16)
    _swiglu_accumulate(h, wg_ref, wu_ref, wd_ref, acc_ref)
    o_ref[...] = x + mod_ref[0, 5:6, :] * acc_ref[...]


def _ffn(xa, mods, g2, wg, wu, wd, n_batch, seq_len):
    t, d = xa.shape
    tm = TOKEN_TILE
    row = pl.BlockSpec((tm, d), lambda i: (i, 0))
    resident = lambda a: pl.BlockSpec(a.shape, lambda i: (0,) * a.ndim, pipeline_mode=pl.Buffered(1))
    return pl.pallas_call(
        _ffn_kernel,
        grid=(t // tm,),
        in_specs=[row, pl.BlockSpec((1, 6, d), _mod_index(seq_len // tm, n_batch)),
                  pl.BlockSpec(g2.shape, lambda i: (0, 0)), resident(wg), resident(wu), resident(wd)],
        out_specs=row,
        out_shape=jax.ShapeDtypeStruct((t, d), F32),
        scratch_shapes=[pltpu.VMEM((tm, d), F32)],
        compiler_params=_cparams(1),
        name="ffn",
    )(xa, mods, g2, wg, wu, wd)


def _router_kernel(x_ref, mod_ref, g2_ref, r_ref, h_ref, idx_ref, wt_ref):
    h = _modnorm(x_ref[...], g2_ref[...], mod_ref[0, 4:5, :], mod_ref[0, 3:4, :])
    h_ref[...] = h
    h_hi = h.astype(BF16)
    h_lo = (h - h_hi.astype(F32)).astype(BF16)
    both = jnp.dot(h_hi, r_ref[...], preferred_element_type=F32)
    logits = (both[:, 0:LANES] + both[:, LANES:2 * LANES]
              + jnp.dot(h_lo, r_ref[:, 0:LANES], preferred_element_type=F32))
    lane = lax.broadcasted_iota(jnp.int32, logits.shape, 1)
    neg = -jnp.inf
    l1 = jnp.where(lane < N_EXPERTS, logits, neg)
    m1 = jnp.max(l1, axis=-1, keepdims=True)
    i1 = jnp.min(jnp.where(l1 == m1, lane, LANES), axis=-1, keepdims=True)
    l2 = jnp.where(lane == i1, neg, l1)
    m2 = jnp.max(l2, axis=-1, keepdims=True)
    i2 = jnp.min(jnp.where(l2 == m2, lane, LANES), axis=-1, keepdims=True)
    e = jnp.exp(m2 - m1)
    w1 = 1.0 / (1.0 + e)
    w2 = e / (1.0 + e)
    idx_ref[...] = jnp.where(lane == 0, i1, jnp.where(lane == 1, i2, 0))
    wt_ref[...] = jnp.where(lane == 0, w1, jnp.where(lane == 1, w2, 0.0))


def _router(xa, mods, g2, r_pad, n_rows, n_batch, seq_len):
    d = xa.shape[1]
    tm = TOKEN_TILE
    row = lambda w: pl.BlockSpec((tm, w), lambda i: (i, 0))
    return pl.pallas_call(
        _router_kernel,
        grid=(n_rows // tm,),
        in_specs=[row(d), pl.BlockSpec((1, 6, d), _mod_index(seq_len // tm, n_batch)),
                  pl.BlockSpec(g2.shape, lambda i: (0, 0)), pl.BlockSpec(r_pad.shape, lambda i: (0, 0))],
        out_specs=[row(d), row(LANES), row(LANES)],
        out_shape=[jax.ShapeDtypeStruct((n_rows, d), F32),
                   jax.ShapeDtypeStruct((n_rows, LANES), jnp.int32),
                   jax.ShapeDtypeStruct((n_rows, LANES), F32)],
        compiler_params=_cparams(1),
        name="moe_router",
    )(xa, mods, g2, r_pad)


def _row_copy(src, src_row, dst, dst_row, sem):
    return pltpu.make_async_copy(src.at[pl.ds(src_row, 1), :], dst.at[pl.ds(dst_row, 1), :], sem)


DMA_ISSUE_UNROLL = 8


def _idx_copy(dest_hbm, dest_smem, sem_idx, tile, slot):
    n = dest_hbm.shape[1]
    half = dest_smem.at[pl.ds(pl.multiple_of(slot * n, n), n)]
    return pltpu.make_async_copy(dest_hbm.at[tile], half, sem_idx.at[slot])


def _expert_kernel(te_ref, nused_ref, src_hbm, h_hbm, wg_ref, wu_ref, wd_ref, o_ref,
                   acc_ref, xs_even, xs_odd, src_smem, sem_idx, sem_rows):
    del te_ref
    i = pl.program_id(0)
    n = pl.num_programs(0)
    tm = o_ref.shape[0]
    live = i < nused_ref[0]
    per_chunk = -(-tm // N_FFN_CHUNKS)
    bufs = (xs_even, xs_odd)

    def idx_copy(tile, half):
        return _idx_copy(src_hbm, src_smem, sem_idx, jnp.minimum(tile, n - 1), half)

    def row_gather(half, r):
        return _row_copy(h_hbm, src_smem[half * tm + r], bufs[half], r, sem_rows.at[half])

    def rows_done(half):
        return pltpu.make_async_copy(h_hbm.at[pl.ds(0, tm), :], bufs[half], sem_rows.at[half])

    def issue_all(half):
        def issue(r, carry):
            row_gather(half, r).start()
            return carry

        lax.fori_loop(0, tm, issue, 0, unroll=DMA_ISSUE_UNROLL)

    @pl.when(i == 0)
    def _():
        first = idx_copy(0, 0)
        first.start()
        first.wait()
        issue_all(0)
        idx_copy(1, 1).start()

    def step(cur, nxt):
        rows_done(cur).wait()
        idx_copy(i + 1, nxt).wait()
        idx_copy(i + 2, cur).start()

        @pl.when(live)
        def _():
            def issue_next(c):
                for r in range(c * per_chunk, min((c + 1) * per_chunk, tm)):
                    row_gather(nxt, r).start()

            _swiglu_accumulate(bufs[cur][...].astype(BF16), wg_ref, wu_ref, wd_ref, acc_ref, issue_next)
            o_ref[...] = acc_ref[...]

        @pl.when(jnp.logical_not(live))
        def _():
            issue_all(nxt)
            o_ref[...] = jnp.zeros_like(o_ref)

        @pl.when(i == n - 1)
        def _():
            rows_done(nxt).wait()
            idx_copy(i + 2, cur).wait()

    for parity in range(2):
        @pl.when(i % 2 == parity)
        def _(parity=parity):
            step(parity, 1 - parity)


def _experts(tile_expert, n_used, src, h, wg, wu, wd):
    n_tiles, tm = src.shape
    d = h.shape[1]
    row = pl.BlockSpec((tm, d), lambda i, te, nu: (i, 0))
    wspec = lambda a: pl.BlockSpec((None,) + a.shape[1:], lambda i, te, nu: (te[i], 0, 0))
    hbm = pl.BlockSpec(memory_space=pl.ANY)
    return pl.pallas_call(
        _expert_kernel,
        grid_spec=pltpu.PrefetchScalarGridSpec(
            num_scalar_prefetch=2,
            grid=(n_tiles,),
            in_specs=[hbm, hbm, wspec(wg), wspec(wu), wspec(wd)],
            out_specs=row,
            scratch_shapes=[pltpu.VMEM((tm, d), F32), pltpu.VMEM((tm, d), F32), pltpu.VMEM((tm, d), F32),
                            pltpu.SMEM((2 * tm,), jnp.int32), pltpu.SemaphoreType.DMA((2,)),
                            pltpu.SemaphoreType.DMA((2,))]),
        out_shape=jax.ShapeDtypeStruct((n_tiles * tm, d), F32),
        compiler_params=_cparams(1),
        name="moe_experts",
    )(tile_expert, n_used, src, h, wg, wu, wd)


def _combine_kernel(dest_hbm, eo_hbm, x_ref, wt_ref, mod_ref, o_ref, dest_smem, buf, sem_idx, sem_rows):
    i = pl.program_id(0)
    n = pl.num_programs(0)
    tm = x_ref.shape[0]
    slot = i % 2

    def gather(s):
        base = s * (2 * tm)

        def issue(r, carry):
            _row_copy(eo_hbm, dest_smem[base + 2 * r], buf.at[s, 0], r, sem_rows.at[s]).start()
            _row_copy(eo_hbm, dest_smem[base + 2 * r + 1], buf.at[s, 1], r, sem_rows.at[s]).start()
            return carry

        lax.fori_loop(0, tm, issue, 0, unroll=DMA_ISSUE_UNROLL)

    @pl.when(i == 0)
    def _():
        first = _idx_copy(dest_hbm, dest_smem, sem_idx, 0, 0)
        first.start()
        first.wait()
        gather(0)

        @pl.when(n > 1)
        def _():
            _idx_copy(dest_hbm, dest_smem, sem_idx, 1, 1).start()

    @pl.when(i + 1 < n)
    def _():
        _idx_copy(dest_hbm, dest_smem, sem_idx, i + 1, 1 - slot).wait()

        @pl.when(i + 2 < n)
        def _():
            _idx_copy(dest_hbm, dest_smem, sem_idx, i + 2, slot).start()

        gather(1 - slot)

    for k in range(2):
        pltpu.make_async_copy(eo_hbm.at[pl.ds(0, tm), :], buf.at[slot, k], sem_rows.at[slot]).wait()
    wt = wt_ref[...]
    y = wt[:, 0:1] * buf[slot, 0] + wt[:, 1:2] * buf[slot, 1]
    o_ref[...] = x_ref[...] + mod_ref[0, 5:6, :] * y


def _combine(dest, eo, xa, wts, mods, n_rows, n_batch, seq_len):
    d = xa.shape[1]
    tm = TOKEN_TILE
    row = lambda w: pl.BlockSpec((tm, w), lambda i: (i, 0))
    return pl.pallas_call(
        _combine_kernel,
        grid=(n_rows // tm,),
        in_specs=[pl.BlockSpec(memory_space=pl.ANY), pl.BlockSpec(memory_space=pl.ANY), row(d), row(LANES),
                  pl.BlockSpec((1, 6, d), _mod_index(seq_len // tm, n_batch))],
        out_specs=row(d),
        out_shape=jax.ShapeDtypeStruct((n_rows, d), F32),
        scratch_shapes=[pltpu.SMEM((4 * tm,), jnp.int32), pltpu.VMEM((2, 2, tm, d), F32),
                        pltpu.SemaphoreType.DMA((2,)), pltpu.SemaphoreType.DMA((2,))],
        compiler_params=_cparams(1),
        name="moe_combine",
    )(dest.reshape(n_rows // tm, 2 * tm), eo, xa, wts, mods)


def _moe(xa, mods, g2, r_pad, wg, wu, wd, n_rows, n_batch, seq_len):
    h, idx, wts = _router(xa, mods, g2, r_pad, n_rows, n_batch, seq_len)
    tm = EXPERT_TILE
    e_flat = idx[:, :2].reshape(-1)
    onehot = (e_flat[:, None] == jnp.arange(N_EXPERTS, dtype=jnp.int32)[None, :]).astype(jnp.int32)
    csum = jnp.cumsum(onehot, axis=0)
    counts = csum[-1]
    rank = jnp.sum(onehot * csum, axis=1) - 1
    padded = ((counts + tm - 1) // tm) * tm
    ends = jnp.cumsum(padded)
    starts = ends - padded
    dest = (jnp.sum(onehot * starts[None, :], axis=1) + rank).astype(jnp.int32)
    n_slots = 2 * n_rows + N_EXPERTS * tm
    tile_start = jnp.arange(n_slots // tm, dtype=jnp.int32) * tm
    tile_expert = jnp.minimum(jnp.sum((tile_start[:, None] >= ends[None, :]).astype(jnp.int32), axis=1),
                              N_EXPERTS - 1).astype(jnp.int32)
    n_used = (ends[-1:] // tm).astype(jnp.int32)
    src = jnp.zeros((n_slots,), jnp.int32).at[dest].set(jnp.arange(2 * n_rows, dtype=jnp.int32) // 2)
    eo = _experts(tile_expert, n_used, src.reshape(n_slots // tm, tm), h, wg, wu, wd)
    return _combine(dest, eo, xa, wts, mods, n_rows, n_batch, seq_len)


def _softplus(v):
    return jnp.maximum(v, 0.0) + jnp.log1p(jnp.exp(-jnp.abs(v)))


def _ssd_kernel(xl_ref, dl_ref, xc_ref, dc_ref, cw_ref, cb_ref, dtb_ref, alogc_ref, dsk_ref, *rest, want_ctx):
    if want_ctx:
        yl_ref, yc_ref, xs_l, dtt_l, xs_c, dtt_c, st_ref = rest
    else:
        yl_ref, xs_l, dtt_l, xs_c, dtt_c, st_ref = rest
        yc_ref = None
    ck = SSD_CHUNK
    hp = SSD_HEAD_DIM
    ns = SSD_STATE
    n_col = 2 * SSD_HEADS
    hi = lax.Precision.HIGHEST
    row = lax.broadcasted_iota(jnp.int32, (ck, 1), 0)
    li = lax.broadcasted_iota(jnp.int32, (ck, ck), 0)
    si = lax.broadcasted_iota(jnp.int32, (ck, ck), 1)
    masks = (si <= li, si >= li)
    tris = (masks[0].astype(F32), masks[1].astype(F32))
    a_col = -jnp.exp(alogc_ref[...])
    dskip = dsk_ref[...]

    def prep(raw_ref, dtraw_ref, xs_s, dtt_s, y_ref):
        n = raw_ref.shape[0]
        nk = n // ck

        def body(k, carry):
            r0 = pl.multiple_of(k * ck, ck)
            a = raw_ref[pl.ds(r0, ck), :]
            top = raw_ref[pl.ds(pl.multiple_of(jnp.maximum(r0 - 8, 0), 8), 8), :][7:8, :]
            bot = raw_ref[pl.ds(pl.multiple_of(jnp.minimum(r0 + ck, n - 8), 8), 8), :][0:1, :]
            top = jnp.where(k > 0, top, 0.0)
            bot = jnp.where(k < nk - 1, bot, 0.0)
            prev = jnp.where(row == 0, top, pltpu.roll(a, 1, 0))
            nxt = jnp.where(row == ck - 1, bot, pltpu.roll(a, ck - 1, 0))
            xs = _silu(prev * cw_ref[0:1, :] + a * cw_ref[1:2, :] + nxt * cw_ref[2:3, :] + cb_ref[...])
            xs_s[pl.ds(r0, ck), :] = xs
            dt = _softplus(dtraw_ref[pl.ds(r0, ck), :] + dtb_ref[...])
            dtt_s[:, pl.ds(r0, ck)] = dt.T[0:n_col, :]
            if y_ref is not None:
                y_ref[pl.ds(r0, ck), :] = xs[:, 0:SSD_WIDTH] * dskip
            return carry

        lax.fori_loop(0, nk, body, 0)

    def run(xs_s, dtt_s, y_ref):
        nk = xs_s.shape[0] // ck

        def one(kk, dr):
            r0 = pl.multiple_of(kk * ck, ck)
            xc = xs_s[pl.ds(r0, ck), :]
            dtr = dtt_s[:, pl.ds(r0, ck)]
            dta_r = dtr * a_col
            la_r = jnp.dot(dta_r, tris[1 - dr], precision=hi, preferred_element_type=F32)
            la_end = la_r[:, ck - 1:ck] if dr == 0 else la_r[:, 0:1]
            if y_ref is not None:
                la_c = jnp.concatenate([la_r, jnp.zeros((ck - n_col, ck), F32)], axis=0).T
            ys = []
            for g in range(SSD_GROUPS):
                b_t = xc[:, SSD_WIDTH + g * ns:SSD_WIDTH + (g + 1) * ns].T
                c0 = SSD_WIDTH + SSD_GROUPS * ns + g * ns
                c_g = xc[:, c0:c0 + ns].astype(BF16)
                if y_ref is not None:
                    scores = jnp.dot(c_g, b_t.astype(BF16), preferred_element_type=F32)
                for hh in range(SSD_HEADS // SSD_GROUPS):
                    h = g * (SSD_HEADS // SSD_GROUPS) + hh
                    col = dr * SSD_HEADS + h
                    xh = xc[:, h * hp:(h + 1) * hp].astype(BF16)
                    dt_row = dtr[col:col + 1, :]
                    le = la_end[col:col + 1, :]
                    st = st_ref[col]
                    if y_ref is not None:
                        la_col = la_c[:, col:col + 1]
                        decay = jnp.exp(jnp.where(masks[dr], la_col - la_r[col:col + 1, :], -jnp.inf))
                        y = jnp.dot((scores * decay * dt_row).astype(BF16), xh, preferred_element_type=F32)
                        y = y + jnp.dot(c_g, st.astype(BF16), preferred_element_type=F32) * jnp.exp(la_col)
                        ys.append(y)
                    bw = (b_t * (dt_row * jnp.exp(le - la_r[col:col + 1, :]))).astype(BF16)
                    st_ref[col] = st * jnp.exp(le) + jnp.dot(bw, xh, preferred_element_type=F32)
            if y_ref is not None:
                y_ref[pl.ds(r0, ck), :] += jnp.concatenate(ys, axis=1)

        def body(k, carry):
            one(k, 0)
            one(nk - 1 - k, 1)
            return carry

        lax.fori_loop(0, nk, body, 0, unroll=2)

    prep(xl_ref, dl_ref, xs_l, dtt_l, yl_ref)
    prep(xc_ref, dc_ref, xs_c, dtt_c, yc_ref)
    st_ref[...] = jnp.zeros_like(st_ref)
    run(xs_c, dtt_c, yc_ref)
    run(xs_l, dtt_l, yl_ref)


def _ssd(xbc, dtp, conv_w, conv_b, dt_bias, a_log, d_skip, n_batch, seq_len, ctx_len, want_ctx):
    ctx0 = n_batch * seq_len // ctx_len
    pad = lambda v: jnp.pad(v.reshape(1, -1), ((0, 0), (0, LANES - v.size)))
    lat = lambda w: pl.BlockSpec((seq_len, w), lambda b: (b, 0))
    ctx = lambda w: pl.BlockSpec((ctx_len, w), lambda b: (ctx0 + b, 0))
    const = lambda a: pl.BlockSpec(a.shape, lambda b: (0,) * a.ndim)
    consts = (conv_w, conv_b.reshape(1, -1), pad(dt_bias), a_log.reshape(-1, 1),
              jnp.repeat(d_skip, SSD_HEAD_DIM).reshape(1, -1))
    out_specs = [lat(SSD_WIDTH)]
    out_shape = [jax.ShapeDtypeStruct((n_batch * seq_len, SSD_WIDTH), F32)]
    if want_ctx:
        out_specs.append(pl.BlockSpec((ctx_len, SSD_WIDTH), lambda b: (b, 0)))
        out_shape.append(jax.ShapeDtypeStruct((n_batch * ctx_len, SSD_WIDTH), F32))
    return pl.pallas_call(
        functools.partial(_ssd_kernel, want_ctx=want_ctx),
        grid=(n_batch,),
        in_specs=[lat(SSD_XBC_COLS), lat(LANES), ctx(SSD_XBC_COLS), ctx(LANES)] + [const(a) for a in consts],
        out_specs=out_specs,
        out_shape=out_shape,
        scratch_shapes=[pltpu.VMEM((seq_len, SSD_XBC_COLS), F32), pltpu.VMEM((2 * SSD_HEADS, seq_len), F32),
                        pltpu.VMEM((ctx_len, SSD_XBC_COLS), F32), pltpu.VMEM((2 * SSD_HEADS, ctx_len), F32),
                        pltpu.VMEM((2 * SSD_HEADS, SSD_STATE, SSD_HEAD_DIM), F32)],
        compiler_params=_cparams(1),
        name="ssd",
    )(xbc, dtp, xbc, dtp, *consts)


HY_BLOCK = 256
HY_CH_STEP = 8


def _hyena_tables(seq_len):
    f32 = np.float32
    nj = 2 * seq_len
    lag = np.arange(nj, dtype=np.int32) - seq_len
    dist = np.abs(lag)
    pos = np.minimum(dist, seq_len - 1)
    t = np.linspace(0.0, 1.0, seq_len, dtype=f32)[pos]
    w = (f32(2.0 * math.pi / seq_len) * np.arange(seq_len, dtype=f32))[pos]
    bands = (HYENA_POS_DIM - 1) // 2
    freqs = np.linspace(1e-4, bands - 1, bands, dtype=f32)[None, :]
    ang = (freqs * w[:, None]).astype(f32)
    z = np.concatenate([t[:, None], np.cos(ang), -np.sin(ang)], axis=-1).astype(f32)
    zt = np.pad(z.T, ((0, (-HYENA_POS_DIM) % 8), (0, 0)))
    deltas = np.abs(np.linspace(math.log(HYENA_DECAY_TARGET) / HYENA_SLOW_DECAY,
                                math.log(HYENA_DECAY_TARGET) / HYENA_FAST_DECAY, HYENA_WIDTH, dtype=f32))
    dec = (np.exp(-t[None, :] * deltas[:, None]) * (dist < seq_len)[None, :]).astype(f32)
    fwd = (lag >= 0).astype(f32)[None, :]
    return jnp.asarray(zt), jnp.asarray(dec), jnp.asarray(fwd)


def _hyfilt_kernel(zt_ref, dec_ref, fwd_ref, w1_ref, b1_ref, f1_ref, w2_ref, b2_ref, f2_ref, w3_ref, o_ref):
    hi = lax.Precision.HIGHEST
    h = jnp.sin(f1_ref[...] * (jnp.dot(w1_ref[...], zt_ref[...], precision=hi, preferred_element_type=F32)
                               + b1_ref[...]))
    h = jnp.sin(f2_ref[...] * (jnp.dot(w2_ref[...], h, precision=hi, preferred_element_type=F32) + b2_ref[...]))
    hw = jnp.dot(w3_ref[...], h, precision=hi, preferred_element_type=F32)
    fwd = fwd_ref[...] > 0.5
    dec = dec_ref[...]
    nw = HYENA_WIDTH
    for o in range(HYENA_ORDER):
        o_ref[o] = jnp.where(fwd, hw[o * nw:(o + 1) * nw], hw[(HYENA_ORDER + o) * nw:(HYENA_ORDER + o + 1) * nw]) * dec


def _hyena_filters(tables, w1, b1, f1, w2, b2, f2, w3):
    zt, dec, fwd = tables
    nj = zt.shape[1]
    tj = 512
    col = lambda v: v.reshape(-1, 1)
    w1t = jnp.pad(w1.T, ((0, 0), (0, zt.shape[0] - w1.shape[0])))
    consts = (w1t, col(b1), col(f1), w2.T, col(b2), col(f2), w3.T)
    lanes = lambda a: pl.BlockSpec((a.shape[0], tj), lambda j: (0, j))
    const = lambda a: pl.BlockSpec(a.shape, lambda j: (0, 0))
    return pl.pallas_call(
        _hyfilt_kernel,
        grid=(nj // tj,),
        in_specs=[lanes(zt), lanes(dec), lanes(fwd)] + [const(a) for a in consts],
        out_specs=pl.BlockSpec((HYENA_ORDER, HYENA_WIDTH, tj), lambda j: (0, 0, j)),
        out_shape=jax.ShapeDtypeStruct((HYENA_ORDER, HYENA_WIDTH, nj), F32),
        compiler_params=_cparams(1),
        name="hyena_filters",
    )(zt, dec, fwd, *consts)


def _hyconv_kernel(cw_ref, cb_ref, hb_ref, v_ref, x1_ref, x2_ref, kf_ref, o_ref):
    n_b, n_ch, seq_len = v_ref.shape
    nb = seq_len // HY_BLOCK
    blk = HY_BLOCK
    c_base = pl.program_id(0) * n_ch
    lane = lax.broadcasted_iota(jnp.int32, (1, seq_len), 1)

    def sconv(x, ch):
        prev = jnp.where(lane == 0, 0.0, pltpu.roll(x, 1, 1))
        nxt = jnp.where(lane == seq_len - 1, 0.0, pltpu.roll(x, seq_len - 1, 1))
        return prev * cw_ref[0, ch] + x * cw_ref[1, ch] + nxt * cw_ref[2, ch] + cb_ref[ch]

    def long_conv(vals, kf_row):
        skew = pltpu.roll(jnp.broadcast_to(kf_row, (blk, 2 * seq_len)), 0, 1, stride=1, stride_axis=0)
        vb = vals.astype(BF16)
        acc = [None] * nb
        for d in range(-(nb - 1), nb):
            tt = skew[:, seq_len + d * blk:seq_len + (d + 1) * blk].astype(BF16)
            sis = list(range(max(0, -d), min(nb, nb - d)))
            lhs = [vb[:, s * blk:(s + 1) * blk] for s in sis]
            lhs = lhs[0] if len(lhs) == 1 else jnp.concatenate(lhs, axis=0)
            out = jnp.dot(lhs, tt, preferred_element_type=F32)
            for idx, s in enumerate(sis):
                piece = out[idx * n_b:(idx + 1) * n_b]
                acc[s + d] = piece if acc[s + d] is None else acc[s + d] + piece
        return acc[0] if nb == 1 else jnp.concatenate(acc, axis=1)

    def channel(cc, carry):
        ch = c_base + cc
        v = sconv(v_ref[:, cc, :], ch)
        x1 = sconv(x1_ref[:, cc, :], HYENA_WIDTH + ch)
        x2 = sconv(x2_ref[:, cc, :], 2 * HYENA_WIDTH + ch)
        z = x1 * (long_conv(v, kf_ref[0, pl.ds(cc, 1), :]) + v * hb_ref[0, ch])
        o_ref[:, cc, :] = x2 * (long_conv(z, kf_ref[1, pl.ds(cc, 1), :]) + z * hb_ref[1, ch])
        return carry

    lax.fori_loop(0, n_ch, channel, 0)


def _hyena_conv(hy, kf, conv_w, conv_b, hy_bias):
    n_batch, _, seq_len = hy.shape
    cs = HY_CH_STEP
    nw = HYENA_WIDTH
    stream = lambda k: pl.BlockSpec((n_batch, cs, seq_len), lambda c: (0, k * (nw // cs) + c, 0))
    smem = pl.BlockSpec(memory_space=pltpu.SMEM)
    return pl.pallas_call(
        _hyconv_kernel,
        grid=(nw // cs,),
        in_specs=[smem, smem, smem, stream(0), stream(1), stream(2),
                  pl.BlockSpec((HYENA_ORDER, cs, 2 * seq_len), lambda c: (0, c, 0))],
        out_specs=pl.BlockSpec((n_batch, cs, seq_len), lambda c: (0, c, 0)),
        out_shape=jax.ShapeDtypeStruct((n_batch, nw, seq_len), F32),
        compiler_params=_cparams(1),
        name="hyena_conv",
    )(conv_w, conv_b, hy_bias, hy, hy, hy, kf)


def _rope_tables(seq_len, extra):
    rows = seq_len // GRID_W
    row = jnp.repeat(jnp.arange(rows, dtype=F32), GRID_W)
    col = jnp.tile(jnp.arange(GRID_W, dtype=F32), rows)
    inv = ROPE_THETA ** (-jnp.arange(0, ROPE_AXIS_DIM, 2, dtype=F32) / ROPE_AXIS_DIM)
    ang = jnp.stack([row[:, None] * inv, col[:, None] * inv], axis=1)
    cos = jnp.cos(ang)
    sin = jnp.sin(ang)
    cos_h = jnp.concatenate([cos, cos], axis=-1).reshape(seq_len, HEAD_DIM)
    sin_h = jnp.concatenate([-sin, sin], axis=-1).reshape(seq_len, HEAD_DIM)
    cos_t = jnp.concatenate([jnp.tile(cos_h, (1, LANES // HEAD_DIM)), jnp.ones((extra, LANES), F32)], axis=0)
    sin_t = jnp.concatenate([jnp.tile(sin_h, (1, LANES // HEAD_DIM)), jnp.zeros((extra, LANES), F32)], axis=0)
    return cos_t, sin_t


def _block_diag_ones(n, seg):
    i = jnp.arange(n) // seg
    return (i[:, None] == i[None, :]).astype(BF16)


def kernel(x, c, ctx, c_ctx, w_ada, b_ada, norm1, norm2, w_in, w_out, q_norm, k_norm, att_sinks, att_out_norm, ssd_conv_w, ssd_conv_b, ssd_dt_bias, ssd_a_log, ssd_d, ssd_norm, hy_conv_w, hy_conv_b, hy_w1, hy_b1, hy_f1, hy_w2, hy_b2, hy_f2, hy_w3, hy_bias, hy_out_norm, ffn_w_gate, ffn_w_up, ffn_w_down, moe_router, moe_w_gate, moe_w_up, moe_w_down):
    n_batch, seq_len, d = x.shape
    ctx_len = ctx.shape[1]
    n_lat = n_batch * seq_len
    n_ctx = n_batch * ctx_len
    depth = w_in.shape[0]
    xa = jnp.concatenate([x.reshape(n_lat, d), ctx.reshape(n_ctx, d)], axis=0)

    cc = jnp.concatenate([c, c_ctx[None, :]], axis=0)
    pad_rows = (-cc.shape[0]) % 8
    cc = jnp.pad(cc, ((0, pad_rows), (0, 0)))
    mods_all = _adaln(cc, w_ada, b_ada)[:, :n_batch + 1].reshape(depth, n_batch + 1, 6, d)

    cos_t, sin_t = _rope_tables(seq_len, TOKEN_TILE)
    bd_q = _block_diag_ones(Q_COLS, HEAD_DIM)
    bd_h = _block_diag_ones(HYENA_WIDTH, HYENA_WIDTH // HYENA_GROUPS)
    hy_tab_l = _hyena_tables(seq_len)
    hy_tab_c = _hyena_tables(ctx_len)

    for i in range(depth):
        last = i == depth - 1
        j = i // 2
        mods = mods_all[i]
        wi = w_in[i]
        c_dt = QKV_W + ZX_W
        w_cat = jnp.concatenate([wi[:, :c_dt], wi[:, c_dt + SSD_DT_COLS:], wi[:, c_dt:c_dt + SSD_DT_COLS],
                                 jnp.zeros((d, LANES - SSD_DT_COLS), F32)], axis=1).astype(BF16)
        qg = jnp.tile(q_norm[i], Q_COLS // HEAD_DIM)[None, :]
        kg = jnp.tile(k_norm[i], KV_COLS // HEAD_DIM)[None, :]
        q, k, v, z, xbc, hy_l, hy_c, dtp = _inproj(xa, mods, norm1[i][None, :], w_cat, cos_t, sin_t, qg, kg,
                                                   bd_q, n_batch, seq_len, ctx_len)

        att_l = _attention(att_sinks[i], q, k, v, n_batch, seq_len, ctx_len, True)
        ssd_out = _ssd(xbc, dtp, ssd_conv_w[i], ssd_conv_b[i], ssd_dt_bias[i], ssd_a_log[i], ssd_d[i],
                       n_batch, seq_len, ctx_len, not last)
        filt = (hy_w1[i], hy_b1[i], hy_f1[i], hy_w2[i], hy_b2[i], hy_f2[i], hy_w3[i])
        hyo_l = _hyena_conv(hy_l, _hyena_filters(hy_tab_l, *filt), hy_conv_w[i], hy_conv_b[i], hy_bias[i])
        if last:
            att = (att_l, None)
            sy = (ssd_out[0], None)
            hyo = (hyo_l, None)
            n_rows = n_lat
        else:
            att = (att_l, _attention(att_sinks[i], q, k, v, n_batch, seq_len, ctx_len, False))
            sy = tuple(ssd_out)
            hyo_c = _hyena_conv(hy_c, _hyena_filters(hy_tab_c, *filt), hy_conv_w[i], hy_conv_b[i], hy_bias[i])
            hyo = (hyo_l, hyo_c)
            n_rows = n_lat + n_ctx
        xa = _merge(att, sy, hyo, z, xa, mods, att_out_norm[i][None, :], ssd_norm[i][None, :],
                    hy_out_norm[i][None, :], bd_h, w_out[i].astype(BF16), n_batch, seq_len)

        g2 = norm2[i][None, :]
        if i % 2 == 0:
            xa = _ffn(xa, mods, g2, ffn_w_gate[j].astype(BF16), ffn_w_up[j].astype(BF16),
                      ffn_w_down[j].astype(BF16), n_batch, seq_len)
        else:
            r_full = jnp.pad(moe_router[j], ((0, 0), (0, LANES - N_EXPERTS)))
            r_hi = r_full.astype(BF16)
            r_pad = jnp.concatenate([r_hi, (r_full - r_hi.astype(F32)).astype(BF16)], axis=1)
            xa = _moe(xa, mods, g2, r_pad, moe_w_gate[j].astype(BF16), moe_w_up[j].astype(BF16),
                      moe_w_down[j].astype(BF16), n_rows, n_batch, seq_len)
    return xa[:n_lat].reshape(n_batch, seq_len, d)
```

```python
import functools
import math

import jax
import jax.numpy as jnp
import numpy as np
from jax import lax
from jax.experimental import pallas as pl
from jax.experimental.pallas import tpu as pltpu

F32 = jnp.float32
BF16 = jnp.bfloat16

D_MODEL = 1024
DEPTH = 4
GRID_W = 64
EPS = 1e-6
HEAD_DIM = 64
ATT_WIDTH = 512
ATT_HEADS = 8
ATT_KV_HEADS = 2
ATT_GROUP = 4
WINDOW = 128
ATT_BLOCK = 128
ROPE_THETA = 10000.0
ROPE_AXIS_DIM = 32
SSD_WIDTH = 256
SSD_HEAD_DIM = 64
SSD_HEADS = 4
SSD_STATE = 64
SSD_GROUPS = 2
SSD_CHUNK = 128
HYENA_WIDTH = 256
HYENA_GROUPS = 4
HYENA_ORDER = 2
HYENA_POS_DIM = 33
HYENA_FAST_DECAY = 0.3
HYENA_SLOW_DECAY = 1.5
HYENA_DECAY_TARGET = 1e-2
Q_COLS = 512
KV_COLS = 128
SSD_XBC_COLS = 512
SSD_DT_COLS = 8
HY_COLS = 768
FFN_DIM = 2816
N_EXPERTS = 8
FFN_CHUNK = 256
N_FFN_CHUNKS = FFN_DIM // FFN_CHUNK
LANES = 128
QKV_W = Q_COLS + 2 * KV_COLS
ZX_W = SSD_WIDTH + SSD_XBC_COLS
PROJ_PAD = QKV_W + ZX_W + LANES
VMEM_LIMIT = 56 * 1024 * 1024
TOKEN_TILE = 512
EXPERT_TILE = 512


def _cparams(n_axes):
    return pltpu.CompilerParams(dimension_semantics=("arbitrary",) * n_axes,
                                vmem_limit_bytes=VMEM_LIMIT)


def _silu(v):
    return v / (1.0 + jnp.exp(-v))


def _modnorm(x, g, scale, shift):
    ms = jnp.mean(x * x, axis=-1, keepdims=True)
    return x * lax.rsqrt(ms + EPS) * g * (1.0 + scale) + shift


def _segsum(t, bd):
    hi = t.astype(BF16)
    lo = (t - hi.astype(F32)).astype(BF16)
    return (jnp.dot(hi, bd, preferred_element_type=F32)
            + jnp.dot(lo, bd, preferred_element_type=F32))


def _mod_index(tiles_per_batch, n_batch):
    return lambda i: (jnp.minimum(i // tiles_per_batch, n_batch), 0, 0)


def _adaln_kernel(c_ref, w_ref, b_ref, o_ref):
    s = _silu(c_ref[...]).astype(BF16)
    o_ref[...] = jnp.dot(s, w_ref[...].astype(BF16), preferred_element_type=F32) + b_ref[...]


def _adaln(cc, w_ada, b_ada):
    depth, d, n = w_ada.shape
    r = cc.shape[0]
    tn = 512
    return pl.pallas_call(
        _adaln_kernel,
        grid=(depth, n // tn),
        in_specs=[pl.BlockSpec((r, d), lambda l, j: (0, 0)),
                  pl.BlockSpec((None, d, tn), lambda l, j: (l, 0, j)),
                  pl.BlockSpec((None, 1, tn), lambda l, j: (l, 0, j))],
        out_specs=pl.BlockSpec((None, r, tn), lambda l, j: (l, 0, j)),
        out_shape=jax.ShapeDtypeStruct((depth, r, n), F32),
        compiler_params=_cparams(2),
        name="adaln",
    )(cc, w_ada, b_ada.reshape(depth, 1, n))


def _inproj_kernel(x_ref, mod_ref, g1_ref, w_ref, wh_ref, cos_ref, sin_ref, qg_ref, kg_ref, bd_ref,
                   q_ref, k_ref, v_ref, z_ref, xbc_ref, hyl_ref, hyc_ref, dt_ref, *, n_lat_tiles):
    x = x_ref[...]
    h = _modnorm(x, g1_ref[...], mod_ref[0, 1:2, :], mod_ref[0, 0:1, :]).astype(BF16)
    pq = jnp.dot(h, w_ref[:, 0:QKV_W], preferred_element_type=F32)
    cos = cos_ref[...]
    sin = sin_ref[...]
    lane = lax.broadcasted_iota(jnp.int32, (1, LANES), 1)
    first_half = (lane % 32) < 16

    def rope(t):
        partner = jnp.where(first_half, pltpu.roll(t, LANES - 16, 1), pltpu.roll(t, 16, 1))
        return t * cos + partner * sin

    q = pq[:, 0:Q_COLS]
    qn = q * lax.rsqrt(_segsum(q * q, bd_ref[...]) * (1.0 / HEAD_DIM) + EPS) * qg_ref[...]
    scale = HEAD_DIM ** -0.5
    for j in range(Q_COLS // LANES):
        pair = (rope(qn[:, LANES * j:LANES * (j + 1)]) * scale).astype(BF16)
        q_ref[2 * j] = pair[:, 0:HEAD_DIM]
        q_ref[2 * j + 1] = pair[:, HEAD_DIM:LANES]
    k = pq[:, Q_COLS:Q_COLS + KV_COLS]
    kn = k * lax.rsqrt(_segsum(k * k, bd_ref[0:KV_COLS, 0:KV_COLS]) * (1.0 / HEAD_DIM) + EPS) * kg_ref[...]
    kt = rope(kn).T.astype(BF16)
    vv = pq[:, Q_COLS + KV_COLS:QKV_W].astype(BF16)
    for j in range(ATT_KV_HEADS):
        k_ref[j] = kt[j * HEAD_DIM:(j + 1) * HEAD_DIM, :]
        v_ref[j] = vv[:, j * HEAD_DIM:(j + 1) * HEAD_DIM]
    zx = jnp.dot(h, w_ref[:, QKV_W:QKV_W + ZX_W], preferred_element_type=F32)
    z_ref[...] = zx[:, 0:SSD_WIDTH]
    xbc_ref[...] = zx[:, SSD_WIDTH:ZX_W]
    hy_t = lax.dot_general(wh_ref[...], h, (((1,), (1,)), ((), ())), preferred_element_type=F32)
    is_lat = pl.program_id(0) < n_lat_tiles

    @pl.when(is_lat)
    def _():
        hyl_ref[...] = hy_t

    @pl.when(jnp.logical_not(is_lat))
    def _():
        ctx_len = hyc_ref.shape[2]
        for k in range(hyc_ref.shape[0]):
            hyc_ref[k] = hy_t[:, k * ctx_len:(k + 1) * ctx_len]
    dt_ref[...] = jnp.dot(h, w_ref[:, QKV_W + ZX_W:PROJ_PAD], preferred_element_type=F32)


def _inproj(xa, mods, g1, w_cat, w_hy_t, cos_t, sin_t, qg, kg, bd, n_batch, seq_len, ctx_len):
    t, d = xa.shape
    tm = TOKEN_TILE
    tpb = seq_len // tm
    n_lat = n_batch * tpb
    bpt = tm // ctx_len
    rope_idx = lambda i: (jnp.where(i < n_lat, i % tpb, tpb), 0)
    row = lambda w: pl.BlockSpec((tm, w), lambda i: (i, 0))
    heads = lambda nh: pl.BlockSpec((nh, tm, HEAD_DIM), lambda i: (0, i, 0))
    const = lambda a: pl.BlockSpec(a.shape, lambda i: (0,) * a.ndim)
    lat_tile = lambda i: jnp.minimum(i, n_lat - 1)
    hy_lat = pl.BlockSpec((None, HY_COLS, tm), lambda i: (lat_tile(i) // tpb, 0, lat_tile(i) % tpb))
    hy_ctx = pl.BlockSpec((bpt, HY_COLS, ctx_len), lambda i: (jnp.maximum(i - n_lat, 0), 0, 0))
    return pl.pallas_call(
        functools.partial(_inproj_kernel, n_lat_tiles=n_lat),
        grid=(t // tm,),
        in_specs=[row(d),
                  pl.BlockSpec((1, 6, d), _mod_index(tpb, n_batch)),
                  const(g1), const(w_cat), const(w_hy_t),
                  pl.BlockSpec((tm, LANES), rope_idx), pl.BlockSpec((tm, LANES), rope_idx),
                  const(qg), const(kg), const(bd)],
        out_specs=[heads(ATT_HEADS), pl.BlockSpec((ATT_KV_HEADS, HEAD_DIM, tm), lambda i: (0, 0, i)),
                   heads(ATT_KV_HEADS), row(SSD_WIDTH), row(SSD_XBC_COLS),
                   hy_lat, hy_ctx, row(LANES)],
        out_shape=[jax.ShapeDtypeStruct((ATT_HEADS, t, HEAD_DIM), BF16),
                   jax.ShapeDtypeStruct((ATT_KV_HEADS, HEAD_DIM, t), BF16),
                   jax.ShapeDtypeStruct((ATT_KV_HEADS, t, HEAD_DIM), BF16),
                   jax.ShapeDtypeStruct((t, SSD_WIDTH), F32),
                   jax.ShapeDtypeStruct((t, SSD_XBC_COLS), F32),
                   jax.ShapeDtypeStruct((n_batch, HY_COLS, seq_len), F32),
                   jax.ShapeDtypeStruct((n_batch, HY_COLS, ctx_len), F32),
                   jax.ShapeDtypeStruct((t, LANES), F32)],
        compiler_params=_cparams(1),
        name="inproj",
    )(xa, mods, g1, w_cat, w_hy_t, cos_t, sin_t, qg, kg, bd)


def _attn_kernel(sink_ref, q_ref, *refs, n_q, band):
    if band:
        k_ref, v_ref, kc_ref, vc_ref, o_ref, bias_ref = refs
        seq_len = v_ref.shape[1]
        assert n_q >= 3
    else:
        kc_ref, vc_ref, o_ref = refs
    qb = ATT_BLOCK
    rows = ATT_GROUP * qb
    row_id = lax.broadcasted_iota(jnp.int32, (rows, 1), 0)
    nt = (((1,), (1,)), ((), ()))

    if band:
        @pl.when(pl.program_id(0) == 0)
        def _():
            rel0 = (lax.broadcasted_iota(jnp.int32, (rows, band), 1)
                    - lax.broadcasted_iota(jnp.int32, (rows, band), 0) % qb)
            for var in range(3):
                bias_ref[var] = jnp.where(jnp.abs(rel0 - var * WINDOW) <= WINDOW, 0.0, -jnp.inf)

    for j in range(ATT_KV_HEADS):
        kc = kc_ref[j]
        vc = vc_ref[j]
        snk = jnp.zeros((rows, 1), F32)
        for g in range(ATT_GROUP):
            snk = jnp.where(row_id // qb == g, sink_ref[ATT_GROUP * j + g], snk)

        def block(i, carry, j=j, kc=kc, vc=vc, snk=snk):
            q0 = pl.multiple_of(i * qb, qb)
            qh = jnp.concatenate([q_ref[ATT_GROUP * j + g, pl.ds(q0, qb), :] for g in range(ATT_GROUP)],
                                 axis=0)
            s_ctx = jnp.dot(qh, kc, preferred_element_type=F32)
            m = jnp.maximum(jnp.max(s_ctx, axis=-1, keepdims=True), snk)
            if band:
                k0 = pl.multiple_of(jnp.clip(q0 - WINDOW, 0, seq_len - band), qb)
                var = jnp.where(i == 0, 0, jnp.where(i == n_q - 1, 2, 1))
                s_loc = jnp.dot(qh, k_ref[j, :, pl.ds(k0, band)], preferred_element_type=F32) + bias_ref[var]
                m = jnp.maximum(m, jnp.max(s_loc, axis=-1, keepdims=True))
            p_ctx = jnp.exp(s_ctx - m)
            den = jnp.sum(p_ctx, axis=-1, keepdims=True) + jnp.exp(snk - m)
            o = jnp.dot(p_ctx.astype(BF16), vc, preferred_element_type=F32)
            if band:
                p_loc = jnp.exp(s_loc - m)
                den = den + jnp.sum(p_loc, axis=-1, keepdims=True)
                o = o + jnp.dot(p_loc.astype(BF16), v_ref[j, pl.ds(k0, band), :], preferred_element_type=F32)
            o = o / den
            for g in range(ATT_GROUP):
                c0 = (ATT_GROUP * j + g) * HEAD_DIM
                o_ref[pl.ds(q0, qb), c0:c0 + HEAD_DIM] = o[g * qb:(g + 1) * qb, :]
            return carry

        lax.fori_loop(0, n_q, block, 0, unroll=4)


def _attention(sinks, q, k, v, n_batch, seq_len, ctx_len, latent):
    ctx_blk0 = n_batch * seq_len // ctx_len
    kc_spec = pl.BlockSpec((ATT_KV_HEADS, HEAD_DIM, ctx_len), lambda b: (0, 0, ctx_blk0 + b))
    vc_spec = pl.BlockSpec((ATT_KV_HEADS, ctx_len, HEAD_DIM), lambda b: (0, ctx_blk0 + b, 0))
    smem = pl.BlockSpec(memory_space=pltpu.SMEM)
    scratch = []
    if latent:
        rows = seq_len
        band = ATT_BLOCK + 2 * WINDOW
        in_specs = [smem, pl.BlockSpec((ATT_HEADS, rows, HEAD_DIM), lambda b: (0, b, 0)),
                    pl.BlockSpec((ATT_KV_HEADS, HEAD_DIM, rows), lambda b: (0, 0, b)),
                    pl.BlockSpec((ATT_KV_HEADS, rows, HEAD_DIM), lambda b: (0, b, 0)), kc_spec, vc_spec]
        args = (sinks, q, k, v, k, v)
        scratch = [pltpu.VMEM((3, ATT_GROUP * ATT_BLOCK, band), F32)]
    else:
        rows = ctx_len
        band = 0
        in_specs = [smem, pl.BlockSpec((ATT_HEADS, rows, HEAD_DIM), lambda b: (0, ctx_blk0 + b, 0)),
                    kc_spec, vc_spec]
        args = (sinks, q, k, v)
    return pl.pallas_call(
        functools.partial(_attn_kernel, n_q=rows // ATT_BLOCK, band=band),
        grid=(n_batch,),
        in_specs=in_specs,
        out_specs=pl.BlockSpec((rows, ATT_WIDTH), lambda b: (b, 0)),
        out_shape=jax.ShapeDtypeStruct((n_batch * rows, ATT_WIDTH), F32),
        scratch_shapes=scratch,
        compiler_params=_cparams(1),
        name="attn_latent" if latent else "attn_ctx",
    )(*args)


def _merge_kernel(*refs, n_lat_tiles, with_ctx):
    if with_ctx:
        (attl_ref, attc_ref, syl_ref, syc_ref, hyl_ref, hyc_ref, z_ref, x_ref, mod_ref,
         ga_ref, gs_ref, gh_ref, bd_ref, w_ref, o_ref) = refs
        is_lat = pl.program_id(0) < n_lat_tiles
        att = jnp.where(is_lat, attl_ref[...], attc_ref[...])
        sy = jnp.where(is_lat, syl_ref[...], syc_ref[...])
        hy_ctx = jnp.concatenate([hyc_ref[k] for k in range(hyc_ref.shape[0])], axis=1)
        hy_t = jnp.where(is_lat, hyl_ref[...], hy_ctx)
    else:
        attl_ref, syl_ref, hyl_ref, z_ref, x_ref, mod_ref, ga_ref, gs_ref, gh_ref, bd_ref, w_ref, o_ref = refs
        att = attl_ref[...]
        sy = syl_ref[...]
        hy_t = hyl_ref[...]
    a = att * lax.rsqrt(jnp.mean(att * att, axis=-1, keepdims=True) + EPS) * ga_ref[...]
    s = sy * _silu(z_ref[...])
    s = s * lax.rsqrt(jnp.mean(s * s, axis=-1, keepdims=True) + EPS) * gs_ref[...]
    hy = hy_t.T
    hn = hy * lax.rsqrt(_segsum(hy * hy, bd_ref[...]) * (1.0 / (HYENA_WIDTH // HYENA_GROUPS)) + EPS) * gh_ref[...]
    y = jnp.dot(a.astype(BF16), w_ref[0:ATT_WIDTH, :], preferred_element_type=F32)
    y = y + jnp.dot(s.astype(BF16), w_ref[ATT_WIDTH:ATT_WIDTH + SSD_WIDTH, :], preferred_element_type=F32)
    y = y + jnp.dot(hn.astype(BF16), w_ref[ATT_WIDTH + SSD_WIDTH:, :], preferred_element_type=F32)
    o_ref[...] = x_ref[...] + mod_ref[0, 2:3, :] * y


def _merge(att, sy, hy, z, xa, mods, ga, gs, gh, bd, w_out, n_batch, seq_len):
    d = xa.shape[1]
    tm = TOKEN_TILE
    nl = att[0].shape[0] // tm
    with_ctx = att[1] is not None
    t = att[0].shape[0] + (att[1].shape[0] if with_ctx else 0)
    row = lambda w: pl.BlockSpec((tm, w), lambda i: (i, 0))
    lat_row = lambda w: pl.BlockSpec((tm, w), lambda i: (jnp.minimum(i, nl - 1), 0))
    ctx_row = lambda w: pl.BlockSpec((tm, w), lambda i: (jnp.maximum(i - nl, 0), 0))
    const = lambda a: pl.BlockSpec(a.shape, lambda i: (0,) * a.ndim)
    tpb = seq_len // tm
    lat_tile = lambda i: jnp.minimum(i, nl - 1)
    hy_lat = pl.BlockSpec((None, HYENA_WIDTH, tm), lambda i: (lat_tile(i) // tpb, 0, lat_tile(i) % tpb))
    if with_ctx:
        ctx_len = hy[1].shape[2]
        streams = [att[0], att[1], sy[0], sy[1], hy[0], hy[1]]
        specs = [lat_row(ATT_WIDTH), ctx_row(ATT_WIDTH), lat_row(SSD_WIDTH), ctx_row(SSD_WIDTH), hy_lat,
                 pl.BlockSpec((tm // ctx_len, HYENA_WIDTH, ctx_len), lambda i: (jnp.maximum(i - nl, 0), 0, 0))]
    else:
        streams = [att[0], sy[0], hy[0]]
        specs = [row(ATT_WIDTH), row(SSD_WIDTH), hy_lat]
    return pl.pallas_call(
        functools.partial(_merge_kernel, n_lat_tiles=nl, with_ctx=with_ctx),
        grid=(t // tm,),
        in_specs=specs + [row(SSD_WIDTH), row(d),
                          pl.BlockSpec((1, 6, d), _mod_index(seq_len // tm, n_batch)),
                          const(ga), const(gs), const(gh), const(bd), const(w_out)],
        out_specs=row(d),
        out_shape=jax.ShapeDtypeStruct((t, d), F32),
        compiler_params=_cparams(1),
        name="merge_outproj",
    )(*streams, z, xa, mods, ga, gs, gh, bd, w_out)


def _swiglu_accumulate(h, wg_ref, wu_ref, wd_ref, acc_ref):
    for c in range(N_FFN_CHUNKS):
        cols = slice(c * FFN_CHUNK, (c + 1) * FFN_CHUNK)
        g = jnp.dot(h, wg_ref[:, cols], preferred_element_type=F32)
        u = jnp.dot(h, wu_ref[:, cols], preferred_element_type=F32)
        a = (_silu(g) * u).astype(BF16)
        part = jnp.dot(a, wd_ref[cols, :], preferred_element_type=F32)
        if c == 0:
            acc_ref[...] = part
        else:
            acc_ref[...] += part


def _ffn_kernel(x_ref, mod_ref, g2_ref, wg_ref, wu_ref, wd_ref, o_ref, acc_ref):
    x = x_ref[...]
    h = _modnorm(x, g2_ref[...], mod_ref[0, 4:5, :], mod_ref[0, 3:4, :]).astype(BF16)
    _swiglu_accumulate(h, wg_ref, wu_ref, wd_ref, acc_ref)
    o_ref[...] = x + mod_ref[0, 5:6, :] * acc_ref[...]


def _ffn(xa, mods, g2, wg, wu, wd, n_batch, seq_len):
    t, d = xa.shape
    tm = TOKEN_TILE
    row = pl.BlockSpec((tm, d), lambda i: (i, 0))
    resident = lambda a: pl.BlockSpec(a.shape, lambda i: (0,) * a.ndim, pipeline_mode=pl.Buffered(1))
    return pl.pallas_call(
        _ffn_kernel,
        grid=(t // tm,),
        in_specs=[row, pl.BlockSpec((1, 6, d), _mod_index(seq_len // tm, n_batch)),
                  pl.BlockSpec(g2.shape, lambda i: (0, 0)), resident(wg), resident(wu), resident(wd)],
        out_specs=row,
        out_shape=jax.ShapeDtypeStruct((t, d), F32),
        scratch_shapes=[pltpu.VMEM((tm, d), F32)],
        compiler_params=_cparams(1),
        name="ffn",
    )(xa, mods, g2, wg, wu, wd)


def _router_kernel(x_ref, mod_ref, g2_ref, r_ref, h_ref, idx_ref, wt_ref):
    h = _modnorm(x_ref[...], g2_ref[...], mod_ref[0, 4:5, :], mod_ref[0, 3:4, :])
    h_ref[...] = h
    h_hi = h.astype(BF16)
    h_lo = (h - h_hi.astype(F32)).astype(BF16)
    both = jnp.dot(h_hi, r_ref[...], preferred_element_type=F32)
    logits = (both[:, 0:LANES] + both[:, LANES:2 * LANES]
              + jnp.dot(h_lo, r_ref[:, 0:LANES], preferred_element_type=F32))
    lane = lax.broadcasted_iota(jnp.int32, logits.shape, 1)
    neg = -jnp.inf
    l1 = jnp.where(lane < N_EXPERTS, logits, neg)
    m1 = jnp.max(l1, axis=-1, keepdims=True)
    i1 = jnp.min(jnp.where(l1 == m1, lane, LANES), axis=-1, keepdims=True)
    l2 = jnp.where(lane == i1, neg, l1)
    m2 = jnp.max(l2, axis=-1, keepdims=True)
    i2 = jnp.min(jnp.where(l2 == m2, lane, LANES), axis=-1, keepdims=True)
    e = jnp.exp(m2 - m1)
    w1 = 1.0 / (1.0 + e)
    w2 = e / (1.0 + e)
    idx_ref[...] = jnp.where(lane == 0, i1, jnp.where(lane == 1, i2, 0))
    wt_ref[...] = jnp.where(lane == 0, w1, jnp.where(lane == 1, w2, 0.0))


def _router(xa, mods, g2, r_pad, n_rows, n_batch, seq_len):
    d = xa.shape[1]
    tm = TOKEN_TILE
    row = lambda w: pl.BlockSpec((tm, w), lambda i: (i, 0))
    return pl.pallas_call(
        _router_kernel,
        grid=(n_rows // tm,),
        in_specs=[row(d), pl.BlockSpec((1, 6, d), _mod_index(seq_len // tm, n_batch)),
                  pl.BlockSpec(g2.shape, lambda i: (0, 0)), pl.BlockSpec(r_pad.shape, lambda i: (0, 0))],
        out_specs=[row(d), row(LANES), row(LANES)],
        out_shape=[jax.ShapeDtypeStruct((n_rows, d), F32),
                   jax.ShapeDtypeStruct((n_rows, LANES), jnp.int32),
                   jax.ShapeDtypeStruct((n_rows, LANES), F32)],
        compiler_params=_cparams(1),
        name="moe_router",
    )(xa, mods, g2, r_pad)


def _row_copy(src, src_row, dst, dst_row, sem):
    return pltpu.make_async_copy(src.at[pl.ds(src_row, 1), :], dst.at[pl.ds(dst_row, 1), :], sem)


DMA_ISSUE_UNROLL = 8


def _idx_copy(dest_hbm, dest_smem, sem_idx, tile, slot):
    n = dest_hbm.shape[1]
    half = dest_smem.at[pl.ds(pl.multiple_of(slot * n, n), n)]
    return pltpu.make_async_copy(dest_hbm.at[tile], half, sem_idx.at[slot])


def _dispatch_kernel(pad_tile_ref, dest_hbm, h_ref, xs_out, dest_smem, zeros, sem_idx, sem_rows, sem_zero):
    i = pl.program_id(0)
    n = pl.num_programs(0)
    tm = h_ref.shape[0]
    slot = i % 2

    @pl.when(i == 0)
    def _():
        zeros[...] = jnp.zeros_like(zeros)

        def zero_copy(e):
            return pltpu.make_async_copy(zeros, xs_out.at[pl.ds(pl.multiple_of(pad_tile_ref[e], tm), tm), :],
                                         sem_zero)

        for e in range(2 * N_EXPERTS):
            @pl.when(pad_tile_ref[e] >= 0)
            def _(e=e):
                zero_copy(e).start()
        for e in range(2 * N_EXPERTS):
            @pl.when(pad_tile_ref[e] >= 0)
            def _(e=e):
                zero_copy(e).wait()
        _idx_copy(dest_hbm, dest_smem, sem_idx, 0, 0).start()

    @pl.when(i + 1 < n)
    def _():
        _idx_copy(dest_hbm, dest_smem, sem_idx, i + 1, 1 - slot).start()

    _idx_copy(dest_hbm, dest_smem, sem_idx, i, slot).wait()

    base = slot * (2 * tm)

    def issue(r, carry):
        _row_copy(h_ref, r, xs_out, dest_smem[base + 2 * r], sem_rows).start()
        _row_copy(h_ref, r, xs_out, dest_smem[base + 2 * r + 1], sem_rows).start()
        return carry

    lax.fori_loop(0, tm, issue, 0, unroll=DMA_ISSUE_UNROLL)
    for _ in range(2):
        pltpu.make_async_copy(h_ref, xs_out.at[pl.ds(0, tm), :], sem_rows).wait()


def _dispatch(pad_tile, dest, h, n_slots):
    n_rows, d = h.shape
    tm = TOKEN_TILE
    assert tm == EXPERT_TILE
    return pl.pallas_call(
        _dispatch_kernel,
        grid_spec=pltpu.PrefetchScalarGridSpec(
            num_scalar_prefetch=1,
            grid=(n_rows // tm,),
            in_specs=[pl.BlockSpec(memory_space=pl.ANY), pl.BlockSpec((tm, d), lambda i, pt: (i, 0))],
            out_specs=pl.BlockSpec(memory_space=pl.ANY),
            scratch_shapes=[pltpu.SMEM((4 * tm,), jnp.int32), pltpu.VMEM((tm, d), F32),
                            pltpu.SemaphoreType.DMA((2,)), pltpu.SemaphoreType.DMA(()),
                            pltpu.SemaphoreType.DMA(())]),
        out_shape=jax.ShapeDtypeStruct((n_slots, d), F32),
        compiler_params=_cparams(1),
        name="moe_dispatch",
    )(pad_tile, dest.reshape(n_rows // tm, 2 * tm), h)


def _expert_kernel(te_ref, nused_ref, xs_ref, wg_ref, wu_ref, wd_ref, o_ref, acc_ref):
    del te_ref
    live = pl.program_id(0) < nused_ref[0]

    @pl.when(live)
    def _():
        _swiglu_accumulate(xs_ref[...].astype(BF16), wg_ref, wu_ref, wd_ref, acc_ref)
        o_ref[...] = acc_ref[...]

    @pl.when(jnp.logical_not(live))
    def _():
        o_ref[...] = jnp.zeros_like(o_ref)


def _experts(tile_expert, n_used, xs, wg, wu, wd):
    s, d = xs.shape
    tm = EXPERT_TILE
    row = pl.BlockSpec((tm, d), lambda i, te, nu: (i, 0))
    xs_row = pl.BlockSpec((tm, d), lambda i, te, nu: (jnp.minimum(i, nu[0] - 1), 0))
    wspec = lambda a: pl.BlockSpec((None,) + a.shape[1:], lambda i, te, nu: (te[i], 0, 0))
    return pl.pallas_call(
        _expert_kernel,
        grid_spec=pltpu.PrefetchScalarGridSpec(
            num_scalar_prefetch=2,
            grid=(s // tm,),
            in_specs=[xs_row, wspec(wg), wspec(wu), wspec(wd)],
            out_specs=row,
            scratch_shapes=[pltpu.VMEM((tm, d), F32)]),
        out_shape=jax.ShapeDtypeStruct((s, d), F32),
        compiler_params=_cparams(1),
        name="moe_experts",
    )(tile_expert, n_used, xs, wg, wu, wd)


def _combine_kernel(dest_hbm, eo_hbm, x_ref, wt_ref, mod_ref, o_ref, dest_smem, buf, sem_idx, sem_rows):
    i = pl.program_id(0)
    n = pl.num_programs(0)
    tm = x_ref.shape[0]
    slot = i % 2

    def gather(s):
        base = s * (2 * tm)

        def issue(r, carry):
            _row_copy(eo_hbm, dest_smem[base + 2 * r], buf.at[s, 0], r, sem_rows.at[s]).start()
            _row_copy(eo_hbm, dest_smem[base + 2 * r + 1], buf.at[s, 1], r, sem_rows.at[s]).start()
            return carry

        lax.fori_loop(0, tm, issue, 0, unroll=DMA_ISSUE_UNROLL)

    @pl.when(i == 0)
    def _():
        first = _idx_copy(dest_hbm, dest_smem, sem_idx, 0, 0)
        first.start()
        first.wait()
        gather(0)

        @pl.when(n > 1)
        def _():
            _idx_copy(dest_hbm, dest_smem, sem_idx, 1, 1).start()

    @pl.when(i + 1 < n)
    def _():
        _idx_copy(dest_hbm, dest_smem, sem_idx, i + 1, 1 - slot).wait()

        @pl.when(i + 2 < n)
        def _():
            _idx_copy(dest_hbm, dest_smem, sem_idx, i + 2, slot).start()

        gather(1 - slot)

    for k in range(2):
        pltpu.make_async_copy(eo_hbm.at[pl.ds(0, tm), :], buf.at[slot, k], sem_rows.at[slot]).wait()
    wt = wt_ref[...]
    y = wt[:, 0:1] * buf[slot, 0] + wt[:, 1:2] * buf[slot, 1]
    o_ref[...] = x_ref[...] + mod_ref[0, 5:6, :] * y


def _combine(dest, eo, xa, wts, mods, n_rows, n_batch, seq_len):
    d = xa.shape[1]
    tm = TOKEN_TILE
    row = lambda w: pl.BlockSpec((tm, w), lambda i: (i, 0))
    return pl.pallas_call(
        _combine_kernel,
        grid=(n_rows // tm,),
        in_specs=[pl.BlockSpec(memory_space=pl.ANY), pl.BlockSpec(memory_space=pl.ANY), row(d), row(LANES),
                  pl.BlockSpec((1, 6, d), _mod_index(seq_len // tm, n_batch))],
        out_specs=row(d),
        out_shape=jax.ShapeDtypeStruct((n_rows, d), F32),
        scratch_shapes=[pltpu.SMEM((4 * tm,), jnp.int32), pltpu.VMEM((2, 2, tm, d), F32),
                        pltpu.SemaphoreType.DMA((2,)), pltpu.SemaphoreType.DMA((2,))],
        compiler_params=_cparams(1),
        name="moe_combine",
    )(dest.reshape(n_rows // tm, 2 * tm), eo, xa, wts, mods)


def _moe(xa, mods, g2, r_pad, wg, wu, wd, n_rows, n_batch, seq_len):
    h, idx, wts = _router(xa, mods, g2, r_pad, n_rows, n_batch, seq_len)
    tm = EXPERT_TILE
    e_flat = idx[:, :2].reshape(-1)
    onehot = (e_flat[:, None] == jnp.arange(N_EXPERTS, dtype=jnp.int32)[None, :]).astype(jnp.int32)
    csum = jnp.cumsum(onehot, axis=0)
    counts = csum[-1]
    rank = jnp.sum(onehot * csum, axis=1) - 1
    padded = ((counts + tm - 1) // tm) * tm
    ends = jnp.cumsum(padded)
    starts = ends - padded
    dest = (jnp.sum(onehot * starts[None, :], axis=1) + rank).astype(jnp.int32)
    n_slots = 2 * n_rows + N_EXPERTS * tm
    tile_start = jnp.arange(n_slots // tm, dtype=jnp.int32) * tm
    tile_expert = jnp.minimum(jnp.sum((tile_start[:, None] >= ends[None, :]).astype(jnp.int32), axis=1),
                              N_EXPERTS - 1).astype(jnp.int32)
    n_used = (ends[-1:] // tm).astype(jnp.int32)
    tail = ends[-1] + jnp.arange(N_EXPERTS, dtype=ends.dtype) * tm
    pad_tile = jnp.concatenate([jnp.where(padded > 0, ends - tm, -1),
                                jnp.where(tail < n_slots, tail, -1)]).astype(jnp.int32)
    xs = _dispatch(pad_tile, dest, h, n_slots)
    eo = _experts(tile_expert, n_used, xs, wg, wu, wd)
    return _combine(dest, eo, xa, wts, mods, n_rows, n_batch, seq_len)


def _softplus(v):
    return jnp.maximum(v, 0.0) + jnp.log1p(jnp.exp(-jnp.abs(v)))


def _ssd_kernel(xl_ref, dl_ref, xc_ref, dc_ref, cw_ref, cb_ref, dtb_ref, alogc_ref, dsk_ref, *rest, want_ctx):
    if want_ctx:
        yl_ref, yc_ref, xs_l, dtt_l, xs_c, dtt_c, st_ref = rest
    else:
        yl_ref, xs_l, dtt_l, xs_c, dtt_c, st_ref = rest
        yc_ref = None
    ck = SSD_CHUNK
    hp = SSD_HEAD_DIM
    ns = SSD_STATE
    n_col = 2 * SSD_HEADS
    hi = lax.Precision.HIGHEST
    row = lax.broadcasted_iota(jnp.int32, (ck, 1), 0)
    li = lax.broadcasted_iota(jnp.int32, (ck, ck), 0)
    si = lax.broadcasted_iota(jnp.int32, (ck, ck), 1)
    masks = (si <= li, si >= li)
    tris = (masks[0].astype(F32), masks[1].astype(F32))
    a_col = -jnp.exp(alogc_ref[...])
    dskip = dsk_ref[...]

    def prep(raw_ref, dtraw_ref, xs_s, dtt_s, y_ref):
        n = raw_ref.shape[0]
        nk = n // ck

        def body(k, carry):
            r0 = pl.multiple_of(k * ck, ck)
            a = raw_ref[pl.ds(r0, ck), :]
            top = raw_ref[pl.ds(pl.multiple_of(jnp.maximum(r0 - 8, 0), 8), 8), :][7:8, :]
            bot = raw_ref[pl.ds(pl.multiple_of(jnp.minimum(r0 + ck, n - 8), 8), 8), :][0:1, :]
            top = jnp.where(k > 0, top, 0.0)
            bot = jnp.where(k < nk - 1, bot, 0.0)
            prev = jnp.where(row == 0, top, pltpu.roll(a, 1, 0))
            nxt = jnp.where(row == ck - 1, bot, pltpu.roll(a, ck - 1, 0))
            xs = _silu(prev * cw_ref[0:1, :] + a * cw_ref[1:2, :] + nxt * cw_ref[2:3, :] + cb_ref[...])
            xs_s[pl.ds(r0, ck), :] = xs
            dt = _softplus(dtraw_ref[pl.ds(r0, ck), :] + dtb_ref[...])
            dtt_s[:, pl.ds(r0, ck)] = dt.T[0:n_col, :]
            if y_ref is not None:
                y_ref[pl.ds(r0, ck), :] = xs[:, 0:SSD_WIDTH] * dskip
            return carry

        lax.fori_loop(0, nk, body, 0)

    def run(xs_s, dtt_s, y_ref):
        nk = xs_s.shape[0] // ck

        def one(kk, dr):
            r0 = pl.multiple_of(kk * ck, ck)
            xc = xs_s[pl.ds(r0, ck), :]
            dtr = dtt_s[:, pl.ds(r0, ck)]
            dta_r = dtr * a_col
            la_r = jnp.dot(dta_r, tris[1 - dr], precision=hi, preferred_element_type=F32)
            la_end = la_r[:, ck - 1:ck] if dr == 0 else la_r[:, 0:1]
            if y_ref is not None:
                la_c = jnp.concatenate([la_r, jnp.zeros((ck - n_col, ck), F32)], axis=0).T
            ys = []
            for g in range(SSD_GROUPS):
                b_t = xc[:, SSD_WIDTH + g * ns:SSD_WIDTH + (g + 1) * ns].T
                c0 = SSD_WIDTH + SSD_GROUPS * ns + g * ns
                c_g = xc[:, c0:c0 + ns].astype(BF16)
                if y_ref is not None:
                    scores = jnp.dot(c_g, b_t.astype(BF16), preferred_element_type=F32)
                for hh in range(SSD_HEADS // SSD_GROUPS):
                    h = g * (SSD_HEADS // SSD_GROUPS) + hh
                    col = dr * SSD_HEADS + h
                    xh = xc[:, h * hp:(h + 1) * hp].astype(BF16)
                    dt_row = dtr[col:col + 1, :]
                    le = la_end[col:col + 1, :]
                    st = st_ref[col]
                    if y_ref is not None:
                        la_col = la_c[:, col:col + 1]
                        decay = jnp.exp(jnp.where(masks[dr], la_col - la_r[col:col + 1, :], -jnp.inf))
                        y = jnp.dot((scores * decay * dt_row).astype(BF16), xh, preferred_element_type=F32)
                        y = y + jnp.dot(c_g, st.astype(BF16), preferred_element_type=F32) * jnp.exp(la_col)
                        ys.append(y)
                    bw = (b_t * (dt_row * jnp.exp(le - la_r[col:col + 1, :]))).astype(BF16)
                    st_ref[col] = st * jnp.exp(le) + jnp.dot(bw, xh, preferred_element_type=F32)
            if y_ref is not None:
                y_ref[pl.ds(r0, ck), :] += jnp.concatenate(ys, axis=1)

        def body(k, carry):
            one(k, 0)
            one(nk - 1 - k, 1)
            return carry

        lax.fori_loop(0, nk, body, 0, unroll=2)

    prep(xl_ref, dl_ref, xs_l, dtt_l, yl_ref)
    prep(xc_ref, dc_ref, xs_c, dtt_c, yc_ref)
    st_ref[...] = jnp.zeros_like(st_ref)
    run(xs_c, dtt_c, yc_ref)
    run(xs_l, dtt_l, yl_ref)


def _ssd(xbc, dtp, conv_w, conv_b, dt_bias, a_log, d_skip, n_batch, seq_len, ctx_len, want_ctx):
    ctx0 = n_batch * seq_len // ctx_len
    pad = lambda v: jnp.pad(v.reshape(1, -1), ((0, 0), (0, LANES - v.size)))
    lat = lambda w: pl.BlockSpec((seq_len, w), lambda b: (b, 0))
    ctx = lambda w: pl.BlockSpec((ctx_len, w), lambda b: (ctx0 + b, 0))
    const = lambda a: pl.BlockSpec(a.shape, lambda b: (0,) * a.ndim)
    consts = (conv_w, conv_b.reshape(1, -1), pad(dt_bias), a_log.reshape(-1, 1),
              jnp.repeat(d_skip, SSD_HEAD_DIM).reshape(1, -1))
    out_specs = [lat(SSD_WIDTH)]
    out_shape = [jax.ShapeDtypeStruct((n_batch * seq_len, SSD_WIDTH), F32)]
    if want_ctx:
        out_specs.append(pl.BlockSpec((ctx_len, SSD_WIDTH), lambda b: (b, 0)))
        out_shape.append(jax.ShapeDtypeStruct((n_batch * ctx_len, SSD_WIDTH), F32))
    return pl.pallas_call(
        functools.partial(_ssd_kernel, want_ctx=want_ctx),
        grid=(n_batch,),
        in_specs=[lat(SSD_XBC_COLS), lat(LANES), ctx(SSD_XBC_COLS), ctx(LANES)] + [const(a) for a in consts],
        out_specs=out_specs,
        out_shape=out_shape,
        scratch_shapes=[pltpu.VMEM((seq_len, SSD_XBC_COLS), F32), pltpu.VMEM((2 * SSD_HEADS, seq_len), F32),
                        pltpu.VMEM((ctx_len, SSD_XBC_COLS), F32), pltpu.VMEM((2 * SSD_HEADS, ctx_len), F32),
                        pltpu.VMEM((2 * SSD_HEADS, SSD_STATE, SSD_HEAD_DIM), F32)],
        compiler_params=_cparams(1),
        name="ssd",
    )(xbc, dtp, xbc, dtp, *consts)


HY_BLOCK = 256
HY_CH_STEP = 8


def _hyena_tables(seq_len):
    f32 = np.float32
    nj = 2 * seq_len
    lag = np.arange(nj, dtype=np.int32) - seq_len
    dist = np.abs(lag)
    pos = np.minimum(dist, seq_len - 1)
    t = np.linspace(0.0, 1.0, seq_len, dtype=f32)[pos]
    w = (f32(2.0 * math.pi / seq_len) * np.arange(seq_len, dtype=f32))[pos]
    bands = (HYENA_POS_DIM - 1) // 2
    freqs = np.linspace(1e-4, bands - 1, bands, dtype=f32)[None, :]
    ang = (freqs * w[:, None]).astype(f32)
    z = np.concatenate([t[:, None], np.cos(ang), -np.sin(ang)], axis=-1).astype(f32)
    zt = np.pad(z.T, ((0, (-HYENA_POS_DIM) % 8), (0, 0)))
    deltas = np.abs(np.linspace(math.log(HYENA_DECAY_TARGET) / HYENA_SLOW_DECAY,
                                math.log(HYENA_DECAY_TARGET) / HYENA_FAST_DECAY, HYENA_WIDTH, dtype=f32))
    dec = (np.exp(-t[None, :] * deltas[:, None]) * (dist < seq_len)[None, :]).astype(f32)
    fwd = (lag >= 0).astype(f32)[None, :]
    return jnp.asarray(zt), jnp.asarray(dec), jnp.asarray(fwd)


def _hyfilt_kernel(zt_ref, dec_ref, fwd_ref, w1_ref, b1_ref, f1_ref, w2_ref, b2_ref, f2_ref, w3_ref, o_ref):
    hi = lax.Precision.HIGHEST
    h = jnp.sin(f1_ref[...] * (jnp.dot(w1_ref[...], zt_ref[...], precision=hi, preferred_element_type=F32)
                               + b1_ref[...]))
    h = jnp.sin(f2_ref[...] * (jnp.dot(w2_ref[...], h, precision=hi, preferred_element_type=F32) + b2_ref[...]))
    hw = jnp.dot(w3_ref[...], h, precision=hi, preferred_element_type=F32)
    fwd = fwd_ref[...] > 0.5
    dec = dec_ref[...]
    nw = HYENA_WIDTH
    for o in range(HYENA_ORDER):
        o_ref[o] = jnp.where(fwd, hw[o * nw:(o + 1) * nw], hw[(HYENA_ORDER + o) * nw:(HYENA_ORDER + o + 1) * nw]) * dec


def _hyena_filters(tables, w1, b1, f1, w2, b2, f2, w3):
    zt, dec, fwd = tables
    nj = zt.shape[1]
    tj = 512
    col = lambda v: v.reshape(-1, 1)
    w1t = jnp.pad(w1.T, ((0, 0), (0, zt.shape[0] - w1.shape[0])))
    consts = (w1t, col(b1), col(f1), w2.T, col(b2), col(f2), w3.T)
    lanes = lambda a: pl.BlockSpec((a.shape[0], tj), lambda j: (0, j))
    const = lambda a: pl.BlockSpec(a.shape, lambda j: (0, 0))
    return pl.pallas_call(
        _hyfilt_kernel,
        grid=(nj // tj,),
        in_specs=[lanes(zt), lanes(dec), lanes(fwd)] + [const(a) for a in consts],
        out_specs=pl.BlockSpec((HYENA_ORDER, HYENA_WIDTH, tj), lambda j: (0, 0, j)),
        out_shape=jax.ShapeDtypeStruct((HYENA_ORDER, HYENA_WIDTH, nj), F32),
        compiler_params=_cparams(1),
        name="hyena_filters",
    )(zt, dec, fwd, *consts)


def _hyconv_kernel(cw_ref, cb_ref, hb_ref, v_ref, x1_ref, x2_ref, kf_ref, o_ref):
    n_b, n_ch, seq_len = v_ref.shape
    nb = seq_len // HY_BLOCK
    blk = HY_BLOCK
    c_base = pl.program_id(0) * n_ch
    lane = lax.broadcasted_iota(jnp.int32, (1, seq_len), 1)

    def sconv(x, ch):
        prev = jnp.where(lane == 0, 0.0, pltpu.roll(x, 1, 1))
        nxt = jnp.where(lane == seq_len - 1, 0.0, pltpu.roll(x, seq_len - 1, 1))
        return prev * cw_ref[0, ch] + x * cw_ref[1, ch] + nxt * cw_ref[2, ch] + cb_ref[ch]

    def long_conv(vals, kf_row):
        half = blk // 2
        skew = pltpu.roll(jnp.broadcast_to(kf_row, (half, 2 * seq_len)), 0, 1, stride=1,
                          stride_axis=0).astype(BF16)
        vb = vals.astype(BF16)
        acc = [None] * nb
        for d in range(-(nb - 1), nb):
            a0 = seq_len + d * blk
            tt = jnp.concatenate([skew[:, a0:a0 + blk], skew[:, a0 - half:a0 - half + blk]], axis=0)
            sis = list(range(max(0, -d), min(nb, nb - d)))
            lhs = [vb[:, s * blk:(s + 1) * blk] for s in sis]
            lhs = lhs[0] if len(lhs) == 1 else jnp.concatenate(lhs, axis=0)
            out = jnp.dot(lhs, tt, preferred_element_type=F32)
            for idx, s in enumerate(sis):
                piece = out[idx * n_b:(idx + 1) * n_b]
                acc[s + d] = piece if acc[s + d] is None else acc[s + d] + piece
        return acc[0] if nb == 1 else jnp.concatenate(acc, axis=1)

    def channel(cc, carry):
        ch = c_base + cc
        v = sconv(v_ref[:, cc, :], ch)
        x1 = sconv(x1_ref[:, cc, :], HYENA_WIDTH + ch)
        x2 = sconv(x2_ref[:, cc, :], 2 * HYENA_WIDTH + ch)
        z = x1 * (long_conv(v, kf_ref[0, pl.ds(cc, 1), :]) + v * hb_ref[0, ch])
        o_ref[:, cc, :] = x2 * (long_conv(z, kf_ref[1, pl.ds(cc, 1), :]) + z * hb_ref[1, ch])
        return carry

    lax.fori_loop(0, n_ch, channel, 0)


def _hyena_conv(hy, kf, conv_w, conv_b, hy_bias):
    n_batch, _, seq_len = hy.shape
    cs = HY_CH_STEP
    nw = HYENA_WIDTH
    stream = lambda k: pl.BlockSpec((n_batch, cs, seq_len), lambda c: (0, k * (nw // cs) + c, 0))
    smem = pl.BlockSpec(memory_space=pltpu.SMEM)
    return pl.pallas_call(
        _hyconv_kernel,
        grid=(nw // cs,),
        in_specs=[smem, smem, smem, stream(0), stream(1), stream(2),
                  pl.BlockSpec((HYENA_ORDER, cs, 2 * seq_len), lambda c: (0, c, 0))],
        out_specs=pl.BlockSpec((n_batch, cs, seq_len), lambda c: (0, c, 0)),
        out_shape=jax.ShapeDtypeStruct((n_batch, nw, seq_len), F32),
        compiler_params=_cparams(1),
        name="hyena_conv",
    )(conv_w, conv_b, hy_bias, hy, hy, hy, kf)


def _rope_tables(seq_len, extra):
    rows = seq_len // GRID_W
    row = jnp.repeat(jnp.arange(rows, dtype=F32), GRID_W)
    col = jnp.tile(jnp.arange(GRID_W, dtype=F32), rows)
    inv = ROPE_THETA ** (-jnp.arange(0, ROPE_AXIS_DIM, 2, dtype=F32) / ROPE_AXIS_DIM)
    ang = jnp.stack([row[:, None] * inv, col[:, None] * inv], axis=1)
    cos = jnp.cos(ang)
    sin = jnp.sin(ang)
    cos_h = jnp.concatenate([cos, cos], axis=-1).reshape(seq_len, HEAD_DIM)
    sin_h = jnp.concatenate([-sin, sin], axis=-1).reshape(seq_len, HEAD_DIM)
    cos_t = jnp.concatenate([jnp.tile(cos_h, (1, LANES // HEAD_DIM)), jnp.ones((extra, LANES), F32)], axis=0)
    sin_t = jnp.concatenate([jnp.tile(sin_h, (1, LANES // HEAD_DIM)), jnp.zeros((extra, LANES), F32)], axis=0)
    return cos_t, sin_t


def _block_diag_ones(n, seg):
    i = jnp.arange(n) // seg
    return (i[:, None] == i[None, :]).astype(BF16)


def kernel(x, c, ctx, c_ctx, w_ada, b_ada, norm1, norm2, w_in, w_out, q_norm, k_norm, att_sinks, att_out_norm, ssd_conv_w, ssd_conv_b, ssd_dt_bias, ssd_a_log, ssd_d, ssd_norm, hy_conv_w, hy_conv_b, hy_w1, hy_b1, hy_f1, hy_w2, hy_b2, hy_f2, hy_w3, hy_bias, hy_out_norm, ffn_w_gate, ffn_w_up, ffn_w_down, moe_router, moe_w_gate, moe_w_up, moe_w_down):
    n_batch, seq_len, d = x.shape
    ctx_len = ctx.shape[1]
    n_lat = n_batch * seq_len
    n_ctx = n_batch * ctx_len
    depth = w_in.shape[0]
    xa = jnp.concatenate([x.reshape(n_lat, d), ctx.reshape(n_ctx, d)], axis=0)

    cc = jnp.concatenate([c, c_ctx[None, :]], axis=0)
    pad_rows = (-cc.shape[0]) % 8
    cc = jnp.pad(cc, ((0, pad_rows), (0, 0)))
    mods_all = _adaln(cc, w_ada, b_ada)[:, :n_batch + 1].reshape(depth, n_batch + 1, 6, d)

    cos_t, sin_t = _rope_tables(seq_len, TOKEN_TILE)
    bd_q = _block_diag_ones(Q_COLS, HEAD_DIM)
    bd_h = _block_diag_ones(HYENA_WIDTH, HYENA_WIDTH // HYENA_GROUPS)
    hy_tab_l = _hyena_tables(seq_len)
    hy_tab_c = _hyena_tables(ctx_len)

    for i in range(depth):
        last = i == depth - 1
        j = i // 2
        mods = mods_all[i]
        wi = w_in[i]
        c_dt = QKV_W + ZX_W
        w_cat = jnp.concatenate([wi[:, :c_dt], wi[:, c_dt:c_dt + SSD_DT_COLS],
                                 jnp.zeros((d, LANES - SSD_DT_COLS), F32)], axis=1).astype(BF16)
        w_hy_t = wi[:, c_dt + SSD_DT_COLS:].T.astype(BF16)
        qg = jnp.tile(q_norm[i], Q_COLS // HEAD_DIM)[None, :]
        kg = jnp.tile(k_norm[i], KV_COLS // HEAD_DIM)[None, :]
        q, k, v, z, xbc, hy_l, hy_c, dtp = _inproj(xa, mods, norm1[i][None, :], w_cat, w_hy_t, cos_t, sin_t,
                                                   qg, kg, bd_q, n_batch, seq_len, ctx_len)

        att_l = _attention(att_sinks[i], q, k, v, n_batch, seq_len, ctx_len, True)
        ssd_out = _ssd(xbc, dtp, ssd_conv_w[i], ssd_conv_b[i], ssd_dt_bias[i], ssd_a_log[i], ssd_d[i],
                       n_batch, seq_len, ctx_len, not last)
        filt = (hy_w1[i], hy_b1[i], hy_f1[i], hy_w2[i], hy_b2[i], hy_f2[i], hy_w3[i])
        hyo_l = _hyena_conv(hy_l, _hyena_filters(hy_tab_l, *filt), hy_conv_w[i], hy_conv_b[i], hy_bias[i])
        if last:
            att = (att_l, None)
            sy = (ssd_out[0], None)
            hyo = (hyo_l, None)
            n_rows = n_lat
        else:
            att = (att_l, _attention(att_sinks[i], q, k, v, n_batch, seq_len, ctx_len, False))
            sy = tuple(ssd_out)
            hyo_c = _hyena_conv(hy_c, _hyena_filters(hy_tab_c, *filt), hy_conv_w[i], hy_conv_b[i], hy_bias[i])
            hyo = (hyo_l, hyo_c)
            n_rows = n_lat + n_ctx
        xa = _merge(att, sy, hyo, z, xa, mods, att_out_norm[i][None, :], ssd_norm[i][None, :],
                    hy_out_norm[i][None, :], bd_h, w_out[i].astype(BF16), n_batch, seq_len)

        g2 = norm2[i][None, :]
        if i % 2 == 0:
            xa = _ffn(xa, mods, g2, ffn_w_gate[j].astype(BF16), ffn_w_up[j].astype(BF16),
                      ffn_w_down[j].astype(BF16), n_batch, seq_len)
        else:
            r_full = jnp.pad(moe_router[j], ((0, 0), (0, LANES - N_EXPERTS)))
            r_hi = r_full.astype(BF16)
            r_pad = jnp.concatenate([r_hi, (r_full - r_hi.astype(F32)).astype(BF16)], axis=1)
            xa = _moe(xa, mods, g2, r_pad, moe_w_gate[j].astype(BF16), moe_w_up[j].astype(BF16),
                      moe_w_down[j].astype(BF16), n_rows, n_batch, seq_len)
    return xa[:n_lat].reshape(n_batch, seq_len, d)
```

```python
import functools
import math

import jax
import jax.numpy as jnp
import numpy as np
from jax import lax
from jax.experimental import pallas as pl
from jax.experimental.pallas import tpu as pltpu

F32 = jnp.float32
BF16 = jnp.bfloat16

D_MODEL = 1024
DEPTH = 4
GRID_W = 64
EPS = 1e-6
HEAD_DIM = 64
ATT_WIDTH = 512
ATT_HEADS = 8
ATT_KV_HEADS = 2
ATT_GROUP = 4
WINDOW = 128
ATT_BLOCK = 128
ROPE_THETA = 10000.0
ROPE_AXIS_DIM = 32
SSD_WIDTH = 256
SSD_HEAD_DIM = 64
SSD_HEADS = 4
SSD_STATE = 64
SSD_GROUPS = 2
SSD_CHUNK = 128
HYENA_WIDTH = 256
HYENA_GROUPS = 4
HYENA_ORDER = 2
HYENA_POS_DIM = 33
HYENA_FAST_DECAY = 0.3
HYENA_SLOW_DECAY = 1.5
HYENA_DECAY_TARGET = 1e-2
Q_COLS = 512
KV_COLS = 128
SSD_XBC_COLS = 512
SSD_DT_COLS = 8
HY_COLS = 768
FFN_DIM = 2816
N_EXPERTS = 8
FFN_CHUNK = 256
N_FFN_CHUNKS = FFN_DIM // FFN_CHUNK
LANES = 128
QKV_W = Q_COLS + 2 * KV_COLS
ZX_W = SSD_WIDTH + SSD_XBC_COLS
PROJ_PAD = QKV_W + ZX_W + LANES
VMEM_LIMIT = 56 * 1024 * 1024
TOKEN_TILE = 512
EXPERT_TILE = 512


def _cparams(n_axes):
    return pltpu.CompilerParams(dimension_semantics=("arbitrary",) * n_axes,
                                vmem_limit_bytes=VMEM_LIMIT)


def _silu(v):
    return v / (1.0 + jnp.exp(-v))


def _modnorm(x, g, scale, shift):
    ms = jnp.mean(x * x, axis=-1, keepdims=True)
    return x * lax.rsqrt(ms + EPS) * g * (1.0 + scale) + shift


def _segsum(t, bd):
    hi = t.astype(BF16)
    lo = (t - hi.astype(F32)).astype(BF16)
    return (jnp.dot(hi, bd, preferred_element_type=F32)
            + jnp.dot(lo, bd, preferred_element_type=F32))


def _mod_index(tiles_per_batch, n_batch):
    return lambda i: (jnp.minimum(i // tiles_per_batch, n_batch), 0, 0)


def _adaln_kernel(c_ref, w_ref, b_ref, o_ref):
    s = _silu(c_ref[...]).astype(BF16)
    o_ref[...] = jnp.dot(s, w_ref[...].astype(BF16), preferred_element_type=F32) + b_ref[...]


def _adaln(cc, w_ada, b_ada):
    depth, d, n = w_ada.shape
    r = cc.shape[0]
    tn = 512
    return pl.pallas_call(
        _adaln_kernel,
        grid=(depth, n // tn),
        in_specs=[pl.BlockSpec((r, d), lambda l, j: (0, 0)),
                  pl.BlockSpec((None, d, tn), lambda l, j: (l, 0, j)),
                  pl.BlockSpec((None, 1, tn), lambda l, j: (l, 0, j))],
        out_specs=pl.BlockSpec((None, r, tn), lambda l, j: (l, 0, j)),
        out_shape=jax.ShapeDtypeStruct((depth, r, n), F32),
        compiler_params=_cparams(2),
        name="adaln",
    )(cc, w_ada, b_ada.reshape(depth, 1, n))


def _inproj_kernel(x_ref, mod_ref, g1_ref, w_ref, wh_ref, cos_ref, sin_ref, qg_ref, kg_ref, bd_ref,
                   q_ref, k_ref, v_ref, z_ref, xbc_ref, hyl_ref, hyc_ref, dt_ref, *, n_lat_tiles):
    x = x_ref[...]
    h = _modnorm(x, g1_ref[...], mod_ref[0, 1:2, :], mod_ref[0, 0:1, :]).astype(BF16)
    pq = jnp.dot(h, w_ref[:, 0:QKV_W], preferred_element_type=F32)
    cos = cos_ref[...]
    sin = sin_ref[...]
    lane = lax.broadcasted_iota(jnp.int32, (1, LANES), 1)
    first_half = (lane % 32) < 16

    def rope(t):
        partner = jnp.where(first_half, pltpu.roll(t, LANES - 16, 1), pltpu.roll(t, 16, 1))
        return t * cos + partner * sin

    q = pq[:, 0:Q_COLS]
    qn = q * lax.rsqrt(_segsum(q * q, bd_ref[...]) * (1.0 / HEAD_DIM) + EPS) * qg_ref[...]
    scale = HEAD_DIM ** -0.5
    for j in range(Q_COLS // LANES):
        pair = (rope(qn[:, LANES * j:LANES * (j + 1)]) * scale).astype(BF16)
        q_ref[2 * j] = pair[:, 0:HEAD_DIM]
        q_ref[2 * j + 1] = pair[:, HEAD_DIM:LANES]
    k = pq[:, Q_COLS:Q_COLS + KV_COLS]
    kn = k * lax.rsqrt(_segsum(k * k, bd_ref[0:KV_COLS, 0:KV_COLS]) * (1.0 / HEAD_DIM) + EPS) * kg_ref[...]
    kt = rope(kn).T.astype(BF16)
    vv = pq[:, Q_COLS + KV_COLS:QKV_W].astype(BF16)
    for j in range(ATT_KV_HEADS):
        k_ref[j] = kt[j * HEAD_DIM:(j + 1) * HEAD_DIM, :]
        v_ref[j] = vv[:, j * HEAD_DIM:(j + 1) * HEAD_DIM]
    zx = jnp.dot(h, w_ref[:, QKV_W:QKV_W + ZX_W], preferred_element_type=F32)
    z_ref[...] = zx[:, 0:SSD_WIDTH]
    xbc_ref[...] = zx[:, SSD_WIDTH:ZX_W]
    hy_t = lax.dot_general(wh_ref[...], h, (((1,), (1,)), ((), ())), preferred_element_type=F32)
    is_lat = pl.program_id(0) < n_lat_tiles

    @pl.when(is_lat)
    def _():
        hyl_ref[...] = hy_t

    @pl.when(jnp.logical_not(is_lat))
    def _():
        ctx_len = hyc_ref.shape[2]
        for k in range(hyc_ref.shape[0]):
            hyc_ref[k] = hy_t[:, k * ctx_len:(k + 1) * ctx_len]
    dt_ref[...] = jnp.dot(h, w_ref[:, QKV_W + ZX_W:PROJ_PAD], preferred_element_type=F32)


def _inproj(xa, mods, g1, w_cat, w_hy_t, cos_t, sin_t, qg, kg, bd, n_batch, seq_len, ctx_len):
    t, d = xa.shape
    tm = TOKEN_TILE
    tpb = seq_len // tm
    n_lat = n_batch * tpb
    bpt = tm // ctx_len
    rope_idx = lambda i: (jnp.where(i < n_lat, i % tpb, tpb), 0)
    row = lambda w: pl.BlockSpec((tm, w), lambda i: (i, 0))
    heads = lambda nh: pl.BlockSpec((nh, tm, HEAD_DIM), lambda i: (0, i, 0))
    const = lambda a: pl.BlockSpec(a.shape, lambda i: (0,) * a.ndim)
    lat_tile = lambda i: jnp.minimum(i, n_lat - 1)
    hy_lat = pl.BlockSpec((None, HY_COLS, tm), lambda i: (lat_tile(i) // tpb, 0, lat_tile(i) % tpb))
    hy_ctx = pl.BlockSpec((bpt, HY_COLS, ctx_len), lambda i: (jnp.maximum(i - n_lat, 0), 0, 0))
    return pl.pallas_call(
        functools.partial(_inproj_kernel, n_lat_tiles=n_lat),
        grid=(t // tm,),
        in_specs=[row(d),
                  pl.BlockSpec((1, 6, d), _mod_index(tpb, n_batch)),
                  const(g1), const(w_cat), const(w_hy_t),
                  pl.BlockSpec((tm, LANES), rope_idx), pl.BlockSpec((tm, LANES), rope_idx),
                  const(qg), const(kg), const(bd)],
        out_specs=[heads(ATT_HEADS), pl.BlockSpec((ATT_KV_HEADS, HEAD_DIM, tm), lambda i: (0, 0, i)),
                   heads(ATT_KV_HEADS), row(SSD_WIDTH), row(SSD_XBC_COLS),
                   hy_lat, hy_ctx, row(LANES)],
        out_shape=[jax.ShapeDtypeStruct((ATT_HEADS, t, HEAD_DIM), BF16),
                   jax.ShapeDtypeStruct((ATT_KV_HEADS, HEAD_DIM, t), BF16),
                   jax.ShapeDtypeStruct((ATT_KV_HEADS, t, HEAD_DIM), BF16),
                   jax.ShapeDtypeStruct((t, SSD_WIDTH), F32),
                   jax.ShapeDtypeStruct((t, SSD_XBC_COLS), F32),
                   jax.ShapeDtypeStruct((n_batch, HY_COLS, seq_len), F32),
                   jax.ShapeDtypeStruct((n_batch, HY_COLS, ctx_len), F32),
                   jax.ShapeDtypeStruct((t, LANES), F32)],
        compiler_params=_cparams(1),
        name="inproj",
    )(xa, mods, g1, w_cat, w_hy_t, cos_t, sin_t, qg, kg, bd)


def _attn_kernel(sink_ref, q_ref, *refs, n_q, band):
    if band:
        k_ref, v_ref, kc_ref, vc_ref, o_ref, bias_ref = refs
        seq_len = v_ref.shape[1]
        assert n_q >= 3
    else:
        kc_ref, vc_ref, o_ref = refs
    qb = ATT_BLOCK
    rows = ATT_GROUP * qb
    row_id = lax.broadcasted_iota(jnp.int32, (rows, 1), 0)
    nt = (((1,), (1,)), ((), ()))

    if band:
        @pl.when(pl.program_id(0) == 0)
        def _():
            rel0 = (lax.broadcasted_iota(jnp.int32, (rows, band), 1)
                    - lax.broadcasted_iota(jnp.int32, (rows, band), 0) % qb)
            for var in range(3):
                bias_ref[var] = jnp.where(jnp.abs(rel0 - var * WINDOW) <= WINDOW, 0.0, -jnp.inf)

    for j in range(ATT_KV_HEADS):
        kc = kc_ref[j]
        vc = vc_ref[j]
        snk = jnp.zeros((rows, 1), F32)
        for g in range(ATT_GROUP):
            snk = jnp.where(row_id // qb == g, sink_ref[ATT_GROUP * j + g], snk)

        def block(i, carry, j=j, kc=kc, vc=vc, snk=snk):
            q0 = pl.multiple_of(i * qb, qb)
            qh = jnp.concatenate([q_ref[ATT_GROUP * j + g, pl.ds(q0, qb), :] for g in range(ATT_GROUP)],
                                 axis=0)
            s_ctx = jnp.dot(qh, kc, preferred_element_type=F32)
            m = jnp.maximum(jnp.max(s_ctx, axis=-1, keepdims=True), snk)
            if band:
                k0 = pl.multiple_of(jnp.clip(q0 - WINDOW, 0, seq_len - band), qb)
                var = jnp.where(i == 0, 0, jnp.where(i == n_q - 1, 2, 1))
                s_loc = jnp.dot(qh, k_ref[j, :, pl.ds(k0, band)], preferred_element_type=F32) + bias_ref[var]
                m = jnp.maximum(m, jnp.max(s_loc, axis=-1, keepdims=True))
            p_ctx = jnp.exp(s_ctx - m)
            den = jnp.sum(p_ctx, axis=-1, keepdims=True) + jnp.exp(snk - m)
            o = jnp.dot(p_ctx.astype(BF16), vc, preferred_element_type=F32)
            if band:
                p_loc = jnp.exp(s_loc - m)
                den = den + jnp.sum(p_loc, axis=-1, keepdims=True)
                o = o + jnp.dot(p_loc.astype(BF16), v_ref[j, pl.ds(k0, band), :], preferred_element_type=F32)
            o = o / den
            for g in range(ATT_GROUP):
                c0 = (ATT_GROUP * j + g) * HEAD_DIM
                o_ref[pl.ds(q0, qb), c0:c0 + HEAD_DIM] = o[g * qb:(g + 1) * qb, :]
            return carry

        lax.fori_loop(0, n_q, block, 0, unroll=4)


def _attention(sinks, q, k, v, n_batch, seq_len, ctx_len, latent):
    ctx_blk0 = n_batch * seq_len // ctx_len
    kc_spec = pl.BlockSpec((ATT_KV_HEADS, HEAD_DIM, ctx_len), lambda b: (0, 0, ctx_blk0 + b))
    vc_spec = pl.BlockSpec((ATT_KV_HEADS, ctx_len, HEAD_DIM), lambda b: (0, ctx_blk0 + b, 0))
    smem = pl.BlockSpec(memory_space=pltpu.SMEM)
    scratch = []
    if latent:
        rows = seq_len
        band = ATT_BLOCK + 2 * WINDOW
        in_specs = [smem, pl.BlockSpec((ATT_HEADS, rows, HEAD_DIM), lambda b: (0, b, 0)),
                    pl.BlockSpec((ATT_KV_HEADS, HEAD_DIM, rows), lambda b: (0, 0, b)),
                    pl.BlockSpec((ATT_KV_HEADS, rows, HEAD_DIM), lambda b: (0, b, 0)), kc_spec, vc_spec]
        args = (sinks, q, k, v, k, v)
        scratch = [pltpu.VMEM((3, ATT_GROUP * ATT_BLOCK, band), F32)]
    else:
        rows = ctx_len
        band = 0
        in_specs = [smem, pl.BlockSpec((ATT_HEADS, rows, HEAD_DIM), lambda b: (0, ctx_blk0 + b, 0)),
                    kc_spec, vc_spec]
        args = (sinks, q, k, v)
    return pl.pallas_call(
        functools.partial(_attn_kernel, n_q=rows // ATT_BLOCK, band=band),
        grid=(n_batch,),
        in_specs=in_specs,
        out_specs=pl.BlockSpec((rows, ATT_WIDTH), lambda b: (b, 0)),
        out_shape=jax.ShapeDtypeStruct((n_batch * rows, ATT_WIDTH), F32),
        scratch_shapes=scratch,
        compiler_params=_cparams(1),
        name="attn_latent" if latent else "attn_ctx",
    )(*args)


def _merge_kernel(*refs, n_lat_tiles, with_ctx):
    if with_ctx:
        (attl_ref, attc_ref, syl_ref, syc_ref, hyl_ref, hyc_ref, z_ref, x_ref, mod_ref,
         ga_ref, gs_ref, gh_ref, bd_ref, w_ref, o_ref) = refs
        is_lat = pl.program_id(0) < n_lat_tiles
        att = jnp.where(is_lat, attl_ref[...], attc_ref[...])
        sy = jnp.where(is_lat, syl_ref[...], syc_ref[...])
        hy_ctx = jnp.concatenate([hyc_ref[k] for k in range(hyc_ref.shape[0])], axis=1)
        hy_t = jnp.where(is_lat, hyl_ref[...], hy_ctx)
    else:
        attl_ref, syl_ref, hyl_ref, z_ref, x_ref, mod_ref, ga_ref, gs_ref, gh_ref, bd_ref, w_ref, o_ref = refs
        att = attl_ref[...]
        sy = syl_ref[...]
        hy_t = hyl_ref[...]
    a = att * lax.rsqrt(jnp.mean(att * att, axis=-1, keepdims=True) + EPS) * ga_ref[...]
    s = sy * _silu(z_ref[...])
    s = s * lax.rsqrt(jnp.mean(s * s, axis=-1, keepdims=True) + EPS) * gs_ref[...]
    hy = hy_t.T
    hn = hy * lax.rsqrt(_segsum(hy * hy, bd_ref[...]) * (1.0 / (HYENA_WIDTH // HYENA_GROUPS)) + EPS) * gh_ref[...]
    y = jnp.dot(a.astype(BF16), w_ref[0:ATT_WIDTH, :], preferred_element_type=F32)
    y = y + jnp.dot(s.astype(BF16), w_ref[ATT_WIDTH:ATT_WIDTH + SSD_WIDTH, :], preferred_element_type=F32)
    y = y + jnp.dot(hn.astype(BF16), w_ref[ATT_WIDTH + SSD_WIDTH:, :], preferred_element_type=F32)
    o_ref[...] = x_ref[...] + mod_ref[0, 2:3, :] * y


def _merge(att, sy, hy, z, xa, mods, ga, gs, gh, bd, w_out, n_batch, seq_len):
    d = xa.shape[1]
    tm = TOKEN_TILE
    nl = att[0].shape[0] // tm
    with_ctx = att[1] is not None
    t = att[0].shape[0] + (att[1].shape[0] if with_ctx else 0)
    row = lambda w: pl.BlockSpec((tm, w), lambda i: (i, 0))
    lat_row = lambda w: pl.BlockSpec((tm, w), lambda i: (jnp.minimum(i, nl - 1), 0))
    ctx_row = lambda w: pl.BlockSpec((tm, w), lambda i: (jnp.maximum(i - nl, 0), 0))
    const = lambda a: pl.BlockSpec(a.shape, lambda i: (0,) * a.ndim)
    tpb = seq_len // tm
    lat_tile = lambda i: jnp.minimum(i, nl - 1)
    hy_lat = pl.BlockSpec((None, HYENA_WIDTH, tm), lambda i: (lat_tile(i) // tpb, 0, lat_tile(i) % tpb))
    if with_ctx:
        ctx_len = hy[1].shape[2]
        streams = [att[0], att[1], sy[0], sy[1], hy[0], hy[1]]
        specs = [lat_row(ATT_WIDTH), ctx_row(ATT_WIDTH), lat_row(SSD_WIDTH), ctx_row(SSD_WIDTH), hy_lat,
                 pl.BlockSpec((tm // ctx_len, HYENA_WIDTH, ctx_len), lambda i: (jnp.maximum(i - nl, 0), 0, 0))]
    else:
        streams = [att[0], sy[0], hy[0]]
        specs = [row(ATT_WIDTH), row(SSD_WIDTH), hy_lat]
    return pl.pallas_call(
        functools.partial(_merge_kernel, n_lat_tiles=nl, with_ctx=with_ctx),
        grid=(t // tm,),
        in_specs=specs + [row(SSD_WIDTH), row(d),
                          pl.BlockSpec((1, 6, d), _mod_index(seq_len // tm, n_batch)),
                          const(ga), const(gs), const(gh), const(bd), const(w_out)],
        out_specs=row(d),
        out_shape=jax.ShapeDtypeStruct((t, d), F32),
        compiler_params=_cparams(1),
        name="merge_outproj",
    )(*streams, z, xa, mods, ga, gs, gh, bd, w_out)


def _swiglu_accumulate(h, wg_ref, wu_ref, wd_ref, acc_ref):
    for c in range(N_FFN_CHUNKS):
        cols = slice(c * FFN_CHUNK, (c + 1) * FFN_CHUNK)
        g = jnp.dot(h, wg_ref[:, cols], preferred_element_type=F32)
        u = jnp.dot(h, wu_ref[:, cols], preferred_element_type=F32)
        a = (_silu(g) * u).astype(BF16)
        part = jnp.dot(a, wd_ref[cols, :], preferred_element_type=F32)
        if c == 0:
            acc_ref[...] = part
        else:
            acc_ref[...] += part


def _ffn_kernel(x_ref, mod_ref, g2_ref, wg_ref, wu_ref, wd_ref, o_ref, acc_ref):
    x = x_ref[...]
    h = _modnorm(x, g2_ref[...], mod_ref[0, 4:5, :], mod_ref[0, 3:4, :]).astype(BF16)
    _swiglu_accumulate(h, wg_ref, wu_ref, wd_ref, acc_ref)
    o_ref[...] = x + mod_ref[0, 5:6, :] * acc_ref[...]


def _ffn(xa, mods, g2, wg, wu, wd, n_batch, seq_len):
    t, d = xa.shape
    tm = TOKEN_TILE
    row = pl.BlockSpec((tm, d), lambda i: (i, 0))
    resident = lambda a: pl.BlockSpec(a.shape, lambda i: (0,) * a.ndim, pipeline_mode=pl.Buffered(1))
    return pl.pallas_call(
        _ffn_kernel,
        grid=(t // tm,),
        in_specs=[row, pl.BlockSpec((1, 6, d), _mod_index(seq_len // tm, n_batch)),
                  pl.BlockSpec(g2.shape, lambda i: (0, 0)), resident(wg), resident(wu), resident(wd)],
        out_specs=row,
        out_shape=jax.ShapeDtypeStruct((t, d), F32),
        scratch_shapes=[pltpu.VMEM((tm, d), F32)],
        compiler_params=_cparams(1),
        name="ffn",
    )(xa, mods, g2, wg, wu, wd)


def _router_kernel(x_ref, mod_ref, g2_ref, r_ref, h_ref, idx_ref, wt_ref):
    h = _modnorm(x_ref[...], g2_ref[...], mod_ref[0, 4:5, :], mod_ref[0, 3:4, :])
    h_ref[...] = h
    h_hi = h.astype(BF16)
    h_lo = (h - h_hi.astype(F32)).astype(BF16)
    both = jnp.dot(h_hi, r_ref[...], preferred_element_type=F32)
    logits = (both[:, 0:LANES] + both[:, LANES:2 * LANES]
              + jnp.dot(h_lo, r_ref[:, 0:LANES], preferred_element_type=F32))
    lane = lax.broadcasted_iota(jnp.int32, logits.shape, 1)
    neg = -jnp.inf
    l1 = jnp.where(lane < N_EXPERTS, logits, neg)
    m1 = jnp.max(l1, axis=-1, keepdims=True)
    i1 = jnp.min(jnp.where(l1 == m1, lane, LANES), axis=-1, keepdims=True)
    l2 = jnp.where(lane == i1, neg, l1)
    m2 = jnp.max(l2, axis=-1, keepdims=True)
    i2 = jnp.min(jnp.where(l2 == m2, lane, LANES), axis=-1, keepdims=True)
    e = jnp.exp(m2 - m1)
    w1 = 1.0 / (1.0 + e)
    w2 = e / (1.0 + e)
    idx_ref[...] = jnp.where(lane == 0, i1, jnp.where(lane == 1, i2, 0))
    wt_ref[...] = jnp.where(lane == 0, w1, jnp.where(lane == 1, w2, 0.0))


def _router(xa, mods, g2, r_pad, n_rows, n_batch, seq_len):
    d = xa.shape[1]
    tm = TOKEN_TILE
    row = lambda w: pl.BlockSpec((tm, w), lambda i: (i, 0))
    return pl.pallas_call(
        _router_kernel,
        grid=(n_rows // tm,),
        in_specs=[row(d), pl.BlockSpec((1, 6, d), _mod_index(seq_len // tm, n_batch)),
                  pl.BlockSpec(g2.shape, lambda i: (0, 0)), pl.BlockSpec(r_pad.shape, lambda i: (0, 0))],
        out_specs=[row(d), row(LANES), row(LANES)],
        out_shape=[jax.ShapeDtypeStruct((n_rows, d), F32),
                   jax.ShapeDtypeStruct((n_rows, LANES), jnp.int32),
                   jax.ShapeDtypeStruct((n_rows, LANES), F32)],
        compiler_params=_cparams(1),
        name="moe_router",
    )(xa, mods, g2, r_pad)


def _row_copy(src, src_row, dst, dst_row, sem):
    return pltpu.make_async_copy(src.at[pl.ds(src_row, 1), :], dst.at[pl.ds(dst_row, 1), :], sem)


DMA_ISSUE_UNROLL = 8


def _idx_copy(dest_hbm, dest_smem, sem_idx, tile, slot):
    n = dest_hbm.shape[1]
    half = dest_smem.at[pl.ds(pl.multiple_of(slot * n, n), n)]
    return pltpu.make_async_copy(dest_hbm.at[tile], half, sem_idx.at[slot])


def _dispatch_kernel(pad_tile_ref, dest_hbm, h_ref, xs_out, dest_smem, zeros, sem_idx, sem_rows, sem_zero):
    i = pl.program_id(0)
    n = pl.num_programs(0)
    tm = h_ref.shape[0]
    slot = i % 2

    @pl.when(i == 0)
    def _():
        zeros[...] = jnp.zeros_like(zeros)

        def zero_copy(e):
            return pltpu.make_async_copy(zeros, xs_out.at[pl.ds(pl.multiple_of(pad_tile_ref[e], tm), tm), :],
                                         sem_zero)

        for e in range(2 * N_EXPERTS):
            @pl.when(pad_tile_ref[e] >= 0)
            def _(e=e):
                zero_copy(e).start()
        for e in range(2 * N_EXPERTS):
            @pl.when(pad_tile_ref[e] >= 0)
            def _(e=e):
                zero_copy(e).wait()
        _idx_copy(dest_hbm, dest_smem, sem_idx, 0, 0).start()

    @pl.when(i + 1 < n)
    def _():
        _idx_copy(dest_hbm, dest_smem, sem_idx, i + 1, 1 - slot).start()

    _idx_copy(dest_hbm, dest_smem, sem_idx, i, slot).wait()

    base = slot * (2 * tm)

    def issue(r, carry):
        _row_copy(h_ref, r, xs_out, dest_smem[base + 2 * r], sem_rows).start()
        _row_copy(h_ref, r, xs_out, dest_smem[base + 2 * r + 1], sem_rows).start()
        return carry

    lax.fori_loop(0, tm, issue, 0, unroll=DMA_ISSUE_UNROLL)
    for _ in range(2):
        pltpu.make_async_copy(h_ref, xs_out.at[pl.ds(0, tm), :], sem_rows).wait()


def _dispatch(pad_tile, dest, h, n_slots):
    n_rows, d = h.shape
    tm = TOKEN_TILE
    assert tm == EXPERT_TILE
    return pl.pallas_call(
        _dispatch_kernel,
        grid_spec=pltpu.PrefetchScalarGridSpec(
            num_scalar_prefetch=1,
            grid=(n_rows // tm,),
            in_specs=[pl.BlockSpec(memory_space=pl.ANY), pl.BlockSpec((tm, d), lambda i, pt: (i, 0))],
            out_specs=pl.BlockSpec(memory_space=pl.ANY),
            scratch_shapes=[pltpu.SMEM((4 * tm,), jnp.int32), pltpu.VMEM((tm, d), F32),
                            pltpu.SemaphoreType.DMA((2,)), pltpu.SemaphoreType.DMA(()),
                            pltpu.SemaphoreType.DMA(())]),
        out_shape=jax.ShapeDtypeStruct((n_slots, d), F32),
        compiler_params=_cparams(1),
        name="moe_dispatch",
    )(pad_tile, dest.reshape(n_rows // tm, 2 * tm), h)


def _expert_kernel(te_ref, nused_ref, xs_ref, wg_ref, wu_ref, wd_ref, o_ref, acc_ref):
    del te_ref
    live = pl.program_id(0) < nused_ref[0]

    @pl.when(live)
    def _():
        _swiglu_accumulate(xs_ref[...].astype(BF16), wg_ref, wu_ref, wd_ref, acc_ref)
        o_ref[...] = acc_ref[...]

    @pl.when(jnp.logical_not(live))
    def _():
        o_ref[...] = jnp.zeros_like(o_ref)


def _experts(tile_expert, n_used, xs, wg, wu, wd):
    s, d = xs.shape
    tm = EXPERT_TILE
    row = pl.BlockSpec((tm, d), lambda i, te, nu: (i, 0))
    xs_row = pl.BlockSpec((tm, d), lambda i, te, nu: (jnp.minimum(i, nu[0] - 1), 0))
    wspec = lambda a: pl.BlockSpec((None,) + a.shape[1:], lambda i, te, nu: (te[i], 0, 0))
    return pl.pallas_call(
        _expert_kernel,
        grid_spec=pltpu.PrefetchScalarGridSpec(
            num_scalar_prefetch=2,
            grid=(s // tm,),
            in_specs=[xs_row, wspec(wg), wspec(wu), wspec(wd)],
            out_specs=row,
            scratch_shapes=[pltpu.VMEM((tm, d), F32)]),
        out_shape=jax.ShapeDtypeStruct((s, d), F32),
        compiler_params=_cparams(1),
        name="moe_experts",
    )(tile_expert, n_used, xs, wg, wu, wd)


def _combine_kernel(dest_hbm, eo_hbm, x_ref, wt_ref, mod_ref, o_ref, dest_smem, buf, sem_idx, sem_rows):
    i = pl.program_id(0)
    n = pl.num_programs(0)
    tm = x_ref.shape[0]
    slot = i % 2

    def gather(s):
        base = s * (2 * tm)

        def issue(r, carry):
            _row_copy(eo_hbm, dest_smem[base + 2 * r], buf.at[s, 0], r, sem_rows.at[s]).start()
            _row_copy(eo_hbm, dest_smem[base + 2 * r + 1], buf.at[s, 1], r, sem_rows.at[s]).start()
            return carry

        lax.fori_loop(0, tm, issue, 0, unroll=DMA_ISSUE_UNROLL)

    @pl.when(i == 0)
    def _():
        first = _idx_copy(dest_hbm, dest_smem, sem_idx, 0, 0)
        first.start()
        first.wait()
        gather(0)

        @pl.when(n > 1)
        def _():
            _idx_copy(dest_hbm, dest_smem, sem_idx, 1, 1).start()

    @pl.when(i + 1 < n)
    def _():
        _idx_copy(dest_hbm, dest_smem, sem_idx, i + 1, 1 - slot).wait()

        @pl.when(i + 2 < n)
        def _():
            _idx_copy(dest_hbm, dest_smem, sem_idx, i + 2, slot).start()

        gather(1 - slot)

    for k in range(2):
        pltpu.make_async_copy(eo_hbm.at[pl.ds(0, tm), :], buf.at[slot, k], sem_rows.at[slot]).wait()
    wt = wt_ref[...]
    y = wt[:, 0:1] * buf[slot, 0] + wt[:, 1:2] * buf[slot, 1]
    o_ref[...] = x_ref[...] + mod_ref[0, 5:6, :] * y


def _combine(dest, eo, xa, wts, mods, n_rows, n_batch, seq_len):
    d = xa.shape[1]
    tm = TOKEN_TILE
    row = lambda w: pl.BlockSpec((tm, w), lambda i: (i, 0))
    return pl.pallas_call(
        _combine_kernel,
        grid=(n_rows // tm,),
        in_specs=[pl.BlockSpec(memory_space=pl.ANY), pl.BlockSpec(memory_space=pl.ANY), row(d), row(LANES),
                  pl.BlockSpec((1, 6, d), _mod_index(seq_len // tm, n_batch))],
        out_specs=row(d),
        out_shape=jax.ShapeDtypeStruct((n_rows, d), F32),
        scratch_shapes=[pltpu.SMEM((4 * tm,), jnp.int32), pltpu.VMEM((2, 2, tm, d), F32),
                        pltpu.SemaphoreType.DMA((2,)), pltpu.SemaphoreType.DMA((2,))],
        compiler_params=_cparams(1),
        name="moe_combine",
    )(dest.reshape(n_rows // tm, 2 * tm), eo, xa, wts, mods)


def _moe(xa, mods, g2, r_pad, wg, wu, wd, n_rows, n_batch, seq_len):
    h, idx, wts = _router(xa, mods, g2, r_pad, n_rows, n_batch, seq_len)
    tm = EXPERT_TILE
    e_flat = idx[:, :2].reshape(-1)
    onehot = (e_flat[:, None] == jnp.arange(N_EXPERTS, dtype=jnp.int32)[None, :]).astype(jnp.int32)
    csum = jnp.cumsum(onehot, axis=0)
    counts = csum[-1]
    rank = jnp.sum(onehot * csum, axis=1) - 1
    padded = ((counts + tm - 1) // tm) * tm
    ends = jnp.cumsum(padded)
    starts = ends - padded
    dest = (jnp.sum(onehot * starts[None, :], axis=1) + rank).astype(jnp.int32)
    n_slots = 2 * n_rows + N_EXPERTS * tm
    tile_start = jnp.arange(n_slots // tm, dtype=jnp.int32) * tm
    tile_expert = jnp.minimum(jnp.sum((tile_start[:, None] >= ends[None, :]).astype(jnp.int32), axis=1),
                              N_EXPERTS - 1).astype(jnp.int32)
    n_used = (ends[-1:] // tm).astype(jnp.int32)
    tail = ends[-1] + jnp.arange(N_EXPERTS, dtype=ends.dtype) * tm
    pad_tile = jnp.concatenate([jnp.where(padded > 0, ends - tm, -1),
                                jnp.where(tail < n_slots, tail, -1)]).astype(jnp.int32)
    xs = _dispatch(pad_tile, dest, h, n_slots)
    eo = _experts(tile_expert, n_used, xs, wg, wu, wd)
    return _combine(dest, eo, xa, wts, mods, n_rows, n_batch, seq_len)


def _softplus(v):
    return jnp.maximum(v, 0.0) + jnp.log1p(jnp.exp(-jnp.abs(v)))


def _ssd_kernel(xl_ref, dl_ref, xc_ref, dc_ref, cw_ref, cb_ref, dtb_ref, alogc_ref, dsk_ref, *rest, want_ctx):
    if want_ctx:
        yl_ref, yc_ref, *rest = rest
    else:
        yl_ref, *rest = rest
        yc_ref = None
    *lat_s, st_ref = rest
    lat_s, ctx_s = lat_s[:6], lat_s[6:]
    ck = SSD_CHUNK
    hp = SSD_HEAD_DIM
    ns = SSD_STATE
    n_col = 2 * SSD_HEADS
    hi = lax.Precision.HIGHEST
    row = lax.broadcasted_iota(jnp.int32, (ck, 1), 0)
    li = lax.broadcasted_iota(jnp.int32, (ck, ck), 0)
    si = lax.broadcasted_iota(jnp.int32, (ck, ck), 1)
    masks = (si <= li, si >= li)
    tris = (masks[0].astype(F32), masks[1].astype(F32))
    a_col = -jnp.exp(alogc_ref[...])
    dskip = dsk_ref[...]

    def prep(raw_ref, dtraw_ref, xs_s, dtt_s, lar_s, lac_s, bt_s, sc_s, y_ref):
        n = raw_ref.shape[0]
        nk = n // ck

        def body(k, carry):
            r0 = pl.multiple_of(k * ck, ck)
            a = raw_ref[pl.ds(r0, ck), :]
            top = raw_ref[pl.ds(pl.multiple_of(jnp.maximum(r0 - 8, 0), 8), 8), :][7:8, :]
            bot = raw_ref[pl.ds(pl.multiple_of(jnp.minimum(r0 + ck, n - 8), 8), 8), :][0:1, :]
            top = jnp.where(k > 0, top, 0.0)
            bot = jnp.where(k < nk - 1, bot, 0.0)
            prev = jnp.where(row == 0, top, pltpu.roll(a, 1, 0))
            nxt = jnp.where(row == ck - 1, bot, pltpu.roll(a, ck - 1, 0))
            xs = _silu(prev * cw_ref[0:1, :] + a * cw_ref[1:2, :] + nxt * cw_ref[2:3, :] + cb_ref[...])
            xs_s[pl.ds(r0, ck), :] = xs
            dt = _softplus(dtraw_ref[pl.ds(r0, ck), :] + dtb_ref[...])
            dt_t = dt.T[0:n_col, :]
            dtt_s[:, pl.ds(r0, ck)] = dt_t
            for dr in range(2):
                la_r = jnp.dot(dt_t * a_col, tris[1 - dr], precision=hi, preferred_element_type=F32)
                lar_s[dr, :, pl.ds(r0, ck)] = la_r
                if y_ref is not None:
                    lac_s[dr, pl.ds(r0, ck), :] = jnp.concatenate(
                        [la_r, jnp.zeros((ck - n_col, ck), F32)], axis=0).T
            for g in range(SSD_GROUPS):
                b_t = xs[:, SSD_WIDTH + g * ns:SSD_WIDTH + (g + 1) * ns].T
                bt_s[g, :, pl.ds(r0, ck)] = b_t
                if y_ref is not None:
                    c0 = SSD_WIDTH + SSD_GROUPS * ns + g * ns
                    sc_s[g, pl.ds(r0, ck), :] = jnp.dot(xs[:, c0:c0 + ns].astype(BF16), b_t.astype(BF16),
                                                        preferred_element_type=F32)
            if y_ref is not None:
                y_ref[pl.ds(r0, ck), :] = xs[:, 0:SSD_WIDTH] * dskip
            return carry

        lax.fori_loop(0, nk, body, 0, unroll=4)

    def run(xs_s, dtt_s, lar_s, lac_s, bt_s, sc_s, y_ref):
        nk = xs_s.shape[0] // ck

        def one(kk, dr):
            r0 = pl.multiple_of(kk * ck, ck)
            xc = xs_s[pl.ds(r0, ck), :]
            dtr = dtt_s[:, pl.ds(r0, ck)]
            la_r = lar_s[dr, :, pl.ds(r0, ck)]
            la_end = la_r[:, ck - 1:ck] if dr == 0 else la_r[:, 0:1]
            if y_ref is not None:
                la_c = lac_s[dr, pl.ds(r0, ck), :]
            ys = []
            for g in range(SSD_GROUPS):
                b_t = bt_s[g, :, pl.ds(r0, ck)]
                c0 = SSD_WIDTH + SSD_GROUPS * ns + g * ns
                c_g = xc[:, c0:c0 + ns].astype(BF16)
                if y_ref is not None:
                    scores = sc_s[g, pl.ds(r0, ck), :]
                for hh in range(SSD_HEADS // SSD_GROUPS):
                    h = g * (SSD_HEADS // SSD_GROUPS) + hh
                    col = dr * SSD_HEADS + h
                    xh = xc[:, h * hp:(h + 1) * hp].astype(BF16)
                    dt_row = dtr[col:col + 1, :]
                    le = la_end[col:col + 1, :]
                    st = st_ref[col]
                    if y_ref is not None:
                        la_col = la_c[:, col:col + 1]
                        decay = jnp.exp(jnp.where(masks[dr], la_col - la_r[col:col + 1, :], -jnp.inf))
                        y = jnp.dot((scores * decay * dt_row).astype(BF16), xh, preferred_element_type=F32)
                        y = y + jnp.dot(c_g, st.astype(BF16), preferred_element_type=F32) * jnp.exp(la_col)
                        ys.append(y)
                    bw = (b_t * (dt_row * jnp.exp(le - la_r[col:col + 1, :]))).astype(BF16)
                    st_ref[col] = st * jnp.exp(le) + jnp.dot(bw, xh, preferred_element_type=F32)
            if y_ref is not None:
                y_ref[pl.ds(r0, ck), :] += jnp.concatenate(ys, axis=1)

        def body(k, carry):
            one(k, 0)
            one(nk - 1 - k, 1)
            return carry

        lax.fori_loop(0, nk, body, 0, unroll=2)

    prep(xl_ref, dl_ref, *lat_s, yl_ref)
    prep(xc_ref, dc_ref, *ctx_s, yc_ref)
    st_ref[...] = jnp.zeros_like(st_ref)
    run(*ctx_s, yc_ref)
    run(*lat_s, yl_ref)


def _ssd(xbc, dtp, conv_w, conv_b, dt_bias, a_log, d_skip, n_batch, seq_len, ctx_len, want_ctx):
    ctx0 = n_batch * seq_len // ctx_len
    pad = lambda v: jnp.pad(v.reshape(1, -1), ((0, 0), (0, LANES - v.size)))
    lat = lambda w: pl.BlockSpec((seq_len, w), lambda b: (b, 0))
    ctx = lambda w: pl.BlockSpec((ctx_len, w), lambda b: (ctx0 + b, 0))
    const = lambda a: pl.BlockSpec(a.shape, lambda b: (0,) * a.ndim)
    consts = (conv_w, conv_b.reshape(1, -1), pad(dt_bias), a_log.reshape(-1, 1),
              jnp.repeat(d_skip, SSD_HEAD_DIM).reshape(1, -1))
    stream_scratch = lambda n: [pltpu.VMEM((n, SSD_XBC_COLS), F32), pltpu.VMEM((2 * SSD_HEADS, n), F32),
                                pltpu.VMEM((2, 2 * SSD_HEADS, n), F32), pltpu.VMEM((2, n, SSD_CHUNK), F32),
                                pltpu.VMEM((SSD_GROUPS, SSD_STATE, n), F32),
                                pltpu.VMEM((SSD_GROUPS, n, SSD_CHUNK), F32)]
    out_specs = [lat(SSD_WIDTH)]
    out_shape = [jax.ShapeDtypeStruct((n_batch * seq_len, SSD_WIDTH), F32)]
    if want_ctx:
        out_specs.append(pl.BlockSpec((ctx_len, SSD_WIDTH), lambda b: (b, 0)))
        out_shape.append(jax.ShapeDtypeStruct((n_batch * ctx_len, SSD_WIDTH), F32))
    return pl.pallas_call(
        functools.partial(_ssd_kernel, want_ctx=want_ctx),
        grid=(n_batch,),
        in_specs=[lat(SSD_XBC_COLS), lat(LANES), ctx(SSD_XBC_COLS), ctx(LANES)] + [const(a) for a in consts],
        out_specs=out_specs,
        out_shape=out_shape,
        scratch_shapes=[*stream_scratch(seq_len), *stream_scratch(ctx_len),
                        pltpu.VMEM((2 * SSD_HEADS, SSD_STATE, SSD_HEAD_DIM), F32)],
        compiler_params=_cparams(1),
        name="ssd",
    )(xbc, dtp, xbc, dtp, *consts)


HY_BLOCK = 256
HY_CH_STEP = 8


def _hyena_tables(seq_len):
    f32 = np.float32
    nj = 2 * seq_len
    lag = np.arange(nj, dtype=np.int32) - seq_len
    dist = np.abs(lag)
    pos = np.minimum(dist, seq_len - 1)
    t = np.linspace(0.0, 1.0, seq_len, dtype=f32)[pos]
    w = (f32(2.0 * math.pi / seq_len) * np.arange(seq_len, dtype=f32))[pos]
    bands = (HYENA_POS_DIM - 1) // 2
    freqs = np.linspace(1e-4, bands - 1, bands, dtype=f32)[None, :]
    ang = (freqs * w[:, None]).astype(f32)
    z = np.concatenate([t[:, None], np.cos(ang), -np.sin(ang)], axis=-1).astype(f32)
    zt = np.pad(z.T, ((0, (-HYENA_POS_DIM) % 8), (0, 0)))
    deltas = np.abs(np.linspace(math.log(HYENA_DECAY_TARGET) / HYENA_SLOW_DECAY,
                                math.log(HYENA_DECAY_TARGET) / HYENA_FAST_DECAY, HYENA_WIDTH, dtype=f32))
    dec = (np.exp(-t[None, :] * deltas[:, None]) * (dist < seq_len)[None, :]).astype(f32)
    fwd = (lag >= 0).astype(f32)[None, :]
    return jnp.asarray(zt), jnp.asarray(dec), jnp.asarray(fwd)


def _hyfilt_kernel(zt_ref, dec_ref, fwd_ref, w1_ref, b1_ref, f1_ref, w2_ref, b2_ref, f2_ref, w3_ref, o_ref):
    hi = lax.Precision.HIGHEST
    h = jnp.sin(f1_ref[...] * (jnp.dot(w1_ref[...], zt_ref[...], precision=hi, preferred_element_type=F32)
                               + b1_ref[...]))
    h = jnp.sin(f2_ref[...] * (jnp.dot(w2_ref[...], h, precision=hi, preferred_element_type=F32) + b2_ref[...]))
    hw = jnp.dot(w3_ref[...], h, precision=hi, preferred_element_type=F32)
    fwd = fwd_ref[...] > 0.5
    dec = dec_ref[...]
    nw = HYENA_WIDTH
    for o in range(HYENA_ORDER):
        o_ref[o] = jnp.where(fwd, hw[o * nw:(o + 1) * nw], hw[(HYENA_ORDER + o) * nw:(HYENA_ORDER + o + 1) * nw]) * dec


def _hyena_filters(tables, w1, b1, f1, w2, b2, f2, w3):
    zt, dec, fwd = tables
    nj = zt.shape[1]
    tj = 512
    col = lambda v: v.reshape(-1, 1)
    w1t = jnp.pad(w1.T, ((0, 0), (0, zt.shape[0] - w1.shape[0])))
    consts = (w1t, col(b1), col(f1), w2.T, col(b2), col(f2), w3.T)
    lanes = lambda a: pl.BlockSpec((a.shape[0], tj), lambda j: (0, j))
    const = lambda a: pl.BlockSpec(a.shape, lambda j: (0, 0))
    return pl.pallas_call(
        _hyfilt_kernel,
        grid=(nj // tj,),
        in_specs=[lanes(zt), lanes(dec), lanes(fwd)] + [const(a) for a in consts],
        out_specs=pl.BlockSpec((HYENA_ORDER, HYENA_WIDTH, tj), lambda j: (0, 0, j)),
        out_shape=jax.ShapeDtypeStruct((HYENA_ORDER, HYENA_WIDTH, nj), F32),
        compiler_params=_cparams(1),
        name="hyena_filters",
    )(zt, dec, fwd, *consts)


def _hyconv_kernel(cw_ref, cb_ref, hb_ref, v_ref, x1_ref, x2_ref, kf_ref, o_ref):
    n_b, n_ch, seq_len = v_ref.shape
    nb = seq_len // HY_BLOCK
    blk = HY_BLOCK
    c_base = pl.program_id(0) * n_ch
    lane = lax.broadcasted_iota(jnp.int32, (1, seq_len), 1)

    def sconv(x, ch):
        prev = jnp.where(lane == 0, 0.0, pltpu.roll(x, 1, 1))
        nxt = jnp.where(lane == seq_len - 1, 0.0, pltpu.roll(x, seq_len - 1, 1))
        return prev * cw_ref[0, ch] + x * cw_ref[1, ch] + nxt * cw_ref[2, ch] + cb_ref[ch]

    def long_conv(vals, kf_row):
        half = blk // 2
        skew = pltpu.roll(jnp.broadcast_to(kf_row, (half, 2 * seq_len)), 0, 1, stride=1,
                          stride_axis=0).astype(BF16)
        vb = vals.astype(BF16)
        acc = [None] * nb
        for d in range(-(nb - 1), nb):
            a0 = seq_len + d * blk
            tt = jnp.concatenate([skew[:, a0:a0 + blk], skew[:, a0 - half:a0 - half + blk]], axis=0)
            sis = list(range(max(0, -d), min(nb, nb - d)))
            lhs = [vb[:, s * blk:(s + 1) * blk] for s in sis]
            lhs = lhs[0] if len(lhs) == 1 else jnp.concatenate(lhs, axis=0)
            out = jnp.dot(lhs, tt, preferred_element_type=F32)
            for idx, s in enumerate(sis):
                piece = out[idx * n_b:(idx + 1) * n_b]
                acc[s + d] = piece if acc[s + d] is None else acc[s + d] + piece
        return acc[0] if nb == 1 else jnp.concatenate(acc, axis=1)

    def channel(cc, carry):
        ch = c_base + cc
        v = sconv(v_ref[:, cc, :], ch)
        x1 = sconv(x1_ref[:, cc, :], HYENA_WIDTH + ch)
        x2 = sconv(x2_ref[:, cc, :], 2 * HYENA_WIDTH + ch)
        z = x1 * (long_conv(v, kf_ref[0, pl.ds(cc, 1), :]) + v * hb_ref[0, ch])
        o_ref[:, cc, :] = x2 * (long_conv(z, kf_ref[1, pl.ds(cc, 1), :]) + z * hb_ref[1, ch])
        return carry

    lax.fori_loop(0, n_ch, channel, 0)


def _hyena_conv(hy, kf, conv_w, conv_b, hy_bias):
    n_batch, _, seq_len = hy.shape
    cs = HY_CH_STEP
    nw = HYENA_WIDTH
    stream = lambda k: pl.BlockSpec((n_batch, cs, seq_len), lambda c: (0, k * (nw // cs) + c, 0))
    smem = pl.BlockSpec(memory_space=pltpu.SMEM)
    return pl.pallas_call(
        _hyconv_kernel,
        grid=(nw // cs,),
        in_specs=[smem, smem, smem, stream(0), stream(1), stream(2),
                  pl.BlockSpec((HYENA_ORDER, cs, 2 * seq_len), lambda c: (0, c, 0))],
        out_specs=pl.BlockSpec((n_batch, cs, seq_len), lambda c: (0, c, 0)),
        out_shape=jax.ShapeDtypeStruct((n_batch, nw, seq_len), F32),
        compiler_params=_cparams(1),
        name="hyena_conv",
    )(conv_w, conv_b, hy_bias, hy, hy, hy, kf)


def _rope_tables(seq_len, extra):
    rows = seq_len // GRID_W
    row = jnp.repeat(jnp.arange(rows, dtype=F32), GRID_W)
    col = jnp.tile(jnp.arange(GRID_W, dtype=F32), rows)
    inv = ROPE_THETA ** (-jnp.arange(0, ROPE_AXIS_DIM, 2, dtype=F32) / ROPE_AXIS_DIM)
    ang = jnp.stack([row[:, None] * inv, col[:, None] * inv], axis=1)
    cos = jnp.cos(ang)
    sin = jnp.sin(ang)
    cos_h = jnp.concatenate([cos, cos], axis=-1).reshape(seq_len, HEAD_DIM)
    sin_h = jnp.concatenate([-sin, sin], axis=-1).reshape(seq_len, HEAD_DIM)
    cos_t = jnp.concatenate([jnp.tile(cos_h, (1, LANES // HEAD_DIM)), jnp.ones((extra, LANES), F32)], axis=0)
    sin_t = jnp.concatenate([jnp.tile(sin_h, (1, LANES // HEAD_DIM)), jnp.zeros((extra, LANES), F32)], axis=0)
    return cos_t, sin_t


def _block_diag_ones(n, seg):
    i = jnp.arange(n) // seg
    return (i[:, None] == i[None, :]).astype(BF16)


def kernel(x, c, ctx, c_ctx, w_ada, b_ada, norm1, norm2, w_in, w_out, q_norm, k_norm, att_sinks, att_out_norm, ssd_conv_w, ssd_conv_b, ssd_dt_bias, ssd_a_log, ssd_d, ssd_norm, hy_conv_w, hy_conv_b, hy_w1, hy_b1, hy_f1, hy_w2, hy_b2, hy_f2, hy_w3, hy_bias, hy_out_norm, ffn_w_gate, ffn_w_up, ffn_w_down, moe_router, moe_w_gate, moe_w_up, moe_w_down):
    n_batch, seq_len, d = x.shape
    ctx_len = ctx.shape[1]
    n_lat = n_batch * seq_len
    n_ctx = n_batch * ctx_len
    depth = w_in.shape[0]
    xa = jnp.concatenate([x.reshape(n_lat, d), ctx.reshape(n_ctx, d)], axis=0)

    cc = jnp.concatenate([c, c_ctx[None, :]], axis=0)
    pad_rows = (-cc.shape[0]) % 8
    cc = jnp.pad(cc, ((0, pad_rows), (0, 0)))
    mods_all = _adaln(cc, w_ada, b_ada)[:, :n_batch + 1].reshape(depth, n_batch + 1, 6, d)

    cos_t, sin_t = _rope_tables(seq_len, TOKEN_TILE)
    bd_q = _block_diag_ones(Q_COLS, HEAD_DIM)
    bd_h = _block_diag_ones(HYENA_WIDTH, HYENA_WIDTH // HYENA_GROUPS)
    hy_tab_l = _hyena_tables(seq_len)
    hy_tab_c = _hyena_tables(ctx_len)

    for i in range(depth):
        last = i == depth - 1
        j = i // 2
        mods = mods_all[i]
        wi = w_in[i]
        c_dt = QKV_W + ZX_W
        w_cat = jnp.concatenate([wi[:, :c_dt], wi[:, c_dt:c_dt + SSD_DT_COLS],
                                 jnp.zeros((d, LANES - SSD_DT_COLS), F32)], axis=1).astype(BF16)
        w_hy_t = wi[:, c_dt + SSD_DT_COLS:].T.astype(BF16)
        qg = jnp.tile(q_norm[i], Q_COLS // HEAD_DIM)[None, :]
        kg = jnp.tile(k_norm[i], KV_COLS // HEAD_DIM)[None, :]
        q, k, v, z, xbc, hy_l, hy_c, dtp = _inproj(xa, mods, norm1[i][None, :], w_cat, w_hy_t, cos_t, sin_t,
                                                   qg, kg, bd_q, n_batch, seq_len, ctx_len)

        att_l = _attention(att_sinks[i], q, k, v, n_batch, seq_len, ctx_len, True)
        ssd_out = _ssd(xbc, dtp, ssd_conv_w[i], ssd_conv_b[i], ssd_dt_bias[i], ssd_a_log[i], ssd_d[i],
                       n_batch, seq_len, ctx_len, not last)
        filt = (hy_w1[i], hy_b1[i], hy_f1[i], hy_w2[i], hy_b2[i], hy_f2[i], hy_w3[i])
        hyo_l = _hyena_conv(hy_l, _hyena_filters(hy_tab_l, *filt), hy_conv_w[i], hy_conv_b[i], hy_bias[i])
        if last:
            att = (att_l, None)
            sy = (ssd_out[0], None)
            hyo = (hyo_l, None)
            n_rows = n_lat
        else:
            att = (att_l, _attention(att_sinks[i], q, k, v, n_batch, seq_len, ctx_len, False))
            sy = tuple(ssd_out)
            hyo_c = _hyena_conv(hy_c, _hyena_filters(hy_tab_c, *filt), hy_conv_w[i], hy_conv_b[i], hy_bias[i])
            hyo = (hyo_l, hyo_c)
            n_rows = n_lat + n_ctx
        xa = _merge(att, sy, hyo, z, xa, mods, att_out_norm[i][None, :], ssd_norm[i][None, :],
                    hy_out_norm[i][None, :], bd_h, w_out[i].astype(BF16), n_batch, seq_len)

        g2 = norm2[i][None, :]
        if i % 2 == 0:
            xa = _ffn(xa, mods, g2, ffn_w_gate[j].astype(BF16), ffn_w_up[j].astype(BF16),
                      ffn_w_down[j].astype(BF16), n_batch, seq_len)
        else:
            r_full = jnp.pad(moe_router[j], ((0, 0), (0, LANES - N_EXPERTS)))
            r_hi = r_full.astype(BF16)
            r_pad = jnp.concatenate([r_hi, (r_full - r_hi.astype(F32)).astype(BF16)], axis=1)
            xa = _moe(xa, mods, g2, r_pad, moe_w_gate[j].astype(BF16), moe_w_up[j].astype(BF16),
                      moe_w_down[j].astype(BF16), n_rows, n_batch, seq_len)
    return xa[:n_lat].reshape(n_batch, seq_len, d)
```

```python
import functools
import math

import jax
import jax.numpy as jnp
import numpy as np
from jax import lax
from jax.experimental import pallas as pl
from jax.experimental.pallas import tpu as pltpu

F32 = jnp.float32
BF16 = jnp.bfloat16

D_MODEL = 1024
DEPTH = 4
GRID_W = 64
EPS = 1e-6
HEAD_DIM = 64
ATT_WIDTH = 512
ATT_HEADS = 8
ATT_KV_HEADS = 2
ATT_GROUP = 4
WINDOW = 128
ATT_BLOCK = 128
ROPE_THETA = 10000.0
ROPE_AXIS_DIM = 32
SSD_WIDTH = 256
SSD_HEAD_DIM = 64
SSD_HEADS = 4
SSD_STATE = 64
SSD_GROUPS = 2
SSD_CHUNK = 128
HYENA_WIDTH = 256
HYENA_GROUPS = 4
HYENA_ORDER = 2
HYENA_POS_DIM = 33
HYENA_FAST_DECAY = 0.3
HYENA_SLOW_DECAY = 1.5
HYENA_DECAY_TARGET = 1e-2
Q_COLS = 512
KV_COLS = 128
SSD_XBC_COLS = 512
SSD_DT_COLS = 8
HY_COLS = 768
FFN_DIM = 2816
N_EXPERTS = 8
FFN_CHUNK = 256
N_FFN_CHUNKS = FFN_DIM // FFN_CHUNK
LANES = 128
QKV_W = Q_COLS + 2 * KV_COLS
ZX_W = SSD_WIDTH + SSD_XBC_COLS
PROJ_PAD = QKV_W + ZX_W + LANES
VMEM_LIMIT = 56 * 1024 * 1024
TOKEN_TILE = 512
INPROJ_TILE = 1024
EXPERT_TILE = 512


def _cparams(n_axes):
    return pltpu.CompilerParams(dimension_semantics=("arbitrary",) * n_axes,
                                vmem_limit_bytes=VMEM_LIMIT)


def _silu(v):
    return v / (1.0 + jnp.exp(-v))


def _modnorm(x, g, scale, shift):
    ms = jnp.mean(x * x, axis=-1, keepdims=True)
    return x * lax.rsqrt(ms + EPS) * g * (1.0 + scale) + shift


def _segsum(t, bd):
    hi = t.astype(BF16)
    lo = (t - hi.astype(F32)).astype(BF16)
    return (jnp.dot(hi, bd, preferred_element_type=F32)
            + jnp.dot(lo, bd, preferred_element_type=F32))


def _mod_index(tiles_per_batch, n_batch):
    return lambda i: (jnp.minimum(i // tiles_per_batch, n_batch), 0, 0)


def _adaln_kernel(c_ref, w_ref, b_ref, o_ref):
    s = _silu(c_ref[...]).astype(BF16)
    o_ref[...] = jnp.dot(s, w_ref[...].astype(BF16), preferred_element_type=F32) + b_ref[...]


def _adaln(cc, w_ada, b_ada):
    depth, d, n = w_ada.shape
    r = cc.shape[0]
    tn = 512
    return pl.pallas_call(
        _adaln_kernel,
        grid=(depth, n // tn),
        in_specs=[pl.BlockSpec((r, d), lambda l, j: (0, 0)),
                  pl.BlockSpec((None, d, tn), lambda l, j: (l, 0, j)),
                  pl.BlockSpec((None, 1, tn), lambda l, j: (l, 0, j))],
        out_specs=pl.BlockSpec((None, r, tn), lambda l, j: (l, 0, j)),
        out_shape=jax.ShapeDtypeStruct((depth, r, n), F32),
        compiler_params=_cparams(2),
        name="adaln",
    )(cc, w_ada, b_ada.reshape(depth, 1, n))


def _inproj_kernel(x_ref, mod_ref, g1_ref, w_ref, wh_ref, cos_ref, sin_ref, qg_ref, kg_ref, bd_ref,
                   q_ref, k_ref, v_ref, z_ref, xbc_ref, hyl_ref, hyc_ref, dt_ref, *, n_lat_tiles):
    x = x_ref[...]
    h = _modnorm(x, g1_ref[...], mod_ref[0, 1:2, :], mod_ref[0, 0:1, :]).astype(BF16)
    pq = jnp.dot(h, w_ref[:, 0:QKV_W], preferred_element_type=F32)
    cos = cos_ref[...]
    sin = sin_ref[...]
    lane = lax.broadcasted_iota(jnp.int32, (1, LANES), 1)
    first_half = (lane % 32) < 16

    def rope(t):
        partner = jnp.where(first_half, pltpu.roll(t, LANES - 16, 1), pltpu.roll(t, 16, 1))
        return t * cos + partner * sin

    q = pq[:, 0:Q_COLS]
    qn = q * lax.rsqrt(_segsum(q * q, bd_ref[...]) * (1.0 / HEAD_DIM) + EPS) * qg_ref[...]
    scale = HEAD_DIM ** -0.5
    for j in range(Q_COLS // LANES):
        pair = (rope(qn[:, LANES * j:LANES * (j + 1)]) * scale).astype(BF16)
        q_ref[2 * j] = pair[:, 0:HEAD_DIM]
        q_ref[2 * j + 1] = pair[:, HEAD_DIM:LANES]
    k = pq[:, Q_COLS:Q_COLS + KV_COLS]
    kn = k * lax.rsqrt(_segsum(k * k, bd_ref[0:KV_COLS, 0:KV_COLS]) * (1.0 / HEAD_DIM) + EPS) * kg_ref[...]
    kt = rope(kn).T.astype(BF16)
    vv = pq[:, Q_COLS + KV_COLS:QKV_W].astype(BF16)
    for j in range(ATT_KV_HEADS):
        k_ref[j] = kt[j * HEAD_DIM:(j + 1) * HEAD_DIM, :]
        v_ref[j] = vv[:, j * HEAD_DIM:(j + 1) * HEAD_DIM]
    zx = jnp.dot(h, w_ref[:, QKV_W:QKV_W + ZX_W], preferred_element_type=F32)
    z_ref[...] = zx[:, 0:SSD_WIDTH]
    xbc_ref[...] = zx[:, SSD_WIDTH:ZX_W]
    hy_t = lax.dot_general(wh_ref[...], h, (((1,), (1,)), ((), ())), preferred_element_type=F32)
    is_lat = pl.program_id(0) < n_lat_tiles

    @pl.when(is_lat)
    def _():
        hyl_ref[...] = hy_t

    @pl.when(jnp.logical_not(is_lat))
    def _():
        ctx_len = hyc_ref.shape[2]
        for k in range(hyc_ref.shape[0]):
            hyc_ref[k] = hy_t[:, k * ctx_len:(k + 1) * ctx_len]
    dt_ref[...] = jnp.dot(h, w_ref[:, QKV_W + ZX_W:PROJ_PAD], preferred_element_type=F32)


def _inproj(xa, mods, g1, w_cat, w_hy_t, cos_t, sin_t, qg, kg, bd, n_batch, seq_len, ctx_len):
    t, d = xa.shape
    tm = INPROJ_TILE
    tpb = seq_len // tm
    n_lat = n_batch * tpb
    bpt = tm // ctx_len
    rope_idx = lambda i: (jnp.where(i < n_lat, i % tpb, tpb), 0)
    row = lambda w: pl.BlockSpec((tm, w), lambda i: (i, 0))
    heads = lambda nh: pl.BlockSpec((nh, tm, HEAD_DIM), lambda i: (0, i, 0))
    const = lambda a: pl.BlockSpec(a.shape, lambda i: (0,) * a.ndim)
    lat_tile = lambda i: jnp.minimum(i, n_lat - 1)
    hy_lat = pl.BlockSpec((None, HY_COLS, tm), lambda i: (lat_tile(i) // tpb, 0, lat_tile(i) % tpb))
    hy_ctx = pl.BlockSpec((bpt, HY_COLS, ctx_len), lambda i: (jnp.maximum(i - n_lat, 0), 0, 0))
    return pl.pallas_call(
        functools.partial(_inproj_kernel, n_lat_tiles=n_lat),
        grid=(t // tm,),
        in_specs=[row(d),
                  pl.BlockSpec((1, 6, d), _mod_index(tpb, n_batch)),
                  const(g1), const(w_cat), const(w_hy_t),
                  pl.BlockSpec((tm, LANES), rope_idx), pl.BlockSpec((tm, LANES), rope_idx),
                  const(qg), const(kg), const(bd)],
        out_specs=[heads(ATT_HEADS), pl.BlockSpec((ATT_KV_HEADS, HEAD_DIM, tm), lambda i: (0, 0, i)),
                   heads(ATT_KV_HEADS), row(SSD_WIDTH), row(SSD_XBC_COLS),
                   hy_lat, hy_ctx, row(LANES)],
        out_shape=[jax.ShapeDtypeStruct((ATT_HEADS, t, HEAD_DIM), BF16),
                   jax.ShapeDtypeStruct((ATT_KV_HEADS, HEAD_DIM, t), BF16),
                   jax.ShapeDtypeStruct((ATT_KV_HEADS, t, HEAD_DIM), BF16),
                   jax.ShapeDtypeStruct((t, SSD_WIDTH), F32),
                   jax.ShapeDtypeStruct((t, SSD_XBC_COLS), F32),
                   jax.ShapeDtypeStruct((n_batch, HY_COLS, seq_len), F32),
                   jax.ShapeDtypeStruct((n_batch, HY_COLS, ctx_len), F32),
                   jax.ShapeDtypeStruct((t, LANES), F32)],
        compiler_params=_cparams(1),
        name="inproj",
    )(xa, mods, g1, w_cat, w_hy_t, cos_t, sin_t, qg, kg, bd)


def _attn_kernel(sink_ref, q_ref, *refs, n_q, band):
    if band:
        k_ref, v_ref, kc_ref, vc_ref, o_ref, bias_ref = refs
        seq_len = v_ref.shape[1]
        assert n_q >= 3
    else:
        kc_ref, vc_ref, o_ref = refs
    qb = ATT_BLOCK
    rows = ATT_GROUP * qb
    row_id = lax.broadcasted_iota(jnp.int32, (rows, 1), 0)
    nt = (((1,), (1,)), ((), ()))

    if band:
        @pl.when(pl.program_id(0) == 0)
        def _():
            rel0 = (lax.broadcasted_iota(jnp.int32, (rows, band), 1)
                    - lax.broadcasted_iota(jnp.int32, (rows, band), 0) % qb)
            for var in range(3):
                bias_ref[var] = jnp.where(jnp.abs(rel0 - var * WINDOW) <= WINDOW, 0.0, -jnp.inf)

    for j in range(ATT_KV_HEADS):
        kc = kc_ref[j]
        vc = vc_ref[j]
        snk = jnp.zeros((rows, 1), F32)
        for g in range(ATT_GROUP):
            snk = jnp.where(row_id // qb == g, sink_ref[ATT_GROUP * j + g], snk)

        def block(i, carry, j=j, kc=kc, vc=vc, snk=snk):
            q0 = pl.multiple_of(i * qb, qb)
            qh = jnp.concatenate([q_ref[ATT_GROUP * j + g, pl.ds(q0, qb), :] for g in range(ATT_GROUP)],
                                 axis=0)
            s_ctx = jnp.dot(qh, kc, preferred_element_type=F32)
            m = jnp.maximum(jnp.max(s_ctx, axis=-1, keepdims=True), snk)
            if band:
                k0 = pl.multiple_of(jnp.clip(q0 - WINDOW, 0, seq_len - band), qb)
                var = jnp.where(i == 0, 0, jnp.where(i == n_q - 1, 2, 1))
                s_loc = jnp.dot(qh, k_ref[j, :, pl.ds(k0, band)], preferred_element_type=F32) + bias_ref[var]
                m = jnp.maximum(m, jnp.max(s_loc, axis=-1, keepdims=True))
            p_ctx = jnp.exp(s_ctx - m)
            den = jnp.sum(p_ctx, axis=-1, keepdims=True) + jnp.exp(snk - m)
            o = jnp.dot(p_ctx.astype(BF16), vc, preferred_element_type=F32)
            if band:
                p_loc = jnp.exp(s_loc - m)
                den = den + jnp.sum(p_loc, axis=-1, keepdims=True)
                o = o + jnp.dot(p_loc.astype(BF16), v_ref[j, pl.ds(k0, band), :], preferred_element_type=F32)
            o = o / den
            for g in range(ATT_GROUP):
                c0 = (ATT_GROUP * j + g) * HEAD_DIM
                o_ref[pl.ds(q0, qb), c0:c0 + HEAD_DIM] = o[g * qb:(g + 1) * qb, :]
            return carry

        lax.fori_loop(0, n_q, block, 0, unroll=4)


def _attention(sinks, q, k, v, n_batch, seq_len, ctx_len, latent):
    ctx_blk0 = n_batch * seq_len // ctx_len
    kc_spec = pl.BlockSpec((ATT_KV_HEADS, HEAD_DIM, ctx_len), lambda b: (0, 0, ctx_blk0 + b))
    vc_spec = pl.BlockSpec((ATT_KV_HEADS, ctx_len, HEAD_DIM), lambda b: (0, ctx_blk0 + b, 0))
    smem = pl.BlockSpec(memory_space=pltpu.SMEM)
    scratch = []
    if latent:
        rows = seq_len
        band = ATT_BLOCK + 2 * WINDOW
        in_specs = [smem, pl.BlockSpec((ATT_HEADS, rows, HEAD_DIM), lambda b: (0, b, 0)),
                    pl.BlockSpec((ATT_KV_HEADS, HEAD_DIM, rows), lambda b: (0, 0, b)),
                    pl.BlockSpec((ATT_KV_HEADS, rows, HEAD_DIM), lambda b: (0, b, 0)), kc_spec, vc_spec]
        args = (sinks, q, k, v, k, v)
        scratch = [pltpu.VMEM((3, ATT_GROUP * ATT_BLOCK, band), F32)]
    else:
        rows = ctx_len
        band = 0
        in_specs = [smem, pl.BlockSpec((ATT_HEADS, rows, HEAD_DIM), lambda b: (0, ctx_blk0 + b, 0)),
                    kc_spec, vc_spec]
        args = (sinks, q, k, v)
    return pl.pallas_call(
        functools.partial(_attn_kernel, n_q=rows // ATT_BLOCK, band=band),
        grid=(n_batch,),
        in_specs=in_specs,
        out_specs=pl.BlockSpec((rows, ATT_WIDTH), lambda b: (b, 0)),
        out_shape=jax.ShapeDtypeStruct((n_batch * rows, ATT_WIDTH), F32),
        scratch_shapes=scratch,
        compiler_params=_cparams(1),
        name="attn_latent" if latent else "attn_ctx",
    )(*args)


def _merge_kernel(*refs, n_lat_tiles, with_ctx):
    if with_ctx:
        (attl_ref, attc_ref, syl_ref, syc_ref, hyl_ref, hyc_ref, z_ref, x_ref, mod_ref,
         ga_ref, gs_ref, gh_ref, bd_ref, w_ref, o_ref) = refs
        is_lat = pl.program_id(0) < n_lat_tiles
        att = jnp.where(is_lat, attl_ref[...], attc_ref[...])
        sy = jnp.where(is_lat, syl_ref[...], syc_ref[...])
        hy_ctx = jnp.concatenate([hyc_ref[k] for k in range(hyc_ref.shape[0])], axis=1)
        hy_t = jnp.where(is_lat, hyl_ref[...], hy_ctx)
    else:
        attl_ref, syl_ref, hyl_ref, z_ref, x_ref, mod_ref, ga_ref, gs_ref, gh_ref, bd_ref, w_ref, o_ref = refs
        att = attl_ref[...]
        sy = syl_ref[...]
        hy_t = hyl_ref[...]
    a = att * lax.rsqrt(jnp.mean(att * att, axis=-1, keepdims=True) + EPS) * ga_ref[...]
    s = sy * _silu(z_ref[...])
    s = s * lax.rsqrt(jnp.mean(s * s, axis=-1, keepdims=True) + EPS) * gs_ref[...]
    hy = hy_t.T
    hn = hy * lax.rsqrt(_segsum(hy * hy, bd_ref[...]) * (1.0 / (HYENA_WIDTH // HYENA_GROUPS)) + EPS) * gh_ref[...]
    y = jnp.dot(a.astype(BF16), w_ref[0:ATT_WIDTH, :], preferred_element_type=F32)
    y = y + jnp.dot(s.astype(BF16), w_ref[ATT_WIDTH:ATT_WIDTH + SSD_WIDTH, :], preferred_element_type=F32)
    y = y + jnp.dot(hn.astype(BF16), w_ref[ATT_WIDTH + SSD_WIDTH:, :], preferred_element_type=F32)
    o_ref[...] = x_ref[...] + mod_ref[0, 2:3, :] * y


def _merge(att, sy, hy, z, xa, mods, ga, gs, gh, bd, w_out, n_batch, seq_len):
    d = xa.shape[1]
    tm = TOKEN_TILE
    nl = att[0].shape[0] // tm
    with_ctx = att[1] is not None
    t = att[0].shape[0] + (att[1].shape[0] if with_ctx else 0)
    row = lambda w: pl.BlockSpec((tm, w), lambda i: (i, 0))
    lat_row = lambda w: pl.BlockSpec((tm, w), lambda i: (jnp.minimum(i, nl - 1), 0))
    ctx_row = lambda w: pl.BlockSpec((tm, w), lambda i: (jnp.maximum(i - nl, 0), 0))
    const = lambda a: pl.BlockSpec(a.shape, lambda i: (0,) * a.ndim)
    tpb = seq_len // tm
    lat_tile = lambda i: jnp.minimum(i, nl - 1)
    hy_lat = pl.BlockSpec((None, HYENA_WIDTH, tm), lambda i: (lat_tile(i) // tpb, 0, lat_tile(i) % tpb))
    if with_ctx:
        ctx_len = hy[1].shape[2]
        streams = [att[0], att[1], sy[0], sy[1], hy[0], hy[1]]
        specs = [lat_row(ATT_WIDTH), ctx_row(ATT_WIDTH), lat_row(SSD_WIDTH), ctx_row(SSD_WIDTH), hy_lat,
                 pl.BlockSpec((tm // ctx_len, HYENA_WIDTH, ctx_len), lambda i: (jnp.maximum(i - nl, 0), 0, 0))]
    else:
        streams = [att[0], sy[0], hy[0]]
        specs = [row(ATT_WIDTH), row(SSD_WIDTH), hy_lat]
    return pl.pallas_call(
        functools.partial(_merge_kernel, n_lat_tiles=nl, with_ctx=with_ctx),
        grid=(t // tm,),
        in_specs=specs + [row(SSD_WIDTH), row(d),
                          pl.BlockSpec((1, 6, d), _mod_index(seq_len // tm, n_batch)),
                          const(ga), const(gs), const(gh), const(bd), const(w_out)],
        out_specs=row(d),
        out_shape=jax.ShapeDtypeStruct((t, d), F32),
        compiler_params=_cparams(1),
        name="merge_outproj",
    )(*streams, z, xa, mods, ga, gs, gh, bd, w_out)


def _swiglu_accumulate(h, wg_ref, wu_ref, wd_ref, acc_ref):
    for c in range(N_FFN_CHUNKS):
        cols = slice(c * FFN_CHUNK, (c + 1) * FFN_CHUNK)
        g = jnp.dot(h, wg_ref[:, cols], preferred_element_type=F32)
        u = jnp.dot(h, wu_ref[:, cols], preferred_element_type=F32)
        a = (_silu(g) * u).astype(BF16)
        part = jnp.dot(a, wd_ref[cols, :], preferred_element_type=F32)
        if c == 0:
            acc_ref[...] = part
        else:
            acc_ref[...] += part


def _ffn_kernel(x_ref, mod_ref, g2_ref, wg_ref, wu_ref, wd_ref, o_ref, acc_ref):
    x = x_ref[...]
    h = _modnorm(x, g2_ref[...], mod_ref[0, 4:5, :], mod_ref[0, 3:4, :]).astype(BF16)
    _swiglu_accumulate(h, wg_ref, wu_ref, wd_ref, acc_ref)
    o_ref[...] = x + mod_ref[0, 5:6, :] * acc_ref[...]


def _ffn(xa, mods, g2, wg, wu, wd, n_batch, seq_len):
    t, d = xa.shape
    tm = TOKEN_TILE
    row = pl.BlockSpec((tm, d), lambda i: (i, 0))
    resident = lambda a: pl.BlockSpec(a.shape, lambda i: (0,) * a.ndim, pipeline_mode=pl.Buffered(1))
    return pl.pallas_call(
        _ffn_kernel,
        grid=(t // tm,),
        in_specs=[row, pl.BlockSpec((1, 6, d), _mod_index(seq_len // tm, n_batch)),
                  pl.BlockSpec(g2.shape, lambda i: (0, 0)), resident(wg), resident(wu), resident(wd)],
        out_specs=row,
        out_shape=jax.ShapeDtypeStruct((t, d), F32),
        scratch_shapes=[pltpu.VMEM((tm, d), F32)],
        compiler_params=_cparams(1),
        name="ffn",
    )(xa, mods, g2, wg, wu, wd)


def _router_kernel(x_ref, mod_ref, g2_ref, r_ref, h_ref, idx_ref, wt_ref):
    h = _modnorm(x_ref[...], g2_ref[...], mod_ref[0, 4:5, :], mod_ref[0, 3:4, :])
    h_ref[...] = h
    h_hi = h.astype(BF16)
    h_lo = (h - h_hi.astype(F32)).astype(BF16)
    both = jnp.dot(h_hi, r_ref[...], preferred_element_type=F32)
    logits = (both[:, 0:LANES] + both[:, LANES:2 * LANES]
              + jnp.dot(h_lo, r_ref[:, 0:LANES], preferred_element_type=F32))
    lane = lax.broadcasted_iota(jnp.int32, logits.shape, 1)
    neg = -jnp.inf
    l1 = jnp.where(lane < N_EXPERTS, logits, neg)
    m1 = jnp.max(l1, axis=-1, keepdims=True)
    i1 = jnp.min(jnp.where(l1 == m1, lane, LANES), axis=-1, keepdims=True)
    l2 = jnp.where(lane == i1, neg, l1)
    m2 = jnp.max(l2, axis=-1, keepdims=True)
    i2 = jnp.min(jnp.where(l2 == m2, lane, LANES), axis=-1, keepdims=True)
    e = jnp.exp(m2 - m1)
    w1 = 1.0 / (1.0 + e)
    w2 = e / (1.0 + e)
    idx_ref[...] = jnp.where(lane == 0, i1, jnp.where(lane == 1, i2, 0))
    wt_ref[...] = jnp.where(lane == 0, w1, jnp.where(lane == 1, w2, 0.0))


def _router(xa, mods, g2, r_pad, n_rows, n_batch, seq_len):
    d = xa.shape[1]
    tm = TOKEN_TILE
    row = lambda w: pl.BlockSpec((tm, w), lambda i: (i, 0))
    return pl.pallas_call(
        _router_kernel,
        grid=(n_rows // tm,),
        in_specs=[row(d), pl.BlockSpec((1, 6, d), _mod_index(seq_len // tm, n_batch)),
                  pl.BlockSpec(g2.shape, lambda i: (0, 0)), pl.BlockSpec(r_pad.shape, lambda i: (0, 0))],
        out_specs=[row(d), row(LANES), row(LANES)],
        out_shape=[jax.ShapeDtypeStruct((n_rows, d), F32),
                   jax.ShapeDtypeStruct((n_rows, LANES), jnp.int32),
                   jax.ShapeDtypeStruct((n_rows, LANES), F32)],
        compiler_params=_cparams(1),
        name="moe_router",
    )(xa, mods, g2, r_pad)


def _row_copy(src, src_row, dst, dst_row, sem):
    return pltpu.make_async_copy(src.at[pl.ds(src_row, 1), :], dst.at[pl.ds(dst_row, 1), :], sem)


DMA_ISSUE_UNROLL = 8


def _idx_copy(dest_hbm, dest_smem, sem_idx, tile, slot):
    n = dest_hbm.shape[1]
    half = dest_smem.at[pl.ds(pl.multiple_of(slot * n, n), n)]
    return pltpu.make_async_copy(dest_hbm.at[tile], half, sem_idx.at[slot])


def _dispatch_kernel(pad_tile_ref, dest_hbm, h_ref, xs_out, dest_smem, zeros, sem_idx, sem_rows, sem_zero):
    i = pl.program_id(0)
    n = pl.num_programs(0)
    tm = h_ref.shape[0]
    slot = i % 2

    @pl.when(i == 0)
    def _():
        zeros[...] = jnp.zeros_like(zeros)

        def zero_copy(e):
            return pltpu.make_async_copy(zeros, xs_out.at[pl.ds(pl.multiple_of(pad_tile_ref[e], tm), tm), :],
                                         sem_zero)

        for e in range(2 * N_EXPERTS):
            @pl.when(pad_tile_ref[e] >= 0)
            def _(e=e):
                zero_copy(e).start()
        for e in range(2 * N_EXPERTS):
            @pl.when(pad_tile_ref[e] >= 0)
            def _(e=e):
                zero_copy(e).wait()
        _idx_copy(dest_hbm, dest_smem, sem_idx, 0, 0).start()

    @pl.when(i + 1 < n)
    def _():
        _idx_copy(dest_hbm, dest_smem, sem_idx, i + 1, 1 - slot).start()

    _idx_copy(dest_hbm, dest_smem, sem_idx, i, slot).wait()

    base = slot * (2 * tm)

    def issue(r, carry):
        _row_copy(h_ref, r, xs_out, dest_smem[base + 2 * r], sem_rows).start()
        _row_copy(h_ref, r, xs_out, dest_smem[base + 2 * r + 1], sem_rows).start()
        return carry

    lax.fori_loop(0, tm, issue, 0, unroll=DMA_ISSUE_UNROLL)
    for _ in range(2):
        pltpu.make_async_copy(h_ref, xs_out.at[pl.ds(0, tm), :], sem_rows).wait()


def _dispatch(pad_tile, dest, h, n_slots):
    n_rows, d = h.shape
    tm = TOKEN_TILE
    assert tm == EXPERT_TILE
    return pl.pallas_call(
        _dispatch_kernel,
        grid_spec=pltpu.PrefetchScalarGridSpec(
            num_scalar_prefetch=1,
            grid=(n_rows // tm,),
            in_specs=[pl.BlockSpec(memory_space=pl.ANY), pl.BlockSpec((tm, d), lambda i, pt: (i, 0))],
            out_specs=pl.BlockSpec(memory_space=pl.ANY),
            scratch_shapes=[pltpu.SMEM((4 * tm,), jnp.int32), pltpu.VMEM((tm, d), F32),
                            pltpu.SemaphoreType.DMA((2,)), pltpu.SemaphoreType.DMA(()),
                            pltpu.SemaphoreType.DMA(())]),
        out_shape=jax.ShapeDtypeStruct((n_slots, d), F32),
        compiler_params=_cparams(1),
        name="moe_dispatch",
    )(pad_tile, dest.reshape(n_rows // tm, 2 * tm), h)


def _expert_kernel(te_ref, nused_ref, xs_ref, wg_ref, wu_ref, wd_ref, o_ref, acc_ref):
    del te_ref
    live = pl.program_id(0) < nused_ref[0]

    @pl.when(live)
    def _():
        _swiglu_accumulate(xs_ref[...].astype(BF16), wg_ref, wu_ref, wd_ref, acc_ref)
        o_ref[...] = acc_ref[...]

    @pl.when(jnp.logical_not(live))
    def _():
        o_ref[...] = jnp.zeros_like(o_ref)


def _experts(tile_expert, n_used, xs, wg, wu, wd):
    s, d = xs.shape
    tm = EXPERT_TILE
    row = pl.BlockSpec((tm, d), lambda i, te, nu: (i, 0))
    xs_row = pl.BlockSpec((tm, d), lambda i, te, nu: (jnp.minimum(i, nu[0] - 1), 0))
    wspec = lambda a: pl.BlockSpec((None,) + a.shape[1:], lambda i, te, nu: (te[i], 0, 0))
    return pl.pallas_call(
        _expert_kernel,
        grid_spec=pltpu.PrefetchScalarGridSpec(
            num_scalar_prefetch=2,
            grid=(s // tm,),
            in_specs=[xs_row, wspec(wg), wspec(wu), wspec(wd)],
            out_specs=row,
            scratch_shapes=[pltpu.VMEM((tm, d), F32)]),
        out_shape=jax.ShapeDtypeStruct((s, d), F32),
        compiler_params=_cparams(1),
        name="moe_experts",
    )(tile_expert, n_used, xs, wg, wu, wd)


def _combine_kernel(dest_hbm, eo_hbm, x_ref, wt_ref, mod_ref, o_ref, dest_smem, buf, sem_idx, sem_rows):
    i = pl.program_id(0)
    n = pl.num_programs(0)
    tm = x_ref.shape[0]
    slot = i % 2

    def gather(s):
        base = s * (2 * tm)

        def issue(r, carry):
            _row_copy(eo_hbm, dest_smem[base + 2 * r], buf.at[s, 0], r, sem_rows.at[s]).start()
            _row_copy(eo_hbm, dest_smem[base + 2 * r + 1], buf.at[s, 1], r, sem_rows.at[s]).start()
            return carry

        lax.fori_loop(0, tm, issue, 0, unroll=DMA_ISSUE_UNROLL)

    @pl.when(i == 0)
    def _():
        first = _idx_copy(dest_hbm, dest_smem, sem_idx, 0, 0)
        first.start()
        first.wait()
        gather(0)

        @pl.when(n > 1)
        def _():
            _idx_copy(dest_hbm, dest_smem, sem_idx, 1, 1).start()

    @pl.when(i + 1 < n)
    def _():
        _idx_copy(dest_hbm, dest_smem, sem_idx, i + 1, 1 - slot).wait()

        @pl.when(i + 2 < n)
        def _():
            _idx_copy(dest_hbm, dest_smem, sem_idx, i + 2, slot).start()

        gather(1 - slot)

    for k in range(2):
        pltpu.make_async_copy(eo_hbm.at[pl.ds(0, tm), :], buf.at[slot, k], sem_rows.at[slot]).wait()
    wt = wt_ref[...]
    y = wt[:, 0:1] * buf[slot, 0] + wt[:, 1:2] * buf[slot, 1]
    o_ref[...] = x_ref[...] + mod_ref[0, 5:6, :] * y


def _combine(dest, eo, xa, wts, mods, n_rows, n_batch, seq_len):
    d = xa.shape[1]
    tm = TOKEN_TILE
    row = lambda w: pl.BlockSpec((tm, w), lambda i: (i, 0))
    return pl.pallas_call(
        _combine_kernel,
        grid=(n_rows // tm,),
        in_specs=[pl.BlockSpec(memory_space=pl.ANY), pl.BlockSpec(memory_space=pl.ANY), row(d), row(LANES),
                  pl.BlockSpec((1, 6, d), _mod_index(seq_len // tm, n_batch))],
        out_specs=row(d),
        out_shape=jax.ShapeDtypeStruct((n_rows, d), F32),
        scratch_shapes=[pltpu.SMEM((4 * tm,), jnp.int32), pltpu.VMEM((2, 2, tm, d), F32),
                        pltpu.SemaphoreType.DMA((2,)), pltpu.SemaphoreType.DMA((2,))],
        compiler_params=_cparams(1),
        name="moe_combine",
    )(dest.reshape(n_rows // tm, 2 * tm), eo, xa, wts, mods)


def _moe(xa, mods, g2, r_pad, wg, wu, wd, n_rows, n_batch, seq_len):
    h, idx, wts = _router(xa, mods, g2, r_pad, n_rows, n_batch, seq_len)
    tm = EXPERT_TILE
    e_flat = idx[:, :2].reshape(-1)
    onehot = (e_flat[:, None] == jnp.arange(N_EXPERTS, dtype=jnp.int32)[None, :]).astype(jnp.int32)
    csum = jnp.cumsum(onehot, axis=0)
    counts = csum[-1]
    rank = jnp.sum(onehot * csum, axis=1) - 1
    padded = ((counts + tm - 1) // tm) * tm
    ends = jnp.cumsum(padded)
    starts = ends - padded
    dest = (jnp.sum(onehot * starts[None, :], axis=1) + rank).astype(jnp.int32)
    n_slots = 2 * n_rows + N_EXPERTS * tm
    tile_start = jnp.arange(n_slots // tm, dtype=jnp.int32) * tm
    tile_expert = jnp.minimum(jnp.sum((tile_start[:, None] >= ends[None, :]).astype(jnp.int32), axis=1),
                              N_EXPERTS - 1).astype(jnp.int32)
    n_used = (ends[-1:] // tm).astype(jnp.int32)
    tail = ends[-1] + jnp.arange(N_EXPERTS, dtype=ends.dtype) * tm
    pad_tile = jnp.concatenate([jnp.where(padded > 0, ends - tm, -1),
                                jnp.where(tail < n_slots, tail, -1)]).astype(jnp.int32)
    xs = _dispatch(pad_tile, dest, h, n_slots)
    eo = _experts(tile_expert, n_used, xs, wg, wu, wd)
    return _combine(dest, eo, xa, wts, mods, n_rows, n_batch, seq_len)


def _softplus(v):
    return jnp.maximum(v, 0.0) + jnp.log1p(jnp.exp(-jnp.abs(v)))


def _ssd_kernel(xl_ref, dl_ref, xc_ref, dc_ref, cw_ref, cb_ref, dtb_ref, alogc_ref, dsk_ref, *rest, want_ctx):
    if want_ctx:
        yl_ref, yc_ref, *rest = rest
    else:
        yl_ref, *rest = rest
        yc_ref = None
    *lat_s, st_ref = rest
    lat_s, ctx_s = lat_s[:6], lat_s[6:]
    ck = SSD_CHUNK
    hp = SSD_HEAD_DIM
    ns = SSD_STATE
    n_col = 2 * SSD_HEADS
    hi = lax.Precision.HIGHEST
    row = lax.broadcasted_iota(jnp.int32, (ck, 1), 0)
    li = lax.broadcasted_iota(jnp.int32, (ck, ck), 0)
    si = lax.broadcasted_iota(jnp.int32, (ck, ck), 1)
    masks = (si <= li, si >= li)
    tris = (masks[0].astype(F32), masks[1].astype(F32))
    a_col = -jnp.exp(alogc_ref[...])
    dskip = dsk_ref[...]

    def prep(raw_ref, dtraw_ref, xs_s, dtt_s, lar_s, lac_s, bt_s, sc_s, y_ref):
        n = raw_ref.shape[0]
        nk = n // ck

        def body(k, carry):
            r0 = pl.multiple_of(k * ck, ck)
            a = raw_ref[pl.ds(r0, ck), :]
            top = raw_ref[pl.ds(pl.multiple_of(jnp.maximum(r0 - 8, 0), 8), 8), :][7:8, :]
            bot = raw_ref[pl.ds(pl.multiple_of(jnp.minimum(r0 + ck, n - 8), 8), 8), :][0:1, :]
            top = jnp.where(k > 0, top, 0.0)
            bot = jnp.where(k < nk - 1, bot, 0.0)
            prev = jnp.where(row == 0, top, pltpu.roll(a, 1, 0))
            nxt = jnp.where(row == ck - 1, bot, pltpu.roll(a, ck - 1, 0))
            xs = _silu(prev * cw_ref[0:1, :] + a * cw_ref[1:2, :] + nxt * cw_ref[2:3, :] + cb_ref[...])
            xs_s[pl.ds(r0, ck), :] = xs
            dt = _softplus(dtraw_ref[pl.ds(r0, ck), :] + dtb_ref[...])
            dt_t = dt.T[0:n_col, :]
            dtt_s[:, pl.ds(r0, ck)] = dt_t
            for dr in range(2):
                la_r = jnp.dot(dt_t * a_col, tris[1 - dr], precision=hi, preferred_element_type=F32)
                lar_s[dr, :, pl.ds(r0, ck)] = la_r
                if y_ref is not None:
                    lac_s[dr, pl.ds(r0, ck), :] = jnp.concatenate(
                        [la_r, jnp.zeros((ck - n_col, ck), F32)], axis=0).T
            for g in range(SSD_GROUPS):
                b_t = xs[:, SSD_WIDTH + g * ns:SSD_WIDTH + (g + 1) * ns].T
                bt_s[g, :, pl.ds(r0, ck)] = b_t
                if y_ref is not None:
                    c0 = SSD_WIDTH + SSD_GROUPS * ns + g * ns
                    sc_s[g, pl.ds(r0, ck), :] = jnp.dot(xs[:, c0:c0 + ns].astype(BF16), b_t.astype(BF16),
                                                        preferred_element_type=F32)
            if y_ref is not None:
                y_ref[pl.ds(r0, ck), :] = xs[:, 0:SSD_WIDTH] * dskip
            return carry

        lax.fori_loop(0, nk, body, 0, unroll=4)

    def run(xs_s, dtt_s, lar_s, lac_s, bt_s, sc_s, y_ref):
        nk = xs_s.shape[0] // ck

        def one(kk, dr):
            r0 = pl.multiple_of(kk * ck, ck)
            xc = xs_s[pl.ds(r0, ck), :]
            dtr = dtt_s[:, pl.ds(r0, ck)]
            la_r = lar_s[dr, :, pl.ds(r0, ck)]
            la_end = la_r[:, ck - 1:ck] if dr == 0 else la_r[:, 0:1]
            if y_ref is not None:
                la_c = lac_s[dr, pl.ds(r0, ck), :]
            ys = []
            for g in range(SSD_GROUPS):
                b_t = bt_s[g, :, pl.ds(r0, ck)]
                c0 = SSD_WIDTH + SSD_GROUPS * ns + g * ns
                c_g = xc[:, c0:c0 + ns].astype(BF16)
                if y_ref is not None:
                    scores = sc_s[g, pl.ds(r0, ck), :]
                for hh in range(SSD_HEADS // SSD_GROUPS):
                    h = g * (SSD_HEADS // SSD_GROUPS) + hh
                    col = dr * SSD_HEADS + h
                    xh = xc[:, h * hp:(h + 1) * hp].astype(BF16)
                    dt_row = dtr[col:col + 1, :]
                    le = la_end[col:col + 1, :]
                    st = st_ref[col]
                    if y_ref is not None:
                        la_col = la_c[:, col:col + 1]
                        decay = jnp.exp(jnp.where(masks[dr], la_col - la_r[col:col + 1, :], -jnp.inf))
                        y = jnp.dot((scores * decay * dt_row).astype(BF16), xh, preferred_element_type=F32)
                        y = y + jnp.dot(c_g, st.astype(BF16), preferred_element_type=F32) * jnp.exp(la_col)
                        ys.append(y)
                    bw = (b_t * (dt_row * jnp.exp(le - la_r[col:col + 1, :]))).astype(BF16)
                    st_ref[col] = st * jnp.exp(le) + jnp.dot(bw, xh, preferred_element_type=F32)
            if y_ref is not None:
                y_ref[pl.ds(r0, ck), :] += jnp.concatenate(ys, axis=1)

        def body(k, carry):
            one(k, 0)
            one(nk - 1 - k, 1)
            return carry

        lax.fori_loop(0, nk, body, 0, unroll=2)

    prep(xl_ref, dl_ref, *lat_s, yl_ref)
    prep(xc_ref, dc_ref, *ctx_s, yc_ref)
    st_ref[...] = jnp.zeros_like(st_ref)
    run(*ctx_s, yc_ref)
    run(*lat_s, yl_ref)


def _ssd(xbc, dtp, conv_w, conv_b, dt_bias, a_log, d_skip, n_batch, seq_len, ctx_len, want_ctx):
    ctx0 = n_batch * seq_len // ctx_len
    pad = lambda v: jnp.pad(v.reshape(1, -1), ((0, 0), (0, LANES - v.size)))
    lat = lambda w: pl.BlockSpec((seq_len, w), lambda b: (b, 0))
    ctx = lambda w: pl.BlockSpec((ctx_len, w), lambda b: (ctx0 + b, 0))
    const = lambda a: pl.BlockSpec(a.shape, lambda b: (0,) * a.ndim)
    consts = (conv_w, conv_b.reshape(1, -1), pad(dt_bias), a_log.reshape(-1, 1),
              jnp.repeat(d_skip, SSD_HEAD_DIM).reshape(1, -1))
    stream_scratch = lambda n: [pltpu.VMEM((n, SSD_XBC_COLS), F32), pltpu.VMEM((2 * SSD_HEADS, n), F32),
                                pltpu.VMEM((2, 2 * SSD_HEADS, n), F32), pltpu.VMEM((2, n, SSD_CHUNK), F32),
                                pltpu.VMEM((SSD_GROUPS, SSD_STATE, n), F32),
                                pltpu.VMEM((SSD_GROUPS, n, SSD_CHUNK), F32)]
    out_specs = [lat(SSD_WIDTH)]
    out_shape = [jax.ShapeDtypeStruct((n_batch * seq_len, SSD_WIDTH), F32)]
    if want_ctx:
        out_specs.append(pl.BlockSpec((ctx_len, SSD_WIDTH), lambda b: (b, 0)))
        out_shape.append(jax.ShapeDtypeStruct((n_batch * ctx_len, SSD_WIDTH), F32))
    return pl.pallas_call(
        functools.partial(_ssd_kernel, want_ctx=want_ctx),
        grid=(n_batch,),
        in_specs=[lat(SSD_XBC_COLS), lat(LANES), ctx(SSD_XBC_COLS), ctx(LANES)] + [const(a) for a in consts],
        out_specs=out_specs,
        out_shape=out_shape,
        scratch_shapes=[*stream_scratch(seq_len), *stream_scratch(ctx_len),
                        pltpu.VMEM((2 * SSD_HEADS, SSD_STATE, SSD_HEAD_DIM), F32)],
        compiler_params=_cparams(1),
        name="ssd",
    )(xbc, dtp, xbc, dtp, *consts)


HY_BLOCK = 256
HY_CH_STEP = 8


def _hyena_tables(seq_len):
    f32 = np.float32
    nj = 2 * seq_len
    lag = np.arange(nj, dtype=np.int32) - seq_len
    dist = np.abs(lag)
    pos = np.minimum(dist, seq_len - 1)
    t = np.linspace(0.0, 1.0, seq_len, dtype=f32)[pos]
    w = (f32(2.0 * math.pi / seq_len) * np.arange(seq_len, dtype=f32))[pos]
    bands = (HYENA_POS_DIM - 1) // 2
    freqs = np.linspace(1e-4, bands - 1, bands, dtype=f32)[None, :]
    ang = (freqs * w[:, None]).astype(f32)
    z = np.concatenate([t[:, None], np.cos(ang), -np.sin(ang)], axis=-1).astype(f32)
    zt = np.pad(z.T, ((0, (-HYENA_POS_DIM) % 8), (0, 0)))
    deltas = np.abs(np.linspace(math.log(HYENA_DECAY_TARGET) / HYENA_SLOW_DECAY,
                                math.log(HYENA_DECAY_TARGET) / HYENA_FAST_DECAY, HYENA_WIDTH, dtype=f32))
    dec = (np.exp(-t[None, :] * deltas[:, None]) * (dist < seq_len)[None, :]).astype(f32)
    fwd = (lag >= 0).astype(f32)[None, :]
    return jnp.asarray(zt), jnp.asarray(dec), jnp.asarray(fwd)


def _hyfilt_kernel(zt_ref, dec_ref, fwd_ref, w1_ref, b1_ref, f1_ref, w2_ref, b2_ref, f2_ref, w3_ref, o_ref):
    hi = lax.Precision.HIGHEST
    h = jnp.sin(f1_ref[...] * (jnp.dot(w1_ref[...], zt_ref[...], precision=hi, preferred_element_type=F32)
                               + b1_ref[...]))
    h = jnp.sin(f2_ref[...] * (jnp.dot(w2_ref[...], h, precision=hi, preferred_element_type=F32) + b2_ref[...]))
    hw = jnp.dot(w3_ref[...], h, precision=hi, preferred_element_type=F32)
    fwd = fwd_ref[...] > 0.5
    dec = dec_ref[...]
    nw = HYENA_WIDTH
    for o in range(HYENA_ORDER):
        o_ref[o] = jnp.where(fwd, hw[o * nw:(o + 1) * nw], hw[(HYENA_ORDER + o) * nw:(HYENA_ORDER + o + 1) * nw]) * dec


def _hyena_filters(tables, w1, b1, f1, w2, b2, f2, w3):
    zt, dec, fwd = tables
    nj = zt.shape[1]
    tj = 512
    col = lambda v: v.reshape(-1, 1)
    w1t = jnp.pad(w1.T, ((0, 0), (0, zt.shape[0] - w1.shape[0])))
    consts = (w1t, col(b1), col(f1), w2.T, col(b2), col(f2), w3.T)
    lanes = lambda a: pl.BlockSpec((a.shape[0], tj), lambda j: (0, j))
    const = lambda a: pl.BlockSpec(a.shape, lambda j: (0, 0))
    return pl.pallas_call(
        _hyfilt_kernel,
        grid=(nj // tj,),
        in_specs=[lanes(zt), lanes(dec), lanes(fwd)] + [const(a) for a in consts],
        out_specs=pl.BlockSpec((HYENA_ORDER, HYENA_WIDTH, tj), lambda j: (0, 0, j)),
        out_shape=jax.ShapeDtypeStruct((HYENA_ORDER, HYENA_WIDTH, nj), F32),
        compiler_params=_cparams(1),
        name="hyena_filters",
    )(zt, dec, fwd, *consts)


def _hyconv_kernel(cw_ref, cb_ref, hb_ref, v_ref, x1_ref, x2_ref, kf_ref, o_ref):
    n_b, n_ch, seq_len = v_ref.shape
    nb = seq_len // HY_BLOCK
    blk = HY_BLOCK
    c_base = pl.program_id(0) * n_ch
    lane = lax.broadcasted_iota(jnp.int32, (1, seq_len), 1)

    def sconv(x, ch):
        prev = jnp.where(lane == 0, 0.0, pltpu.roll(x, 1, 1))
        nxt = jnp.where(lane == seq_len - 1, 0.0, pltpu.roll(x, seq_len - 1, 1))
        return prev * cw_ref[0, ch] + x * cw_ref[1, ch] + nxt * cw_ref[2, ch] + cb_ref[ch]

    def long_conv(vals, kf_row):
        half = blk // 2
        skew = pltpu.roll(jnp.broadcast_to(kf_row, (half, 2 * seq_len)), 0, 1, stride=1,
                          stride_axis=0).astype(BF16)
        vb = vals.astype(BF16)
        acc = [None] * nb
        for d in range(-(nb - 1), nb):
            a0 = seq_len + d * blk
            tt = jnp.concatenate([skew[:, a0:a0 + blk], skew[:, a0 - half:a0 - half + blk]], axis=0)
            sis = list(range(max(0, -d), min(nb, nb - d)))
            lhs = [vb[:, s * blk:(s + 1) * blk] for s in sis]
            lhs = lhs[0] if len(lhs) == 1 else jnp.concatenate(lhs, axis=0)
            out = jnp.dot(lhs, tt, preferred_element_type=F32)
            for idx, s in enumerate(sis):
                piece = out[idx * n_b:(idx + 1) * n_b]
                acc[s + d] = piece if acc[s + d] is None else acc[s + d] + piece
        return acc[0] if nb == 1 else jnp.concatenate(acc, axis=1)

    def channel(cc, carry):
        ch = c_base + cc
        v = sconv(v_ref[:, cc, :], ch)
        x1 = sconv(x1_ref[:, cc, :], HYENA_WIDTH + ch)
        x2 = sconv(x2_ref[:, cc, :], 2 * HYENA_WIDTH + ch)
        z = x1 * (long_conv(v, kf_ref[0, pl.ds(cc, 1), :]) + v * hb_ref[0, ch])
        o_ref[:, cc, :] = x2 * (long_conv(z, kf_ref[1, pl.ds(cc, 1), :]) + z * hb_ref[1, ch])
        return carry

    lax.fori_loop(0, n_ch, channel, 0, unroll=2)


def _hyena_conv(hy, kf, conv_w, conv_b, hy_bias):
    n_batch, _, seq_len = hy.shape
    cs = HY_CH_STEP
    nw = HYENA_WIDTH
    stream = lambda k: pl.BlockSpec((n_batch, cs, seq_len), lambda c: (0, k * (nw // cs) + c, 0))
    smem = pl.BlockSpec(memory_space=pltpu.SMEM)
    return pl.pallas_call(
        _hyconv_kernel,
        grid=(nw // cs,),
        in_specs=[smem, smem, smem, stream(0), stream(1), stream(2),
                  pl.BlockSpec((HYENA_ORDER, cs, 2 * seq_len), lambda c: (0, c, 0))],
        out_specs=pl.BlockSpec((n_batch, cs, seq_len), lambda c: (0, c, 0)),
        out_shape=jax.ShapeDtypeStruct((n_batch, nw, seq_len), F32),
        compiler_params=_cparams(1),
        name="hyena_conv",
    )(conv_w, conv_b, hy_bias, hy, hy, hy, kf)


def _rope_tables(seq_len, extra):
    rows = seq_len // GRID_W
    row = jnp.repeat(jnp.arange(rows, dtype=F32), GRID_W)
    col = jnp.tile(jnp.arange(GRID_W, dtype=F32), rows)
    inv = ROPE_THETA ** (-jnp.arange(0, ROPE_AXIS_DIM, 2, dtype=F32) / ROPE_AXIS_DIM)
    ang = jnp.stack([row[:, None] * inv, col[:, None] * inv], axis=1)
    cos = jnp.cos(ang)
    sin = jnp.sin(ang)
    cos_h = jnp.concatenate([cos, cos], axis=-1).reshape(seq_len, HEAD_DIM)
    sin_h = jnp.concatenate([-sin, sin], axis=-1).reshape(seq_len, HEAD_DIM)
    cos_t = jnp.concatenate([jnp.tile(cos_h, (1, LANES // HEAD_DIM)), jnp.ones((extra, LANES), F32)], axis=0)
    sin_t = jnp.concatenate([jnp.tile(sin_h, (1, LANES // HEAD_DIM)), jnp.zeros((extra, LANES), F32)], axis=0)
    return cos_t, sin_t


def _block_diag_ones(n, seg):
    i = jnp.arange(n) // seg
    return (i[:, None] == i[None, :]).astype(BF16)


def kernel(x, c, ctx, c_ctx, w_ada, b_ada, norm1, norm2, w_in, w_out, q_norm, k_norm, att_sinks, att_out_norm, ssd_conv_w, ssd_conv_b, ssd_dt_bias, ssd_a_log, ssd_d, ssd_norm, hy_conv_w, hy_conv_b, hy_w1, hy_b1, hy_f1, hy_w2, hy_b2, hy_f2, hy_w3, hy_bias, hy_out_norm, ffn_w_gate, ffn_w_up, ffn_w_down, moe_router, moe_w_gate, moe_w_up, moe_w_down):
    n_batch, seq_len, d = x.shape
    ctx_len = ctx.shape[1]
    n_lat = n_batch * seq_len
    n_ctx = n_batch * ctx_len
    depth = w_in.shape[0]
    xa = jnp.concatenate([x.reshape(n_lat, d), ctx.reshape(n_ctx, d)], axis=0)

    cc = jnp.concatenate([c, c_ctx[None, :]], axis=0)
    pad_rows = (-cc.shape[0]) % 8
    cc = jnp.pad(cc, ((0, pad_rows), (0, 0)))
    mods_all = _adaln(cc, w_ada, b_ada)[:, :n_batch + 1].reshape(depth, n_batch + 1, 6, d)

    cos_t, sin_t = _rope_tables(seq_len, INPROJ_TILE)
    bd_q = _block_diag_ones(Q_COLS, HEAD_DIM)
    bd_h = _block_diag_ones(HYENA_WIDTH, HYENA_WIDTH // HYENA_GROUPS)
    hy_tab_l = _hyena_tables(seq_len)
    hy_tab_c = _hyena_tables(ctx_len)

    for i in range(depth):
        last = i == depth - 1
        j = i // 2
        mods = mods_all[i]
        wi = w_in[i]
        c_dt = QKV_W + ZX_W
        w_cat = jnp.concatenate([wi[:, :c_dt], wi[:, c_dt:c_dt + SSD_DT_COLS],
                                 jnp.zeros((d, LANES - SSD_DT_COLS), F32)], axis=1).astype(BF16)
        w_hy_t = wi[:, c_dt + SSD_DT_COLS:].T.astype(BF16)
        qg = jnp.tile(q_norm[i], Q_COLS // HEAD_DIM)[None, :]
        kg = jnp.tile(k_norm[i], KV_COLS // HEAD_DIM)[None, :]
        q, k, v, z, xbc, hy_l, hy_c, dtp = _inproj(xa, mods, norm1[i][None, :], w_cat, w_hy_t, cos_t, sin_t,
                                                   qg, kg, bd_q, n_batch, seq_len, ctx_len)

        att_l = _attention(att_sinks[i], q, k, v, n_batch, seq_len, ctx_len, True)
        ssd_out = _ssd(xbc, dtp, ssd_conv_w[i], ssd_conv_b[i], ssd_dt_bias[i], ssd_a_log[i], ssd_d[i],
                       n_batch, seq_len, ctx_len, not last)
        filt = (hy_w1[i], hy_b1[i], hy_f1[i], hy_w2[i], hy_b2[i], hy_f2[i], hy_w3[i])
        hyo_l = _hyena_conv(hy_l, _hyena_filters(hy_tab_l, *filt), hy_conv_w[i], hy_conv_b[i], hy_bias[i])
        if last:
            att = (att_l, None)
            sy = (ssd_out[0], None)
            hyo = (hyo_l, None)
            n_rows = n_lat
        else:
            att = (att_l, _attention(att_sinks[i], q, k, v, n_batch, seq_len, ctx_len, False))
            sy = tuple(ssd_out)
            hyo_c = _hyena_conv(hy_c, _hyena_filters(hy_tab_c, *filt), hy_conv_w[i], hy_conv_b[i], hy_bias[i])
            hyo = (hyo_l, hyo_c)
            n_rows = n_lat + n_ctx
        xa = _merge(att, sy, hyo, z, xa, mods, att_out_norm[i][None, :], ssd_norm[i][None, :],
                    hy_out_norm[i][None, :], bd_h, w_out[i].astype(BF16), n_batch, seq_len)

        g2 = norm2[i][None, :]
        if i % 2 == 0:
            xa = _ffn(xa, mods, g2, ffn_w_gate[j].astype(BF16), ffn_w_up[j].astype(BF16),
                      ffn_w_down[j].astype(BF16), n_batch, seq_len)
        else:
            r_full = jnp.pad(moe_router[j], ((0, 0), (0, LANES - N_EXPERTS)))
            r_hi = r_full.astype(BF16)
            r_pad = jnp.concatenate([r_hi, (r_full - r_hi.astype(F32)).astype(BF16)], axis=1)
            xa = _moe(xa, mods, g2, r_pad, moe_w_gate[j].astype(BF16), moe_w_up[j].astype(BF16),
                      moe_w_down[j].astype(BF16), n_rows, n_batch, seq_len)
    return xa[:n_lat].reshape(n_batch, seq_len, d)
```

```python
import functools
import math

import jax
import jax.numpy as jnp
import numpy as np
from jax import lax
from jax.experimental import pallas as pl
from jax.experimental.pallas import tpu as pltpu

F32 = jnp.float32
BF16 = jnp.bfloat16

D_MODEL = 1024
DEPTH = 4
GRID_W = 64
EPS = 1e-6
HEAD_DIM = 64
ATT_WIDTH = 512
ATT_HEADS = 8
ATT_KV_HEADS = 2
ATT_GROUP = 4
WINDOW = 128
ATT_BLOCK = 128
ROPE_THETA = 10000.0
ROPE_AXIS_DIM = 32
SSD_WIDTH = 256
SSD_HEAD_DIM = 64
SSD_HEADS = 4
SSD_STATE = 64
SSD_GROUPS = 2
SSD_CHUNK = 128
HYENA_WIDTH = 256
HYENA_GROUPS = 4
HYENA_ORDER = 2
HYENA_POS_DIM = 33
HYENA_FAST_DECAY = 0.3
HYENA_SLOW_DECAY = 1.5
HYENA_DECAY_TARGET = 1e-2
Q_COLS = 512
KV_COLS = 128
SSD_XBC_COLS = 512
SSD_DT_COLS = 8
HY_COLS = 768
FFN_DIM = 2816
N_EXPERTS = 8
FFN_CHUNK = 256
N_FFN_CHUNKS = FFN_DIM // FFN_CHUNK
LANES = 128
QKV_W = Q_COLS + 2 * KV_COLS
ZX_W = SSD_WIDTH + SSD_XBC_COLS
PROJ_PAD = QKV_W + ZX_W + LANES
VMEM_LIMIT = 56 * 1024 * 1024
TOKEN_TILE = 512
INPROJ_TILE = 1024
EXPERT_TILE = 512


def _cparams(n_axes):
    return pltpu.CompilerParams(dimension_semantics=("arbitrary",) * n_axes,
                                vmem_limit_bytes=VMEM_LIMIT)


def _silu(v):
    return v / (1.0 + jnp.exp(-v))


def _modnorm(x, g, scale, shift):
    ms = jnp.mean(x * x, axis=-1, keepdims=True)
    return x * lax.rsqrt(ms + EPS) * g * (1.0 + scale) + shift


def _segsum(t, bd):
    hi = t.astype(BF16)
    lo = (t - hi.astype(F32)).astype(BF16)
    return (jnp.dot(hi, bd, preferred_element_type=F32)
            + jnp.dot(lo, bd, preferred_element_type=F32))


def _mod_index(tiles_per_batch, n_batch):
    return lambda i: (jnp.minimum(i // tiles_per_batch, n_batch), 0, 0)


def _adaln_kernel(c_ref, w_ref, b_ref, o_ref):
    s = _silu(c_ref[...]).astype(BF16)
    o_ref[...] = jnp.dot(s, w_ref[...].astype(BF16), preferred_element_type=F32) + b_ref[...]


def _adaln(cc, w_ada, b_ada):
    depth, d, n = w_ada.shape
    r = cc.shape[0]
    tn = 512
    return pl.pallas_call(
        _adaln_kernel,
        grid=(depth, n // tn),
        in_specs=[pl.BlockSpec((r, d), lambda l, j: (0, 0)),
                  pl.BlockSpec((None, d, tn), lambda l, j: (l, 0, j)),
                  pl.BlockSpec((None, 1, tn), lambda l, j: (l, 0, j))],
        out_specs=pl.BlockSpec((None, r, tn), lambda l, j: (l, 0, j)),
        out_shape=jax.ShapeDtypeStruct((depth, r, n), F32),
        compiler_params=_cparams(2),
        name="adaln",
    )(cc, w_ada, b_ada.reshape(depth, 1, n))


def _inproj_kernel(x_ref, mod_ref, g1_ref, w_ref, wh_ref, cos_ref, sin_ref, qg_ref, kg_ref, bd_ref,
                   q_ref, k_ref, v_ref, z_ref, xbc_ref, hyl_ref, hyc_ref, dt_ref, *, n_lat_tiles):
    x = x_ref[...]
    h = _modnorm(x, g1_ref[...], mod_ref[0, 1:2, :], mod_ref[0, 0:1, :]).astype(BF16)
    pq = jnp.dot(h, w_ref[:, 0:QKV_W], preferred_element_type=F32)
    cos = cos_ref[...]
    sin = sin_ref[...]
    lane = lax.broadcasted_iota(jnp.int32, (1, LANES), 1)
    first_half = (lane % 32) < 16

    def rope(t):
        partner = jnp.where(first_half, pltpu.roll(t, LANES - 16, 1), pltpu.roll(t, 16, 1))
        return t * cos + partner * sin

    q = pq[:, 0:Q_COLS]
    qn = q * lax.rsqrt(_segsum(q * q, bd_ref[...]) * (1.0 / HEAD_DIM) + EPS) * qg_ref[...]
    scale = HEAD_DIM ** -0.5
    for j in range(Q_COLS // LANES):
        pair = (rope(qn[:, LANES * j:LANES * (j + 1)]) * scale).astype(BF16)
        q_ref[2 * j] = pair[:, 0:HEAD_DIM]
        q_ref[2 * j + 1] = pair[:, HEAD_DIM:LANES]
    k = pq[:, Q_COLS:Q_COLS + KV_COLS]
    kn = k * lax.rsqrt(_segsum(k * k, bd_ref[0:KV_COLS, 0:KV_COLS]) * (1.0 / HEAD_DIM) + EPS) * kg_ref[...]
    kt = rope(kn).T.astype(BF16)
    vv = pq[:, Q_COLS + KV_COLS:QKV_W].astype(BF16)
    for j in range(ATT_KV_HEADS):
        k_ref[j] = kt[j * HEAD_DIM:(j + 1) * HEAD_DIM, :]
        v_ref[j] = vv[:, j * HEAD_DIM:(j + 1) * HEAD_DIM]
    zx = jnp.dot(h, w_ref[:, QKV_W:QKV_W + ZX_W], preferred_element_type=F32)
    z_ref[...] = zx[:, 0:SSD_WIDTH]
    xbc_ref[...] = zx[:, SSD_WIDTH:ZX_W]
    hy_t = lax.dot_general(wh_ref[...], h, (((1,), (1,)), ((), ())), preferred_element_type=F32)
    is_lat = pl.program_id(0) < n_lat_tiles

    @pl.when(is_lat)
    def _():
        hyl_ref[...] = hy_t

    @pl.when(jnp.logical_not(is_lat))
    def _():
        ctx_len = hyc_ref.shape[2]
        for k in range(hyc_ref.shape[0]):
            hyc_ref[k] = hy_t[:, k * ctx_len:(k + 1) * ctx_len]
    dt_ref[...] = jnp.dot(h, w_ref[:, QKV_W + ZX_W:PROJ_PAD], preferred_element_type=F32)


def _inproj(xa, mods, g1, w_cat, w_hy_t, cos_t, sin_t, qg, kg, bd, n_batch, seq_len, ctx_len):
    t, d = xa.shape
    tm = INPROJ_TILE
    tpb = seq_len // tm
    n_lat = n_batch * tpb
    bpt = tm // ctx_len
    rope_idx = lambda i: (jnp.where(i < n_lat, i % tpb, tpb), 0)
    row = lambda w: pl.BlockSpec((tm, w), lambda i: (i, 0))
    heads = lambda nh: pl.BlockSpec((nh, tm, HEAD_DIM), lambda i: (0, i, 0))
    const = lambda a: pl.BlockSpec(a.shape, lambda i: (0,) * a.ndim)
    lat_tile = lambda i: jnp.minimum(i, n_lat - 1)
    hy_lat = pl.BlockSpec((None, HY_COLS, tm), lambda i: (lat_tile(i) // tpb, 0, lat_tile(i) % tpb))
    hy_ctx = pl.BlockSpec((bpt, HY_COLS, ctx_len), lambda i: (jnp.maximum(i - n_lat, 0), 0, 0))
    return pl.pallas_call(
        functools.partial(_inproj_kernel, n_lat_tiles=n_lat),
        grid=(t // tm,),
        in_specs=[row(d),
                  pl.BlockSpec((1, 6, d), _mod_index(tpb, n_batch)),
                  const(g1), const(w_cat), const(w_hy_t),
                  pl.BlockSpec((tm, LANES), rope_idx), pl.BlockSpec((tm, LANES), rope_idx),
                  const(qg), const(kg), const(bd)],
        out_specs=[heads(ATT_HEADS), pl.BlockSpec((ATT_KV_HEADS, HEAD_DIM, tm), lambda i: (0, 0, i)),
                   heads(ATT_KV_HEADS), row(SSD_WIDTH), row(SSD_XBC_COLS),
                   hy_lat, hy_ctx, row(LANES)],
        out_shape=[jax.ShapeDtypeStruct((ATT_HEADS, t, HEAD_DIM), BF16),
                   jax.ShapeDtypeStruct((ATT_KV_HEADS, HEAD_DIM, t), BF16),
                   jax.ShapeDtypeStruct((ATT_KV_HEADS, t, HEAD_DIM), BF16),
                   jax.ShapeDtypeStruct((t, SSD_WIDTH), F32),
                   jax.ShapeDtypeStruct((t, SSD_XBC_COLS), F32),
                   jax.ShapeDtypeStruct((n_batch, HY_COLS, seq_len), F32),
                   jax.ShapeDtypeStruct((n_batch, HY_COLS, ctx_len), F32),
                   jax.ShapeDtypeStruct((t, LANES), F32)],
        compiler_params=_cparams(1),
        name="inproj",
    )(xa, mods, g1, w_cat, w_hy_t, cos_t, sin_t, qg, kg, bd)


def _attn_kernel(sink_ref, q_ref, *refs, n_q, band):
    if band:
        k_ref, v_ref, kc_ref, vc_ref, o_ref, bias_ref = refs
        seq_len = v_ref.shape[1]
        assert n_q >= 3
    else:
        kc_ref, vc_ref, o_ref = refs
    qb = ATT_BLOCK
    rows = ATT_GROUP * qb
    row_id = lax.broadcasted_iota(jnp.int32, (rows, 1), 0)
    nt = (((1,), (1,)), ((), ()))

    if band:
        @pl.when(pl.program_id(0) == 0)
        def _():
            rel0 = (lax.broadcasted_iota(jnp.int32, (rows, band), 1)
                    - lax.broadcasted_iota(jnp.int32, (rows, band), 0) % qb)
            for var in range(3):
                bias_ref[var] = jnp.where(jnp.abs(rel0 - var * WINDOW) <= WINDOW, 0.0, -jnp.inf)

    for j in range(ATT_KV_HEADS):
        kc = kc_ref[j]
        vc = vc_ref[j]
        snk = jnp.zeros((rows, 1), F32)
        for g in range(ATT_GROUP):
            snk = jnp.where(row_id // qb == g, sink_ref[ATT_GROUP * j + g], snk)

        def block(i, carry, j=j, kc=kc, vc=vc, snk=snk):
            q0 = pl.multiple_of(i * qb, qb)
            qh = jnp.concatenate([q_ref[ATT_GROUP * j + g, pl.ds(q0, qb), :] for g in range(ATT_GROUP)],
                                 axis=0)
            s_ctx = jnp.dot(qh, kc, preferred_element_type=F32)
            m = jnp.maximum(jnp.max(s_ctx, axis=-1, keepdims=True), snk)
            if band:
                k0 = pl.multiple_of(jnp.clip(q0 - WINDOW, 0, seq_len - band), qb)
                var = jnp.where(i == 0, 0, jnp.where(i == n_q - 1, 2, 1))
                s_loc = jnp.dot(qh, k_ref[j, :, pl.ds(k0, band)], preferred_element_type=F32) + bias_ref[var]
                m = jnp.maximum(m, jnp.max(s_loc, axis=-1, keepdims=True))
            p_ctx = jnp.exp(s_ctx - m)
            den = jnp.sum(p_ctx, axis=-1, keepdims=True) + jnp.exp(snk - m)
            o = jnp.dot(p_ctx.astype(BF16), vc, preferred_element_type=F32)
            if band:
                p_loc = jnp.exp(s_loc - m)
                den = den + jnp.sum(p_loc, axis=-1, keepdims=True)
                o = o + jnp.dot(p_loc.astype(BF16), v_ref[j, pl.ds(k0, band), :], preferred_element_type=F32)
            o = o / den
            for g in range(ATT_GROUP):
                c0 = (ATT_GROUP * j + g) * HEAD_DIM
                o_ref[pl.ds(q0, qb), c0:c0 + HEAD_DIM] = o[g * qb:(g + 1) * qb, :]
            return carry

        lax.fori_loop(0, n_q, block, 0, unroll=4)


def _attention(sinks, q, k, v, n_batch, seq_len, ctx_len, latent):
    ctx_blk0 = n_batch * seq_len // ctx_len
    kc_spec = pl.BlockSpec((ATT_KV_HEADS, HEAD_DIM, ctx_len), lambda b: (0, 0, ctx_blk0 + b))
    vc_spec = pl.BlockSpec((ATT_KV_HEADS, ctx_len, HEAD_DIM), lambda b: (0, ctx_blk0 + b, 0))
    smem = pl.BlockSpec(memory_space=pltpu.SMEM)
    scratch = []
    if latent:
        rows = seq_len
        band = ATT_BLOCK + 2 * WINDOW
        in_specs = [smem, pl.BlockSpec((ATT_HEADS, rows, HEAD_DIM), lambda b: (0, b, 0)),
                    pl.BlockSpec((ATT_KV_HEADS, HEAD_DIM, rows), lambda b: (0, 0, b)),
                    pl.BlockSpec((ATT_KV_HEADS, rows, HEAD_DIM), lambda b: (0, b, 0)), kc_spec, vc_spec]
        args = (sinks, q, k, v, k, v)
        scratch = [pltpu.VMEM((3, ATT_GROUP * ATT_BLOCK, band), F32)]
    else:
        rows = ctx_len
        band = 0
        in_specs = [smem, pl.BlockSpec((ATT_HEADS, rows, HEAD_DIM), lambda b: (0, ctx_blk0 + b, 0)),
                    kc_spec, vc_spec]
        args = (sinks, q, k, v)
    return pl.pallas_call(
        functools.partial(_attn_kernel, n_q=rows // ATT_BLOCK, band=band),
        grid=(n_batch,),
        in_specs=in_specs,
        out_specs=pl.BlockSpec((rows, ATT_WIDTH), lambda b: (b, 0)),
        out_shape=jax.ShapeDtypeStruct((n_batch * rows, ATT_WIDTH), F32),
        scratch_shapes=scratch,
        compiler_params=_cparams(1),
        name="attn_latent" if latent else "attn_ctx",
    )(*args)


def _merge_kernel(*refs, n_lat_tiles, with_ctx):
    if with_ctx:
        (attl_ref, attc_ref, syl_ref, syc_ref, hyl_ref, hyc_ref, z_ref, x_ref, mod_ref,
         ga_ref, gs_ref, gh_ref, bd_ref, w_ref, o_ref) = refs
        is_lat = pl.program_id(0) < n_lat_tiles
        att = jnp.where(is_lat, attl_ref[...], attc_ref[...])
        sy = jnp.where(is_lat, syl_ref[...], syc_ref[...])
        hy_ctx = jnp.concatenate([hyc_ref[k] for k in range(hyc_ref.shape[0])], axis=1)
        hy_t = jnp.where(is_lat, hyl_ref[...], hy_ctx)
    else:
        attl_ref, syl_ref, hyl_ref, z_ref, x_ref, mod_ref, ga_ref, gs_ref, gh_ref, bd_ref, w_ref, o_ref = refs
        att = attl_ref[...]
        sy = syl_ref[...]
        hy_t = hyl_ref[...]
    a = att * lax.rsqrt(jnp.mean(att * att, axis=-1, keepdims=True) + EPS) * ga_ref[...]
    s = sy * _silu(z_ref[...])
    s = s * lax.rsqrt(jnp.mean(s * s, axis=-1, keepdims=True) + EPS) * gs_ref[...]
    hy = hy_t.T
    hn = hy * lax.rsqrt(_segsum(hy * hy, bd_ref[...]) * (1.0 / (HYENA_WIDTH // HYENA_GROUPS)) + EPS) * gh_ref[...]
    y = jnp.dot(a.astype(BF16), w_ref[0:ATT_WIDTH, :], preferred_element_type=F32)
    y = y + jnp.dot(s.astype(BF16), w_ref[ATT_WIDTH:ATT_WIDTH + SSD_WIDTH, :], preferred_element_type=F32)
    y = y + jnp.dot(hn.astype(BF16), w_ref[ATT_WIDTH + SSD_WIDTH:, :], preferred_element_type=F32)
    o_ref[...] = x_ref[...] + mod_ref[0, 2:3, :] * y


def _merge(att, sy, hy, z, xa, mods, ga, gs, gh, bd, w_out, n_batch, seq_len):
    d = xa.shape[1]
    tm = TOKEN_TILE
    nl = att[0].shape[0] // tm
    with_ctx = att[1] is not None
    t = att[0].shape[0] + (att[1].shape[0] if with_ctx else 0)
    row = lambda w: pl.BlockSpec((tm, w), lambda i: (i, 0))
    lat_row = lambda w: pl.BlockSpec((tm, w), lambda i: (jnp.minimum(i, nl - 1), 0))
    ctx_row = lambda w: pl.BlockSpec((tm, w), lambda i: (jnp.maximum(i - nl, 0), 0))
    const = lambda a: pl.BlockSpec(a.shape, lambda i: (0,) * a.ndim)
    tpb = seq_len // tm
    lat_tile = lambda i: jnp.minimum(i, nl - 1)
    hy_lat = pl.BlockSpec((None, HYENA_WIDTH, tm), lambda i: (lat_tile(i) // tpb, 0, lat_tile(i) % tpb))
    if with_ctx:
        ctx_len = hy[1].shape[2]
        streams = [att[0], att[1], sy[0], sy[1], hy[0], hy[1]]
        specs = [lat_row(ATT_WIDTH), ctx_row(ATT_WIDTH), lat_row(SSD_WIDTH), ctx_row(SSD_WIDTH), hy_lat,
                 pl.BlockSpec((tm // ctx_len, HYENA_WIDTH, ctx_len), lambda i: (jnp.maximum(i - nl, 0), 0, 0))]
    else:
        streams = [att[0], sy[0], hy[0]]
        specs = [row(ATT_WIDTH), row(SSD_WIDTH), hy_lat]
    return pl.pallas_call(
        functools.partial(_merge_kernel, n_lat_tiles=nl, with_ctx=with_ctx),
        grid=(t // tm,),
        in_specs=specs + [row(SSD_WIDTH), row(d),
                          pl.BlockSpec((1, 6, d), _mod_index(seq_len // tm, n_batch)),
                          const(ga), const(gs), const(gh), const(bd), const(w_out)],
        out_specs=row(d),
        out_shape=jax.ShapeDtypeStruct((t, d), F32),
        compiler_params=_cparams(1),
        name="merge_outproj",
    )(*streams, z, xa, mods, ga, gs, gh, bd, w_out)


def _swiglu_accumulate(h, wg_ref, wu_ref, wd_ref, acc_ref):
    for c in range(N_FFN_CHUNKS):
        cols = slice(c * FFN_CHUNK, (c + 1) * FFN_CHUNK)
        g = jnp.dot(h, wg_ref[:, cols], preferred_element_type=F32)
        u = jnp.dot(h, wu_ref[:, cols], preferred_element_type=F32)
        a = (_silu(g) * u).astype(BF16)
        part = jnp.dot(a, wd_ref[cols, :], preferred_element_type=F32)
        if c == 0:
            acc_ref[...] = part
        else:
            acc_ref[...] += part


def _ffn_kernel(x_ref, mod_ref, g2_ref, wg_ref, wu_ref, wd_ref, o_ref, acc_ref):
    x = x_ref[...]
    h = _modnorm(x, g2_ref[...], mod_ref[0, 4:5, :], mod_ref[0, 3:4, :]).astype(BF16)
    _swiglu_accumulate(h, wg_ref, wu_ref, wd_ref, acc_ref)
    o_ref[...] = x + mod_ref[0, 5:6, :] * acc_ref[...]


def _ffn(xa, mods, g2, wg, wu, wd, n_batch, seq_len):
    t, d = xa.shape
    tm = TOKEN_TILE
    row = pl.BlockSpec((tm, d), lambda i: (i, 0))
    resident = lambda a: pl.BlockSpec(a.shape, lambda i: (0,) * a.ndim, pipeline_mode=pl.Buffered(1))
    return pl.pallas_call(
        _ffn_kernel,
        grid=(t // tm,),
        in_specs=[row, pl.BlockSpec((1, 6, d), _mod_index(seq_len // tm, n_batch)),
                  pl.BlockSpec(g2.shape, lambda i: (0, 0)), resident(wg), resident(wu), resident(wd)],
        out_specs=row,
        out_shape=jax.ShapeDtypeStruct((t, d), F32),
        scratch_shapes=[pltpu.VMEM((tm, d), F32)],
        compiler_params=_cparams(1),
        name="ffn",
    )(xa, mods, g2, wg, wu, wd)


def _router_kernel(x_ref, mod_ref, g2_ref, r_ref, h_ref, idx_ref, wt_ref):
    h = _modnorm(x_ref[...], g2_ref[...], mod_ref[0, 4:5, :], mod_ref[0, 3:4, :])
    h_ref[...] = h
    h_hi = h.astype(BF16)
    h_lo = (h - h_hi.astype(F32)).astype(BF16)
    both = jnp.dot(h_hi, r_ref[...], preferred_element_type=F32)
    logits = (both[:, 0:LANES] + both[:, LANES:2 * LANES]
              + jnp.dot(h_lo, r_ref[:, 0:LANES], preferred_element_type=F32))
    lane = lax.broadcasted_iota(jnp.int32, logits.shape, 1)
    neg = -jnp.inf
    l1 = jnp.where(lane < N_EXPERTS, logits, neg)
    m1 = jnp.max(l1, axis=-1, keepdims=True)
    i1 = jnp.min(jnp.where(l1 == m1, lane, LANES), axis=-1, keepdims=True)
    l2 = jnp.where(lane == i1, neg, l1)
    m2 = jnp.max(l2, axis=-1, keepdims=True)
    i2 = jnp.min(jnp.where(l2 == m2, lane, LANES), axis=-1, keepdims=True)
    e = jnp.exp(m2 - m1)
    w1 = 1.0 / (1.0 + e)
    w2 = e / (1.0 + e)
    idx_ref[...] = jnp.where(lane == 0, i1, jnp.where(lane == 1, i2, 0))
    wt_ref[...] = jnp.where(lane == 0, w1, jnp.where(lane == 1, w2, 0.0))


def _router(xa, mods, g2, r_pad, n_rows, n_batch, seq_len):
    d = xa.shape[1]
    tm = TOKEN_TILE
    row = lambda w: pl.BlockSpec((tm, w), lambda i: (i, 0))
    return pl.pallas_call(
        _router_kernel,
        grid=(n_rows // tm,),
        in_specs=[row(d), pl.BlockSpec((1, 6, d), _mod_index(seq_len // tm, n_batch)),
                  pl.BlockSpec(g2.shape, lambda i: (0, 0)), pl.BlockSpec(r_pad.shape, lambda i: (0, 0))],
        out_specs=[row(d), row(LANES), row(LANES)],
        out_shape=[jax.ShapeDtypeStruct((n_rows, d), F32),
                   jax.ShapeDtypeStruct((n_rows, LANES), jnp.int32),
                   jax.ShapeDtypeStruct((n_rows, LANES), F32)],
        compiler_params=_cparams(1),
        name="moe_router",
    )(xa, mods, g2, r_pad)


def _row_copy(src, src_row, dst, dst_row, sem):
    return pltpu.make_async_copy(src.at[pl.ds(src_row, 1), :], dst.at[pl.ds(dst_row, 1), :], sem)


DMA_ISSUE_UNROLL = 8


def _idx_copy(dest_hbm, dest_smem, sem_idx, tile, slot):
    n = dest_hbm.shape[1]
    half = dest_smem.at[pl.ds(pl.multiple_of(slot * n, n), n)]
    return pltpu.make_async_copy(dest_hbm.at[tile], half, sem_idx.at[slot])


def _dispatch_kernel(pad_tile_ref, dest_hbm, h_ref, xs_out, dest_smem, zeros, sem_idx, sem_rows, sem_zero):
    i = pl.program_id(0)
    n = pl.num_programs(0)
    tm = h_ref.shape[0]
    slot = i % 2

    @pl.when(i == 0)
    def _():
        zeros[...] = jnp.zeros_like(zeros)

        def zero_copy(e):
            return pltpu.make_async_copy(zeros, xs_out.at[pl.ds(pl.multiple_of(pad_tile_ref[e], tm), tm), :],
                                         sem_zero)

        for e in range(2 * N_EXPERTS):
            @pl.when(pad_tile_ref[e] >= 0)
            def _(e=e):
                zero_copy(e).start()
        for e in range(2 * N_EXPERTS):
            @pl.when(pad_tile_ref[e] >= 0)
            def _(e=e):
                zero_copy(e).wait()
        _idx_copy(dest_hbm, dest_smem, sem_idx, 0, 0).start()

    @pl.when(i + 1 < n)
    def _():
        _idx_copy(dest_hbm, dest_smem, sem_idx, i + 1, 1 - slot).start()

    _idx_copy(dest_hbm, dest_smem, sem_idx, i, slot).wait()

    base = slot * (2 * tm)

    def issue(r, carry):
        _row_copy(h_ref, r, xs_out, dest_smem[base + 2 * r], sem_rows).start()
        _row_copy(h_ref, r, xs_out, dest_smem[base + 2 * r + 1], sem_rows).start()
        return carry

    lax.fori_loop(0, tm, issue, 0, unroll=DMA_ISSUE_UNROLL)
    for _ in range(2):
        pltpu.make_async_copy(h_ref, xs_out.at[pl.ds(0, tm), :], sem_rows).wait()


def _dispatch(pad_tile, dest, h, n_slots):
    n_rows, d = h.shape
    tm = TOKEN_TILE
    assert tm == EXPERT_TILE
    return pl.pallas_call(
        _dispatch_kernel,
        grid_spec=pltpu.PrefetchScalarGridSpec(
            num_scalar_prefetch=1,
            grid=(n_rows // tm,),
            in_specs=[pl.BlockSpec(memory_space=pl.ANY), pl.BlockSpec((tm, d), lambda i, pt: (i, 0))],
            out_specs=pl.BlockSpec(memory_space=pl.ANY),
            scratch_shapes=[pltpu.SMEM((4 * tm,), jnp.int32), pltpu.VMEM((tm, d), F32),
                            pltpu.SemaphoreType.DMA((2,)), pltpu.SemaphoreType.DMA(()),
                            pltpu.SemaphoreType.DMA(())]),
        out_shape=jax.ShapeDtypeStruct((n_slots, d), F32),
        compiler_params=_cparams(1),
        name="moe_dispatch",
    )(pad_tile, dest.reshape(n_rows // tm, 2 * tm), h)


def _expert_kernel(te_ref, nused_ref, xs_ref, wg_ref, wu_ref, wd_ref, o_ref, acc_ref):
    del te_ref
    live = pl.program_id(0) < nused_ref[0]

    @pl.when(live)
    def _():
        _swiglu_accumulate(xs_ref[...].astype(BF16), wg_ref, wu_ref, wd_ref, acc_ref)
        o_ref[...] = acc_ref[...]

    @pl.when(jnp.logical_not(live))
    def _():
        o_ref[...] = jnp.zeros_like(o_ref)


def _experts(tile_expert, n_used, xs, wg, wu, wd):
    s, d = xs.shape
    tm = EXPERT_TILE
    row = pl.BlockSpec((tm, d), lambda i, te, nu: (i, 0))
    xs_row = pl.BlockSpec((tm, d), lambda i, te, nu: (jnp.minimum(i, nu[0] - 1), 0))
    wspec = lambda a: pl.BlockSpec((None,) + a.shape[1:], lambda i, te, nu: (te[i], 0, 0))
    return pl.pallas_call(
        _expert_kernel,
        grid_spec=pltpu.PrefetchScalarGridSpec(
            num_scalar_prefetch=2,
            grid=(s // tm,),
            in_specs=[xs_row, wspec(wg), wspec(wu), wspec(wd)],
            out_specs=row,
            scratch_shapes=[pltpu.VMEM((tm, d), F32)]),
        out_shape=jax.ShapeDtypeStruct((s, d), F32),
        compiler_params=_cparams(1),
        name="moe_experts",
    )(tile_expert, n_used, xs, wg, wu, wd)


def _combine_kernel(dest_hbm, eo_hbm, x_ref, wt_ref, mod_ref, o_ref, dest_smem, buf, sem_idx, sem_rows):
    i = pl.program_id(0)
    n = pl.num_programs(0)
    tm = x_ref.shape[0]
    slot = i % 2

    def gather(s):
        base = s * (2 * tm)

        def issue(r, carry):
            _row_copy(eo_hbm, dest_smem[base + 2 * r], buf.at[s, 0], r, sem_rows.at[s]).start()
            _row_copy(eo_hbm, dest_smem[base + 2 * r + 1], buf.at[s, 1], r, sem_rows.at[s]).start()
            return carry

        lax.fori_loop(0, tm, issue, 0, unroll=DMA_ISSUE_UNROLL)

    @pl.when(i == 0)
    def _():
        first = _idx_copy(dest_hbm, dest_smem, sem_idx, 0, 0)
        first.start()
        first.wait()
        gather(0)

        @pl.when(n > 1)
        def _():
            _idx_copy(dest_hbm, dest_smem, sem_idx, 1, 1).start()

    @pl.when(i + 1 < n)
    def _():
        _idx_copy(dest_hbm, dest_smem, sem_idx, i + 1, 1 - slot).wait()

        @pl.when(i + 2 < n)
        def _():
            _idx_copy(dest_hbm, dest_smem, sem_idx, i + 2, slot).start()

        gather(1 - slot)

    for k in range(2):
        pltpu.make_async_copy(eo_hbm.at[pl.ds(0, tm), :], buf.at[slot, k], sem_rows.at[slot]).wait()
    wt = wt_ref[...]
    y = wt[:, 0:1] * buf[slot, 0] + wt[:, 1:2] * buf[slot, 1]
    o_ref[...] = x_ref[...] + mod_ref[0, 5:6, :] * y


def _combine(dest, eo, xa, wts, mods, n_rows, n_batch, seq_len):
    d = xa.shape[1]
    tm = TOKEN_TILE
    row = lambda w: pl.BlockSpec((tm, w), lambda i: (i, 0))
    return pl.pallas_call(
        _combine_kernel,
        grid=(n_rows // tm,),
        in_specs=[pl.BlockSpec(memory_space=pl.ANY), pl.BlockSpec(memory_space=pl.ANY), row(d), row(LANES),
                  pl.BlockSpec((1, 6, d), _mod_index(seq_len // tm, n_batch))],
        out_specs=row(d),
        out_shape=jax.ShapeDtypeStruct((n_rows, d), F32),
        scratch_shapes=[pltpu.SMEM((4 * tm,), jnp.int32), pltpu.VMEM((2, 2, tm, d), F32),
                        pltpu.SemaphoreType.DMA((2,)), pltpu.SemaphoreType.DMA((2,))],
        compiler_params=_cparams(1),
        name="moe_combine",
    )(dest.reshape(n_rows // tm, 2 * tm), eo, xa, wts, mods)


def _moe(xa, mods, g2, r_pad, wg, wu, wd, n_rows, n_batch, seq_len):
    h, idx, wts = _router(xa, mods, g2, r_pad, n_rows, n_batch, seq_len)
    tm = EXPERT_TILE
    e_flat = idx[:, :2].reshape(-1)
    onehot = (e_flat[:, None] == jnp.arange(N_EXPERTS, dtype=jnp.int32)[None, :]).astype(jnp.int32)
    csum = jnp.cumsum(onehot, axis=0)
    counts = csum[-1]
    rank = jnp.sum(onehot * csum, axis=1) - 1
    padded = ((counts + tm - 1) // tm) * tm
    ends = jnp.cumsum(padded)
    starts = ends - padded
    dest = (jnp.sum(onehot * starts[None, :], axis=1) + rank).astype(jnp.int32)
    n_slots = 2 * n_rows + N_EXPERTS * tm
    tile_start = jnp.arange(n_slots // tm, dtype=jnp.int32) * tm
    tile_expert = jnp.minimum(jnp.sum((tile_start[:, None] >= ends[None, :]).astype(jnp.int32), axis=1),
                              N_EXPERTS - 1).astype(jnp.int32)
    n_used = (ends[-1:] // tm).astype(jnp.int32)
    tail = ends[-1] + jnp.arange(N_EXPERTS, dtype=ends.dtype) * tm
    pad_tile = jnp.concatenate([jnp.where(padded > 0, ends - tm, -1),
                                jnp.where(tail < n_slots, tail, -1)]).astype(jnp.int32)
    xs = _dispatch(pad_tile, dest, h, n_slots)
    eo = _experts(tile_expert, n_used, xs, wg, wu, wd)
    return _combine(dest, eo, xa, wts, mods, n_rows, n_batch, seq_len)


def _softplus(v):
    return jnp.maximum(v, 0.0) + jnp.log1p(jnp.exp(-jnp.abs(v)))


def _ssd_kernel(xl_ref, dl_ref, xc_ref, dc_ref, cw_ref, cb_ref, dtb_ref, alogc_ref, dsk_ref, *rest, want_ctx):
    if want_ctx:
        yl_ref, yc_ref, *rest = rest
    else:
        yl_ref, *rest = rest
        yc_ref = None
    *lat_s, st_ref = rest
    lat_s, ctx_s = lat_s[:6], lat_s[6:]
    ck = SSD_CHUNK
    hp = SSD_HEAD_DIM
    ns = SSD_STATE
    n_col = 2 * SSD_HEADS
    hi = lax.Precision.HIGHEST
    row = lax.broadcasted_iota(jnp.int32, (ck, 1), 0)
    li = lax.broadcasted_iota(jnp.int32, (ck, ck), 0)
    si = lax.broadcasted_iota(jnp.int32, (ck, ck), 1)
    masks = (si <= li, si >= li)
    tris = (masks[0].astype(F32), masks[1].astype(F32))
    a_col = -jnp.exp(alogc_ref[...])
    dskip = dsk_ref[...]

    def prep(raw_ref, dtraw_ref, xs_s, dtt_s, lar_s, lac_s, bt_s, sc_s, y_ref):
        n = raw_ref.shape[0]
        nk = n // ck

        def body(k, carry):
            r0 = pl.multiple_of(k * ck, ck)
            a = raw_ref[pl.ds(r0, ck), :]
            top = raw_ref[pl.ds(pl.multiple_of(jnp.maximum(r0 - 8, 0), 8), 8), :][7:8, :]
            bot = raw_ref[pl.ds(pl.multiple_of(jnp.minimum(r0 + ck, n - 8), 8), 8), :][0:1, :]
            top = jnp.where(k > 0, top, 0.0)
            bot = jnp.where(k < nk - 1, bot, 0.0)
            prev = jnp.where(row == 0, top, pltpu.roll(a, 1, 0))
            nxt = jnp.where(row == ck - 1, bot, pltpu.roll(a, ck - 1, 0))
            xs = _silu(prev * cw_ref[0:1, :] + a * cw_ref[1:2, :] + nxt * cw_ref[2:3, :] + cb_ref[...])
            xs_s[pl.ds(r0, ck), :] = xs
            dt = _softplus(dtraw_ref[pl.ds(r0, ck), :] + dtb_ref[...])
            dt_t = dt.T[0:n_col, :]
            dtt_s[:, pl.ds(r0, ck)] = dt_t
            for dr in range(2):
                la_r = jnp.dot(dt_t * a_col, tris[1 - dr], precision=hi, preferred_element_type=F32)
                lar_s[dr, :, pl.ds(r0, ck)] = la_r
                if y_ref is not None:
                    lac_s[dr, pl.ds(r0, ck), :] = jnp.concatenate(
                        [la_r, jnp.zeros((ck - n_col, ck), F32)], axis=0).T
            for g in range(SSD_GROUPS):
                b_t = xs[:, SSD_WIDTH + g * ns:SSD_WIDTH + (g + 1) * ns].T
                bt_s[g, :, pl.ds(r0, ck)] = b_t
                if y_ref is not None:
                    c0 = SSD_WIDTH + SSD_GROUPS * ns + g * ns
                    sc_s[g, pl.ds(r0, ck), :] = jnp.dot(xs[:, c0:c0 + ns].astype(BF16), b_t.astype(BF16),
                                                        preferred_element_type=F32)
            if y_ref is not None:
                y_ref[pl.ds(r0, ck), :] = xs[:, 0:SSD_WIDTH] * dskip
            return carry

        lax.fori_loop(0, nk, body, 0, unroll=4)

    def run(xs_s, dtt_s, lar_s, lac_s, bt_s, sc_s, y_ref):
        nk = xs_s.shape[0] // ck

        def one(kk, dr):
            r0 = pl.multiple_of(kk * ck, ck)
            xc = xs_s[pl.ds(r0, ck), :]
            dtr = dtt_s[:, pl.ds(r0, ck)]
            la_r = lar_s[dr, :, pl.ds(r0, ck)]
            la_end = la_r[:, ck - 1:ck] if dr == 0 else la_r[:, 0:1]
            if y_ref is not None:
                la_c = lac_s[dr, pl.ds(r0, ck), :]
            ys = []
            for g in range(SSD_GROUPS):
                b_t = bt_s[g, :, pl.ds(r0, ck)]
                c0 = SSD_WIDTH + SSD_GROUPS * ns + g * ns
                c_g = xc[:, c0:c0 + ns].astype(BF16)
                if y_ref is not None:
                    scores = sc_s[g, pl.ds(r0, ck), :]
                for hh in range(SSD_HEADS // SSD_GROUPS):
                    h = g * (SSD_HEADS // SSD_GROUPS) + hh
                    col = dr * SSD_HEADS + h
                    xh = xc[:, h * hp:(h + 1) * hp].astype(BF16)
                    dt_row = dtr[col:col + 1, :]
                    le = la_end[col:col + 1, :]
                    st = st_ref[col]
                    if y_ref is not None:
                        la_col = la_c[:, col:col + 1]
                        decay = jnp.exp(jnp.where(masks[dr], la_col - la_r[col:col + 1, :], -jnp.inf))
                        y = jnp.dot((scores * decay * dt_row).astype(BF16), xh, preferred_element_type=F32)
                        y = y + jnp.dot(c_g, st.astype(BF16), preferred_element_type=F32) * jnp.exp(la_col)
                        ys.append(y)
                    bw = (b_t * (dt_row * jnp.exp(le - la_r[col:col + 1, :]))).astype(BF16)
                    st_ref[col] = st * jnp.exp(le) + jnp.dot(bw, xh, preferred_element_type=F32)
            if y_ref is not None:
                y_ref[pl.ds(r0, ck), :] += jnp.concatenate(ys, axis=1)

        def body(k, carry):
            one(k, 0)
            one(nk - 1 - k, 1)
            return carry

        lax.fori_loop(0, nk, body, 0, unroll=2)

    prep(xl_ref, dl_ref, *lat_s, yl_ref)
    prep(xc_ref, dc_ref, *ctx_s, yc_ref)
    st_ref[...] = jnp.zeros_like(st_ref)
    run(*ctx_s, yc_ref)
    run(*lat_s, yl_ref)


def _ssd(xbc, dtp, conv_w, conv_b, dt_bias, a_log, d_skip, n_batch, seq_len, ctx_len, want_ctx):
    ctx0 = n_batch * seq_len // ctx_len
    pad = lambda v: jnp.pad(v.reshape(1, -1), ((0, 0), (0, LANES - v.size)))
    lat = lambda w: pl.BlockSpec((seq_len, w), lambda b: (b, 0))
    ctx = lambda w: pl.BlockSpec((ctx_len, w), lambda b: (ctx0 + b, 0))
    const = lambda a: pl.BlockSpec(a.shape, lambda b: (0,) * a.ndim)
    consts = (conv_w, conv_b.reshape(1, -1), pad(dt_bias), a_log.reshape(-1, 1),
              jnp.repeat(d_skip, SSD_HEAD_DIM).reshape(1, -1))
    stream_scratch = lambda n: [pltpu.VMEM((n, SSD_XBC_COLS), F32), pltpu.VMEM((2 * SSD_HEADS, n), F32),
                                pltpu.VMEM((2, 2 * SSD_HEADS, n), F32), pltpu.VMEM((2, n, SSD_CHUNK), F32),
                                pltpu.VMEM((SSD_GROUPS, SSD_STATE, n), F32),
                                pltpu.VMEM((SSD_GROUPS, n, SSD_CHUNK), F32)]
    out_specs = [lat(SSD_WIDTH)]
    out_shape = [jax.ShapeDtypeStruct((n_batch * seq_len, SSD_WIDTH), F32)]
    if want_ctx:
        out_specs.append(pl.BlockSpec((ctx_len, SSD_WIDTH), lambda b: (b, 0)))
        out_shape.append(jax.ShapeDtypeStruct((n_batch * ctx_len, SSD_WIDTH), F32))
    return pl.pallas_call(
        functools.partial(_ssd_kernel, want_ctx=want_ctx),
        grid=(n_batch,),
        in_specs=[lat(SSD_XBC_COLS), lat(LANES), ctx(SSD_XBC_COLS), ctx(LANES)] + [const(a) for a in consts],
        out_specs=out_specs,
        out_shape=out_shape,
        scratch_shapes=[*stream_scratch(seq_len), *stream_scratch(ctx_len),
                        pltpu.VMEM((2 * SSD_HEADS, SSD_STATE, SSD_HEAD_DIM), F32)],
        compiler_params=_cparams(1),
        name="ssd",
    )(xbc, dtp, xbc, dtp, *consts)


HY_BLOCK = 256
HY_CH_STEP = 8


def _hyena_tables(seq_len):
    f32 = np.float32
    nj = 2 * seq_len
    lag = np.arange(nj, dtype=np.int32) - seq_len
    dist = np.abs(lag)
    pos = np.minimum(dist, seq_len - 1)
    t = np.linspace(0.0, 1.0, seq_len, dtype=f32)[pos]
    w = (f32(2.0 * math.pi / seq_len) * np.arange(seq_len, dtype=f32))[pos]
    bands = (HYENA_POS_DIM - 1) // 2
    freqs = np.linspace(1e-4, bands - 1, bands, dtype=f32)[None, :]
    ang = (freqs * w[:, None]).astype(f32)
    z = np.concatenate([t[:, None], np.cos(ang), -np.sin(ang)], axis=-1).astype(f32)
    zt = np.pad(z.T, ((0, (-HYENA_POS_DIM) % 8), (0, 0)))
    deltas = np.abs(np.linspace(math.log(HYENA_DECAY_TARGET) / HYENA_SLOW_DECAY,
                                math.log(HYENA_DECAY_TARGET) / HYENA_FAST_DECAY, HYENA_WIDTH, dtype=f32))
    dec = (np.exp(-t[None, :] * deltas[:, None]) * (dist < seq_len)[None, :]).astype(f32)
    fwd = (lag >= 0).astype(f32)[None, :]
    return jnp.asarray(zt), jnp.asarray(dec), jnp.asarray(fwd)


def _hyfilt_kernel(zt_ref, dec_ref, fwd_ref, w1_ref, b1_ref, f1_ref, w2_ref, b2_ref, f2_ref, w3_ref, o_ref):
    hi = lax.Precision.HIGHEST
    h = jnp.sin(f1_ref[...] * (jnp.dot(w1_ref[...], zt_ref[...], precision=hi, preferred_element_type=F32)
                               + b1_ref[...]))
    h = jnp.sin(f2_ref[...] * (jnp.dot(w2_ref[...], h, precision=hi, preferred_element_type=F32) + b2_ref[...]))
    hw = jnp.dot(w3_ref[...], h, precision=hi, preferred_element_type=F32)
    fwd = fwd_ref[...] > 0.5
    dec = dec_ref[...]
    nw = HYENA_WIDTH
    for o in range(HYENA_ORDER):
        o_ref[o] = jnp.where(fwd, hw[o * nw:(o + 1) * nw], hw[(HYENA_ORDER + o) * nw:(HYENA_ORDER + o + 1) * nw]) * dec


def _hyena_filters(tables, w1, b1, f1, w2, b2, f2, w3):
    zt, dec, fwd = tables
    nj = zt.shape[1]
    tj = 512
    col = lambda v: v.reshape(-1, 1)
    w1t = jnp.pad(w1.T, ((0, 0), (0, zt.shape[0] - w1.shape[0])))
    consts = (w1t, col(b1), col(f1), w2.T, col(b2), col(f2), w3.T)
    lanes = lambda a: pl.BlockSpec((a.shape[0], tj), lambda j: (0, j))
    const = lambda a: pl.BlockSpec(a.shape, lambda j: (0, 0))
    return pl.pallas_call(
        _hyfilt_kernel,
        grid=(nj // tj,),
        in_specs=[lanes(zt), lanes(dec), lanes(fwd)] + [const(a) for a in consts],
        out_specs=pl.BlockSpec((HYENA_ORDER, HYENA_WIDTH, tj), lambda j: (0, 0, j)),
        out_shape=jax.ShapeDtypeStruct((HYENA_ORDER, HYENA_WIDTH, nj), F32),
        compiler_params=_cparams(1),
        name="hyena_filters",
    )(zt, dec, fwd, *consts)


def _hyconv_kernel(cw_ref, cb_ref, hb_ref, v_ref, x1_ref, x2_ref, kf_ref, o_ref):
    n_b, n_ch, seq_len = v_ref.shape
    nb = seq_len // HY_BLOCK
    blk = HY_BLOCK
    c_base = pl.program_id(0) * n_ch
    lane = lax.broadcasted_iota(jnp.int32, (1, seq_len), 1)

    def sconv(x, ch):
        prev = jnp.where(lane == 0, 0.0, pltpu.roll(x, 1, 1))
        nxt = jnp.where(lane == seq_len - 1, 0.0, pltpu.roll(x, seq_len - 1, 1))
        return prev * cw_ref[0, ch] + x * cw_ref[1, ch] + nxt * cw_ref[2, ch] + cb_ref[ch]

    def long_conv(vals, kf_row):
        half = blk // 2
        skew = pltpu.roll(jnp.broadcast_to(kf_row, (half, 2 * seq_len)), 0, 1, stride=1,
                          stride_axis=0).astype(BF16)
        vb = vals.astype(BF16)
        acc = [None] * nb
        for d in range(-(nb - 1), nb):
            a0 = seq_len + d * blk
            tt = jnp.concatenate([skew[:, a0:a0 + blk], skew[:, a0 - half:a0 - half + blk]], axis=0)
            sis = list(range(max(0, -d), min(nb, nb - d)))
            lhs = [vb[:, s * blk:(s + 1) * blk] for s in sis]
            lhs = lhs[0] if len(lhs) == 1 else jnp.concatenate(lhs, axis=0)
            out = jnp.dot(lhs, tt, preferred_element_type=F32)
            for idx, s in enumerate(sis):
                piece = out[idx * n_b:(idx + 1) * n_b]
                acc[s + d] = piece if acc[s + d] is None else acc[s + d] + piece
        return acc[0] if nb == 1 else jnp.concatenate(acc, axis=1)

    def channel(cc, carry):
        ch = c_base + cc
        v = sconv(v_ref[:, cc, :], ch)
        x1 = sconv(x1_ref[:, cc, :], HYENA_WIDTH + ch)
        x2 = sconv(x2_ref[:, cc, :], 2 * HYENA_WIDTH + ch)
        z = x1 * (long_conv(v, kf_ref[0, pl.ds(cc, 1), :]) + v * hb_ref[0, ch])
        o_ref[:, cc, :] = x2 * (long_conv(z, kf_ref[1, pl.ds(cc, 1), :]) + z * hb_ref[1, ch])
        return carry

    lax.fori_loop(0, n_ch, channel, 0, unroll=4)


def _hyena_conv(hy, kf, conv_w, conv_b, hy_bias):
    n_batch, _, seq_len = hy.shape
    cs = HY_CH_STEP
    nw = HYENA_WIDTH
    stream = lambda k: pl.BlockSpec((n_batch, cs, seq_len), lambda c: (0, k * (nw // cs) + c, 0))
    smem = pl.BlockSpec(memory_space=pltpu.SMEM)
    return pl.pallas_call(
        _hyconv_kernel,
        grid=(nw // cs,),
        in_specs=[smem, smem, smem, stream(0), stream(1), stream(2),
                  pl.BlockSpec((HYENA_ORDER, cs, 2 * seq_len), lambda c: (0, c, 0))],
        out_specs=pl.BlockSpec((n_batch, cs, seq_len), lambda c: (0, c, 0)),
        out_shape=jax.ShapeDtypeStruct((n_batch, nw, seq_len), F32),
        compiler_params=_cparams(1),
        name="hyena_conv",
    )(conv_w, conv_b, hy_bias, hy, hy, hy, kf)


def _rope_tables(seq_len, extra):
    rows = seq_len // GRID_W
    row = jnp.repeat(jnp.arange(rows, dtype=F32), GRID_W)
    col = jnp.tile(jnp.arange(GRID_W, dtype=F32), rows)
    inv = ROPE_THETA ** (-jnp.arange(0, ROPE_AXIS_DIM, 2, dtype=F32) / ROPE_AXIS_DIM)
    ang = jnp.stack([row[:, None] * inv, col[:, None] * inv], axis=1)
    cos = jnp.cos(ang)
    sin = jnp.sin(ang)
    cos_h = jnp.concatenate([cos, cos], axis=-1).reshape(seq_len, HEAD_DIM)
    sin_h = jnp.concatenate([-sin, sin], axis=-1).reshape(seq_len, HEAD_DIM)
    cos_t = jnp.concatenate([jnp.tile(cos_h, (1, LANES // HEAD_DIM)), jnp.ones((extra, LANES), F32)], axis=0)
    sin_t = jnp.concatenate([jnp.tile(sin_h, (1, LANES // HEAD_DIM)), jnp.zeros((extra, LANES), F32)], axis=0)
    return cos_t, sin_t


def _block_diag_ones(n, seg):
    i = jnp.arange(n) // seg
    return (i[:, None] == i[None, :]).astype(BF16)


def kernel(x, c, ctx, c_ctx, w_ada, b_ada, norm1, norm2, w_in, w_out, q_norm, k_norm, att_sinks, att_out_norm, ssd_conv_w, ssd_conv_b, ssd_dt_bias, ssd_a_log, ssd_d, ssd_norm, hy_conv_w, hy_conv_b, hy_w1, hy_b1, hy_f1, hy_w2, hy_b2, hy_f2, hy_w3, hy_bias, hy_out_norm, ffn_w_gate, ffn_w_up, ffn_w_down, moe_router, moe_w_gate, moe_w_up, moe_w_down):
    n_batch, seq_len, d = x.shape
    ctx_len = ctx.shape[1]
    n_lat = n_batch * seq_len
    n_ctx = n_batch * ctx_len
    depth = w_in.shape[0]
    xa = jnp.concatenate([x.reshape(n_lat, d), ctx.reshape(n_ctx, d)], axis=0)

    cc = jnp.concatenate([c, c_ctx[None, :]], axis=0)
    pad_rows = (-cc.shape[0]) % 8
    cc = jnp.pad(cc, ((0, pad_rows), (0, 0)))
    mods_all = _adaln(cc, w_ada, b_ada)[:, :n_batch + 1].reshape(depth, n_batch + 1, 6, d)

    cos_t, sin_t = _rope_tables(seq_len, INPROJ_TILE)
    bd_q = _block_diag_ones(Q_COLS, HEAD_DIM)
    bd_h = _block_diag_ones(HYENA_WIDTH, HYENA_WIDTH // HYENA_GROUPS)
    hy_tab_l = _hyena_tables(seq_len)
    hy_tab_c = _hyena_tables(ctx_len)

    for i in range(depth):
        last = i == depth - 1
        j = i // 2
        mods = mods_all[i]
        wi = w_in[i]
        c_dt = QKV_W + ZX_W
        w_cat = jnp.concatenate([wi[:, :c_dt], wi[:, c_dt:c_dt + SSD_DT_COLS],
                                 jnp.zeros((d, LANES - SSD_DT_COLS), F32)], axis=1).astype(BF16)
        w_hy_t = wi[:, c_dt + SSD_DT_COLS:].T.astype(BF16)
        qg = jnp.tile(q_norm[i], Q_COLS // HEAD_DIM)[None, :]
        kg = jnp.tile(k_norm[i], KV_COLS // HEAD_DIM)[None, :]
        q, k, v, z, xbc, hy_l, hy_c, dtp = _inproj(xa, mods, norm1[i][None, :], w_cat, w_hy_t, cos_t, sin_t,
                                                   qg, kg, bd_q, n_batch, seq_len, ctx_len)

        att_l = _attention(att_sinks[i], q, k, v, n_batch, seq_len, ctx_len, True)
        ssd_out = _ssd(xbc, dtp, ssd_conv_w[i], ssd_conv_b[i], ssd_dt_bias[i], ssd_a_log[i], ssd_d[i],
                       n_batch, seq_len, ctx_len, not last)
        filt = (hy_w1[i], hy_b1[i], hy_f1[i], hy_w2[i], hy_b2[i], hy_f2[i], hy_w3[i])
        hyo_l = _hyena_conv(hy_l, _hyena_filters(hy_tab_l, *filt), hy_conv_w[i], hy_conv_b[i], hy_bias[i])
        if last:
            att = (att_l, None)
            sy = (ssd_out[0], None)
            hyo = (hyo_l, None)
            n_rows = n_lat
        else:
            att = (att_l, _attention(att_sinks[i], q, k, v, n_batch, seq_len, ctx_len, False))
            sy = tuple(ssd_out)
            hyo_c = _hyena_conv(hy_c, _hyena_filters(hy_tab_c, *filt), hy_conv_w[i], hy_conv_b[i], hy_bias[i])
            hyo = (hyo_l, hyo_c)
            n_rows = n_lat + n_ctx
        xa = _merge(att, sy, hyo, z, xa, mods, att_out_norm[i][None, :], ssd_norm[i][None, :],
                    hy_out_norm[i][None, :], bd_h, w_out[i].astype(BF16), n_batch, seq_len)

        g2 = norm2[i][None, :]
        if i % 2 == 0:
            xa = _ffn(xa, mods, g2, ffn_w_gate[j].astype(BF16), ffn_w_up[j].astype(BF16),
                      ffn_w_down[j].astype(BF16), n_batch, seq_len)
        else:
            r_full = jnp.pad(moe_router[j], ((0, 0), (0, LANES - N_EXPERTS)))
            r_hi = r_full.astype(BF16)
            r_pad = jnp.concatenate([r_hi, (r_full - r_hi.astype(F32)).astype(BF16)], axis=1)
            xa = _moe(xa, mods, g2, r_pad, moe_w_gate[j].astype(BF16), moe_w_up[j].astype(BF16),
                      moe_w_down[j].astype(BF16), n_rows, n_batch, seq_len)
    return xa[:n_lat].reshape(n_batch, seq_len, d)
```
